```python
import math
import jax, jax.numpy as jnp
from jax import lax
import numpy as np

D_MODEL = 1024
BATCH = 8
SEQ = 2048
DEPTH = 4
DEC_BATCH = 128
DEC_SEQ = 4
PAST_LEN = 16384
PAGE_SIZE = 128

M_MIXERS = 2
N_A = (DEPTH + M_MIXERS - 1) // M_MIXERS
N_B = DEPTH // M_MIXERS
SC_DIM = 768
SC_W = 3
GDN_H = 6
GDN_DK = 128
GDN_DV = 128
GDN_DIM = GDN_H * GDN_DK
GDN_CONV_W = 4
GDN_CHUNK = 64
N_MEM = 256
XH = 4
XD = 64
XDIM = XH * XD
D_FF = 2816
FFN_W = 3
ALPHA = (2.0 * DEPTH) ** 0.25
BETA_DN = (8.0 * DEPTH) ** -0.25
LN_EPS = 1e-5
RMS_EPS = 1e-6

kernel_name = "hybrid_shortconv_gdn_memxattn_convffn_step"


def layer_norm(x, g, b):
    xf = x.astype(jnp.float32)
    mu = jnp.mean(xf, -1, keepdims=True)
    var = jnp.mean(jnp.square(xf - mu), -1, keepdims=True)
    y = (xf - mu) * lax.rsqrt(var + LN_EPS)
    return (y * g.astype(jnp.float32) + b.astype(jnp.float32)).astype(x.dtype)


def l2norm(x):
    return x * lax.rsqrt(jnp.sum(jnp.square(x), -1, keepdims=True) + RMS_EPS)


def causal_dwconv(x_full, w):
    width = w.shape[0]
    t = x_full.shape[1] - (width - 1)
    out = x_full[:, 0:t] * w[0]
    for j in range(1, width):
        out = out + x_full[:, j:j + t] * w[j]
    return out


def mem_attention(qm, mem_k, mem_v):
    bsz, t, _ = qm.shape
    q = qm.reshape(bsz, t, XH, XD)
    s = jnp.einsum('bthd,bmhd->bhtm', q, mem_k).astype(jnp.float32) * (XD ** -0.5)
    p = jax.nn.softmax(s, axis=-1)
    o = jnp.einsum('bhtm,bmhd->bthd', p.astype(mem_v.dtype), mem_v)
    return o.reshape(bsz, t, XDIM).astype(qm.dtype)


def gated_delta_chunked(q, k, v, g, beta, s0):
    bsz, t, h, dk = q.shape
    dv = v.shape[-1]
    c = GDN_CHUNK if t % GDN_CHUNK == 0 else t
    n = t // c

    def blocks(a):
        a = jnp.moveaxis(a, 2, 1)
        return a.reshape(bsz, h, n, c, *a.shape[3:])

    q = blocks(q * (dk ** -0.5))
    k = blocks(k)
    v = blocks(v)
    g = blocks(g)
    beta = blocks(beta)
    gc = jnp.cumsum(g, axis=-1)
    tril = jnp.tril(jnp.ones((c, c), dtype=bool))
    strict = jnp.tril(jnp.ones((c, c), dtype=bool), -1)
    diff = gc[..., :, None] - gc[..., None, :]
    decay = jnp.exp(jnp.where(tril, diff, -jnp.inf))
    kb = k * beta[..., None]
    a = jnp.einsum('bhnid,bhnjd->bhnij', kb, k) * decay
    a = jnp.where(strict, a, 0.0) + jnp.eye(c, dtype=a.dtype)
    rhs = jnp.concatenate([v * beta[..., None], kb * jnp.exp(gc)[..., None]], axis=-1)
    sol = lax.linalg.triangular_solve(a, rhs, left_side=True, lower=True, unit_diagonal=True)
    u, w = sol[..., :dv], sol[..., dv:]
    qk = jnp.einsum('bhnid,bhnjd->bhnij', q, k) * decay
    q_dec = q * jnp.exp(gc)[..., None]
    k_dec = k * jnp.exp(gc[..., -1:] - gc)[..., None]
    g_last = gc[..., -1]

    def step(s, inp):
        qk_i, qd_i, kd_i, u_i, w_i, gl_i = inp
        v_new = u_i - jnp.einsum('bhcd,bhde->bhce', w_i, s)
        o = jnp.einsum('bhcd,bhde->bhce', qd_i, s) + jnp.einsum('bhcj,bhje->bhce', qk_i, v_new)
        s = s * jnp.exp(gl_i)[..., None, None] + jnp.einsum('bhcd,bhce->bhde', kd_i, v_new)
        return s, o

    xs = tuple(jnp.moveaxis(a_, 2, 0) for a_ in (qk, q_dec, k_dec, u, w, g_last))
    s_fin, o = lax.scan(step, s0, xs)
    o = jnp.transpose(o, (1, 0, 3, 2, 4)).reshape(bsz, t, h, dv)
    return o, s_fin


def shortconv_mixer(x, hist, w_in, w_conv, w_out, mem_k, mem_v):
    h = x @ w_in
    xin, bg, cg, qm = jnp.split(h, [SC_DIM, 2 * SC_DIM, 3 * SC_DIM], axis=-1)
    u_full = jnp.concatenate([hist.astype(x.dtype), cg * xin], axis=1)
    y = bg * causal_dwconv(u_full, w_conv)
    o_mem = mem_attention(qm, mem_k, mem_v)
    out = jnp.concatenate([y, o_mem], axis=-1) @ w_out
    return out, u_full[:, -(SC_W - 1):]


def gdn_mixer(x, hist, s0, w_in, w_conv, a_log, dt_bias, norm_w, w_out, mem_k, mem_v):
    bsz, t, _ = x.shape
    h = x @ w_in
    qkv, z, b, a, qm = jnp.split(
        h, [3 * GDN_DIM, 4 * GDN_DIM, 4 * GDN_DIM + GDN_H, 4 * GDN_DIM + 2 * GDN_H], axis=-1)
    qkv_full = jnp.concatenate([hist.astype(x.dtype), qkv], axis=1)
    qkv_c = jax.nn.silu(causal_dwconv(qkv_full, w_conv)).astype(jnp.float32)
    q, k, v = jnp.split(qkv_c, 3, axis=-1)
    q = l2norm(q.reshape(bsz, t, GDN_H, GDN_DK))
    k = l2norm(k.reshape(bsz, t, GDN_H, GDN_DK))
    v = v.reshape(bsz, t, GDN_H, GDN_DV)
    beta = jax.nn.sigmoid(b.astype(jnp.float32))
    g = -jnp.exp(a_log.astype(jnp.float32)) * jax.nn.softplus(
        a.astype(jnp.float32) + dt_bias.astype(jnp.float32))
    o, s_new = gated_delta_chunked(q, k, v, g, beta, s0.astype(jnp.float32))
    o = o * lax.rsqrt(jnp.mean(jnp.square(o), -1, keepdims=True) + RMS_EPS)
    o = o * norm_w.astype(jnp.float32) * jax.nn.silu(z.astype(jnp.float32).reshape(bsz, t, GDN_H, GDN_DV))
    o = o.reshape(bsz, t, GDN_DIM).astype(x.dtype)
    o_mem = mem_attention(qm, mem_k, mem_v)
    out = jnp.concatenate([o, o_mem], axis=-1) @ w_out
    return out, qkv_full[:, -(GDN_CONV_W - 1):], s_new.astype(s0.dtype)


def channel_mixer(x, hist, w_up, w_conv, w_down):
    h_full = jnp.concatenate([hist.astype(x.dtype), x @ w_up], axis=1)
    hc = causal_dwconv(h_full, w_conv)
    gate, up = jnp.split(hc, 2, axis=-1)
    return (jax.nn.silu(gate) * up) @ w_down, h_full[:, -(FFN_W - 1):]


def trunk(x, mem_k, mem_v, sc_hist, gdn_hist, gdn_s, ffn_hist,
          w_in_a, conv_a, w_out_a, w_in_b, conv_b, a_log, dt_bias, gdn_norm_w, w_out_b,
          ln1_g, ln1_b, ln2_g, ln2_b, w_up, w_conv_ffn, w_down):
    new_sc, new_gc, new_gs, new_ffn = [], [], [], []
    for i in range(DEPTH):
        j = i // M_MIXERS
        if i % M_MIXERS == 0:
            mix, hs = shortconv_mixer(x, sc_hist[j], w_in_a[j], conv_a[j], w_out_a[j], mem_k[i], mem_v[i])
            new_sc.append(hs)
        else:
            mix, hg, sg = gdn_mixer(x, gdn_hist[j], gdn_s[j], w_in_b[j], conv_b[j], a_log[j], dt_bias[j],
                                    gdn_norm_w[j], w_out_b[j], mem_k[i], mem_v[i])
            new_gc.append(hg)
            new_gs.append(sg)
        x = layer_norm(ALPHA * x + mix, ln1_g[i], ln1_b[i])
        f, hf = channel_mixer(x, ffn_hist[i], w_up[i], w_conv_ffn[i], w_down[i])
        new_ffn.append(hf)
        x = layer_norm(ALPHA * x + f, ln2_g[i], ln2_b[i])
    return x, jnp.stack(new_sc), jnp.stack(new_gc), jnp.stack(new_gs), jnp.stack(new_ffn)


def setup_inputs(seed: int = 0) -> dict:
    key = jax.random.key(seed)
    ks = jax.random.split(key, 32)
    nrm = jax.random.normal
    f32 = jnp.float32
    w_in_b_cols = 4 * GDN_DIM + 2 * GDN_H + XDIM
    dt = jnp.exp(jax.random.uniform(ks[15], (N_B, GDN_H), f32, math.log(1e-3), math.log(1e-1)))
    return {
        "x_prompt": nrm(ks[0], (BATCH, SEQ, D_MODEL), f32),
        "x_sample": nrm(ks[1], (DEC_BATCH, DEC_SEQ, D_MODEL), f32),
        "mem_prompt": nrm(ks[2], (BATCH, N_MEM, D_MODEL), f32),
        "cache_mem_k": nrm(ks[3], (DEPTH, DEC_BATCH, N_MEM, XH, XD), f32),
        "cache_mem_v": nrm(ks[4], (DEPTH, DEC_BATCH, N_MEM, XH, XD), f32),
        "state_shortconv": nrm(ks[5], (N_A, DEC_BATCH, SC_W - 1, SC_DIM), f32),
        "state_gdn_conv": nrm(ks[6], (N_B, DEC_BATCH, GDN_CONV_W - 1, 3 * GDN_DIM), f32),
        "state_gdn": 0.1 * nrm(ks[7], (N_B, DEC_BATCH, GDN_H, GDN_DK, GDN_DV), f32),
        "state_ffn_conv": nrm(ks[8], (DEPTH, DEC_BATCH, FFN_W - 1, 2 * D_FF), f32),
        "w_in_a": nrm(ks[9], (N_A, D_MODEL, 3 * SC_DIM + XDIM), f32) * D_MODEL ** -0.5,
        "conv_a": nrm(ks[10], (N_A, SC_W, SC_DIM), f32) * SC_W ** -0.5,
        "w_out_a": nrm(ks[11], (N_A, SC_DIM + XDIM, D_MODEL), f32) * (SC_DIM + XDIM) ** -0.5 * BETA_DN,
        "w_in_b": nrm(ks[12], (N_B, D_MODEL, w_in_b_cols), f32) * D_MODEL ** -0.5,
        "conv_b": nrm(ks[13], (N_B, GDN_CONV_W, 3 * GDN_DIM), f32) * GDN_CONV_W ** -0.5,
        "a_log": jnp.log(jax.random.uniform(ks[14], (N_B, GDN_H), f32, 1.0, 16.0)),
        "dt_bias": dt + jnp.log(-jnp.expm1(-dt)),
        "gdn_norm_w": 1.0 + 0.02 * nrm(ks[16], (N_B, GDN_DV), f32),
        "w_out_b": nrm(ks[17], (N_B, GDN_DIM + XDIM, D_MODEL), f32) * (GDN_DIM + XDIM) ** -0.5 * BETA_DN,
        "w_mem_kv": nrm(ks[18], (DEPTH, D_MODEL, 2 * XDIM), f32) * D_MODEL ** -0.5,
        "ln1_g": 1.0 + 0.02 * nrm(ks[19], (DEPTH, D_MODEL), f32),
        "ln1_b": 0.02 * nrm(ks[20], (DEPTH, D_MODEL), f32),
        "ln2_g": 1.0 + 0.02 * nrm(ks[21], (DEPTH, D_MODEL), f32),
        "ln2_b": 0.02 * nrm(ks[22], (DEPTH, D_MODEL), f32),
        "w_up": nrm(ks[23], (DEPTH, D_MODEL, 2 * D_FF), f32) * D_MODEL ** -0.5,
        "w_conv_ffn": nrm(ks[24], (DEPTH, FFN_W, 2 * D_FF), f32) * FFN_W ** -0.5,
        "w_down": nrm(ks[25], (DEPTH, D_FF, D_MODEL), f32) * D_FF ** -0.5 * BETA_DN,
    }


def reference(x_prompt, x_sample, mem_prompt, cache_mem_k, cache_mem_v, state_shortconv, state_gdn_conv,
              state_gdn, state_ffn_conv, w_in_a, conv_a, w_out_a, w_in_b, conv_b, a_log, dt_bias,
              gdn_norm_w, w_out_b, w_mem_kv, ln1_g, ln1_b, ln2_g, ln2_b, w_up, w_conv_ffn, w_down):
    weights = (w_in_a, conv_a, w_out_a, w_in_b, conv_b, a_log, dt_bias, gdn_norm_w, w_out_b,
               ln1_g, ln1_b, ln2_g, ln2_b, w_up, w_conv_ffn, w_down)
    bsz = x_prompt.shape[0]
    dtp = x_prompt.dtype
    kv = jnp.einsum('bmd,lde->lbme', mem_prompt, w_mem_kv)
    mem_k_prompt = kv[..., :XDIM].reshape(DEPTH, bsz, N_MEM, XH, XD)
    mem_v_prompt = kv[..., XDIM:].reshape(DEPTH, bsz, N_MEM, XH, XD)
    z_sc = jnp.zeros((N_A, bsz, SC_W - 1, SC_DIM), dtp)
    z_gc = jnp.zeros((N_B, bsz, GDN_CONV_W - 1, 3 * GDN_DIM), dtp)
    z_gs = jnp.zeros((N_B, bsz, GDN_H, GDN_DK, GDN_DV), dtp)
    z_ffn = jnp.zeros((DEPTH, bsz, FFN_W - 1, 2 * D_FF), dtp)
    y_prompt, sc_p, gc_p, gs_p, ffn_p = trunk(x_prompt, mem_k_prompt, mem_v_prompt, z_sc, z_gc, z_gs, z_ffn,
                                              *weights)
    y_sample, sc_s, gc_s, gs_s, ffn_s = trunk(x_sample, cache_mem_k, cache_mem_v, state_shortconv,
                                              state_gdn_conv, state_gdn, state_ffn_conv, *weights)
    return (y_prompt, y_sample, mem_k_prompt, mem_v_prompt, sc_p, gc_p, gs_p, ffn_p, sc_s, gc_s, gs_s, ffn_s)
```

```python
import functools

import jax
import jax.numpy as jnp
from jax import lax
from jax.experimental import pallas as pl
from jax.experimental.pallas import tpu as pltpu

F32 = jnp.float32
BF16 = jnp.bfloat16

DEPTH = 4
D_MODEL = 1024
SC_DIM = 768
SC_W = 3
GDN_H = 6
GDN_DK = 128
GDN_DV = 128
GDN_DIM = GDN_H * GDN_DK
GDN_CONV_W = 4
GDN_CHUNK = 64
N_MEM = 256
XH = 4
XD = 64
XDIM = XH * XD
D_FF = 2816
FFN_W = 3
ALPHA = (2.0 * DEPTH) ** 0.25
LN_EPS = 1e-5
RMS_EPS = 1e-6

V7X_VMEM_BYTES = 64 * 1024 * 1024
VMEM_LIMIT = V7X_VMEM_BYTES - 8 * 1024 * 1024
SUBLANES = 8
LANES = 128

GB_LANES = LANES

SEQ_TILE = 512
SCAN_TILE = 256
ROW_TILE = 512
FF_TILE = 1408
ATTN_BATCH_BLOCK = 16
W_B_COLS = 3 * GDN_DIM + GDN_DIM + XDIM + GB_LANES


def _cparams(sem):
    return pltpu.CompilerParams(dimension_semantics=sem, vmem_limit_bytes=VMEM_LIMIT)


def _resident(shape):
    nd = len(shape)
    return pl.BlockSpec(shape, lambda *_: (0,) * nd, pipeline_mode=pl.Buffered(1))


def _silu(x):
    return x * (1.0 / (1.0 + jnp.exp(-x)))


def _sigmoid(x):
    return 1.0 / (1.0 + jnp.exp(-x))


def _softplus(x):
    return jnp.maximum(x, 0.0) + jnp.log(1.0 + jnp.exp(-jnp.abs(x)))


def _layer_norm(v, g, b):
    mu = jnp.mean(v, -1, keepdims=True)
    d = v - mu
    var = jnp.mean(d * d, -1, keepdims=True)
    return d * lax.rsqrt(var + LN_EPS) * g + b


def _dot(a, b):
    return jnp.dot(a, b, preferred_element_type=F32)


def _dot_nt(a, b):
    return lax.dot_general(a, b, (((1,), (1,)), ((), ())), preferred_element_type=F32)


def _dot_tn(a, b):
    return lax.dot_general(a, b, (((0,), (0,)), ((), ())), preferred_element_type=F32)


def _dot_hi(a, b):
    return jnp.dot(a, b, preferred_element_type=F32, precision=lax.Precision.HIGHEST)


def _dot_nt_hi(a, b):
    return lax.dot_general(a, b, (((1,), (1,)), ((), ())), preferred_element_type=F32,
                           precision=lax.Precision.HIGHEST)


def _kv_kernel(m_ref, w_ref, k_ref, v_ref):
    kv = _dot(m_ref[...].astype(BF16), w_ref[0])
    k_ref[0] = kv[:, :XDIM]
    v_ref[0] = kv[:, XDIM:]


def _mem_kv(mem2d, w_kv):
    rows = mem2d.shape[0]
    out = jax.ShapeDtypeStruct((DEPTH, rows, XDIM), F32)
    return pl.pallas_call(
        _kv_kernel,
        grid=(DEPTH,),
        in_specs=[_resident((rows, D_MODEL)),
                  pl.BlockSpec((1, D_MODEL, 2 * XDIM), lambda l: (l, 0, 0))],
        out_specs=[pl.BlockSpec((1, rows, XDIM), lambda l: (l, 0, 0))] * 2,
        out_shape=[out, out],
        compiler_params=_cparams(("arbitrary",)),
        name="mem_kv",
    )(mem2d, w_kv)


def _head_blockdiag(kv):
    lane_head = lax.broadcasted_iota(jnp.int32, kv.shape, 1) // XD
    return jnp.concatenate(
        [jnp.where(lane_head == h, kv, 0.0).astype(BF16) for h in range(XH)], axis=0)


def _softmax_rows(s):
    m = jnp.max(s, -1, keepdims=True)
    e = jnp.exp(s - m)
    return e / jnp.sum(e, -1, keepdims=True)


def _mem_attn_seq(qm, kbd, vbd):
    s = _dot_nt(qm.astype(BF16), kbd) * (XD ** -0.5)
    p = jnp.concatenate(
        [_softmax_rows(s[:, h * N_MEM:(h + 1) * N_MEM]).astype(BF16) for h in range(XH)], axis=-1)
    return _dot(p, vbd)


def _attn_sample_kernel(q_ref, k_ref, v_ref, o_ref, *, bb, t_len):
    lane_head = lax.broadcasted_iota(jnp.int32, (SUBLANES, XDIM), 1) // XD
    row_ok = lax.broadcasted_iota(jnp.int32, (SUBLANES, XDIM), 0) < t_len

    def body(i, carry):
        rows = [q_ref[t, pl.ds(i, 1), :] for t in range(t_len)]
        rows.append(jnp.zeros((SUBLANES - t_len, XDIM), F32))
        q8 = jnp.concatenate(rows, axis=0)
        qblk = jnp.concatenate(
            [jnp.where(lane_head == h, q8, 0.0).astype(BF16) for h in range(XH)], axis=0)
        s = _dot_nt(qblk, k_ref[i].astype(BF16)) * (XD ** -0.5)
        p = _softmax_rows(s).astype(BF16)
        o_full = _dot(p, v_ref[i].astype(BF16))
        o8 = jnp.zeros((SUBLANES, XDIM), F32)
        for h in range(XH):
            o8 = o8 + jnp.where(lane_head == h, o_full[h * SUBLANES:(h + 1) * SUBLANES], 0.0)
        o8 = jnp.where(row_ok, o8, 0.0).astype(o_ref.dtype)
        for t in range(t_len):
            o_ref[t, pl.ds(i, 1), :] = o8[t:t + 1]
        return carry

    lax.fori_loop(0, bb, body, 0)


def _attn_sample(qm_tm, mem_k, mem_v):
    t_len, bsz, _ = qm_tm.shape
    bb = min(ATTN_BATCH_BLOCK, bsz)
    return pl.pallas_call(
        functools.partial(_attn_sample_kernel, bb=bb, t_len=t_len),
        grid=(bsz // bb,),
        in_specs=[pl.BlockSpec((t_len, bb, XDIM), lambda i: (0, i, 0)),
                  pl.BlockSpec((bb, N_MEM, XDIM), lambda i: (i, 0, 0)),
                  pl.BlockSpec((bb, N_MEM, XDIM), lambda i: (i, 0, 0))],
        out_specs=pl.BlockSpec((t_len, bb, XDIM), lambda i: (0, i, 0)),
        out_shape=jax.ShapeDtypeStruct((t_len, bsz, XDIM), F32),
        compiler_params=_cparams(("arbitrary",)),
        name="attn_sample",
    )(qm_tm, mem_k, mem_v)


def _mix_a_seq_kernel(x_ref, hist_ref, win_ref, wc_ref, wout_ref, k_ref, v_ref, g_ref, b_ref,
                      o_ref, hist_o_ref, ubuf, kbd, vbd, *, tm):
    t = pl.program_id(1)
    lo = SUBLANES - (SC_W - 1)

    @pl.when(t == 0)
    def _():
        ubuf[lo:SUBLANES, :] = hist_ref[0]
        kbd[...] = _head_blockdiag(k_ref[0])
        vbd[...] = _head_blockdiag(v_ref[0])

    x = x_ref[0]
    h = _dot(x.astype(BF16), win_ref[...])
    xin = h[:, :SC_DIM]
    bg = h[:, SC_DIM:2 * SC_DIM]
    cg = h[:, 2 * SC_DIM:3 * SC_DIM]
    qm = h[:, 3 * SC_DIM:]
    u = cg * xin
    ubuf[SUBLANES:SUBLANES + tm, :] = u
    wc = wc_ref[...]
    conv = wc[SC_W - 1:SC_W] * u
    for j in range(SC_W - 1):
        conv = conv + wc[j:j + 1] * ubuf[lo + j:lo + j + tm, :]
    y = bg * conv
    last = ubuf[tm + lo:tm + SUBLANES, :]
    ubuf[lo:SUBLANES, :] = last
    hist_o_ref[0] = last

    o_mem = _mem_attn_seq(qm, kbd[...], vbd[...])
    mix = _dot(y.astype(BF16), wout_ref[:SC_DIM, :]) + _dot(o_mem.astype(BF16), wout_ref[SC_DIM:, :])
    o_ref[0] = _layer_norm(ALPHA * x + mix, g_ref[...], b_ref[...])


def _mix_a_seq(x, hist, w_in, w_conv, w_out, mem_k, mem_v, ln_g, ln_b):
    bsz, t_len, _ = x.shape
    tm = min(SEQ_TILE, t_len)
    n_in = w_in.shape[1]
    return pl.pallas_call(
        functools.partial(_mix_a_seq_kernel, tm=tm),
        grid=(bsz, t_len // tm),
        in_specs=[pl.BlockSpec((1, tm, D_MODEL), lambda b, t: (b, t, 0)),
                  pl.BlockSpec((1, SC_W - 1, SC_DIM), lambda b, t: (b, 0, 0)),
                  _resident((D_MODEL, n_in)),
                  _resident((SC_W, SC_DIM)),
                  _resident((SC_DIM + XDIM, D_MODEL)),
                  pl.BlockSpec((1, N_MEM, XDIM), lambda b, t: (b, 0, 0)),
                  pl.BlockSpec((1, N_MEM, XDIM), lambda b, t: (b, 0, 0)),
                  _resident((1, D_MODEL)),
                  _resident((1, D_MODEL))],
        out_specs=[pl.BlockSpec((1, tm, D_MODEL), lambda b, t: (b, t, 0)),
                   pl.BlockSpec((1, SC_W - 1, SC_DIM), lambda b, t: (b, 0, 0))],
        out_shape=[jax.ShapeDtypeStruct((bsz, t_len, D_MODEL), F32),
                   jax.ShapeDtypeStruct((bsz, SC_W - 1, SC_DIM), F32)],
        scratch_shapes=[pltpu.VMEM((tm + SUBLANES, SC_DIM), F32),
                        pltpu.VMEM((XH * N_MEM, XDIM), BF16),
                        pltpu.VMEM((XH * N_MEM, XDIM), BF16)],
        compiler_params=_cparams(("arbitrary", "arbitrary")),
        name="mix_a_seq",
    )(x, hist, w_in, w_conv, w_out, mem_k, mem_v, ln_g, ln_b)


def _ffn_seq_kernel(x_ref, hist_ref, wup_ref, wc_ref, wdn_ref, g_ref, b_ref,
                    o_ref, hist_o_ref, gbuf, ubuf, *, tm, tf):
    t = pl.program_id(1)
    lo = SUBLANES - (FFN_W - 1)
    nj = D_FF // tf
    halves = ((gbuf, 0), (ubuf, D_FF))

    @pl.when(t == 0)
    def _():
        for buf, off in halves:
            for j in range(nj):
                buf[j, lo:SUBLANES, :] = hist_ref[0, :, off + j * tf:off + (j + 1) * tf]

    x = x_ref[0]
    xb = x.astype(BF16)
    acc = None
    for j in range(nj):
        conv = []
        for buf, off in halves:
            c0 = off + j * tf
            h = _dot(xb, wup_ref[:, c0:c0 + tf])
            buf[j, SUBLANES:SUBLANES + tm, :] = h
            c = wc_ref[FFN_W - 1:FFN_W, c0:c0 + tf] * h
            for w in range(FFN_W - 1):
                c = c + wc_ref[w:w + 1, c0:c0 + tf] * buf[j, lo + w:lo + w + tm, :]
            last = buf[j, tm + lo:tm + SUBLANES, :]
            buf[j, lo:SUBLANES, :] = last
            hist_o_ref[0, :, c0:c0 + tf] = last
            conv.append(c)
        act = (_silu(conv[0]) * conv[1]).astype(BF16)
        d = _dot(act, wdn_ref[j * tf:(j + 1) * tf, :])
        acc = d if acc is None else acc + d
    o_ref[0] = _layer_norm(ALPHA * x + acc, g_ref[...], b_ref[...])


def _ffn_seq(x, hist, w_up, w_conv, w_down, ln_g, ln_b):
    bsz, t_len, _ = x.shape
    tm = min(SEQ_TILE, t_len)
    tf = FF_TILE
    return pl.pallas_call(
        functools.partial(_ffn_seq_kernel, tm=tm, tf=tf),
        grid=(bsz, t_len // tm),
        in_specs=[pl.BlockSpec((1, tm, D_MODEL), lambda b, t: (b, t, 0)),
                  pl.BlockSpec((1, FFN_W - 1, 2 * D_FF), lambda b, t: (b, 0, 0)),
                  _resident((D_MODEL, 2 * D_FF)),
                  _resident((FFN_W, 2 * D_FF)),
                  _resident((D_FF, D_MODEL)),
                  _resident((1, D_MODEL)),
                  _resident((1, D_MODEL))],
        out_specs=[pl.BlockSpec((1, tm, D_MODEL), lambda b, t: (b, t, 0)),
                   pl.BlockSpec((1, FFN_W - 1, 2 * D_FF), lambda b, t: (b, 0, 0))],
        out_shape=[jax.ShapeDtypeStruct((bsz, t_len, D_MODEL), F32),
                   jax.ShapeDtypeStruct((bsz, FFN_W - 1, 2 * D_FF), F32)],
        scratch_shapes=[pltpu.VMEM((D_FF // tf, tm + SUBLANES, tf), F32),
                        pltpu.VMEM((D_FF // tf, tm + SUBLANES, tf), F32)],
        compiler_params=_cparams(("arbitrary", "arbitrary")),
        name="ffn_seq",
    )(x, hist, w_up, w_conv, w_down, ln_g, ln_b)


def _gdn_gates(ba, a_row, dt_row):
    lane = lax.broadcasted_iota(jnp.int32, ba.shape, 1)
    return jnp.where(lane < GDN_H, _sigmoid(ba), a_row * _softplus(ba + dt_row))


def _qkv_post(c, q_ref, k_ref, v_ref, idx):
    c = _silu(c)
    for h in range(GDN_H):
        for ref, base in ((q_ref, 0), (k_ref, GDN_DIM)):
            a = c[:, base + h * GDN_DK:base + (h + 1) * GDN_DK]
            a = a * lax.rsqrt(jnp.sum(a * a, -1, keepdims=True) + RMS_EPS)
            ref[idx + (slice(None), slice(h * GDN_DK, (h + 1) * GDN_DK))] = a
    v_ref[idx + (slice(None), slice(None))] = c[:, 2 * GDN_DIM:]


def _proj_b_seq_kernel(x_ref, hist_ref, win_ref, wc_ref, arow_ref, dtrow_ref, k_ref, v_ref,
                       q_o, k_o, v_o, gb_o, zg_o, om_o, hist_o_ref, sbuf, kbd, vbd, *, tm):
    t = pl.program_id(1)
    lo = SUBLANES - (GDN_CONV_W - 1)
    nqkv = 3 * GDN_DIM

    @pl.when(t == 0)
    def _():
        sbuf[lo:SUBLANES, :] = hist_ref[0]
        kbd[...] = _head_blockdiag(k_ref[0])
        vbd[...] = _head_blockdiag(v_ref[0])

    h = _dot(x_ref[0].astype(BF16), win_ref[...])
    qkv = h[:, :nqkv]
    sbuf[SUBLANES:SUBLANES + tm, :] = qkv
    wc = wc_ref[...]
    conv = wc[GDN_CONV_W - 1:GDN_CONV_W] * qkv
    for j in range(GDN_CONV_W - 1):
        conv = conv + wc[j:j + 1] * sbuf[lo + j:lo + j + tm, :]
    last = sbuf[tm + lo:tm + SUBLANES, :]
    sbuf[lo:SUBLANES, :] = last
    hist_o_ref[0] = last

    _qkv_post(conv, q_o, k_o, v_o, (0,))
    zg_o[0] = _silu(h[:, nqkv:nqkv + GDN_DIM])
    qm = h[:, nqkv + GDN_DIM:nqkv + GDN_DIM + XDIM]
    gb_o[0] = _gdn_gates(h[:, nqkv + GDN_DIM + XDIM:], arow_ref[...], dtrow_ref[...])
    om_o[0] = _mem_attn_seq(qm, kbd[...], vbd[...]).astype(BF16)


def _proj_b_seq(x, hist, w_in, w_conv, a_row, dt_row, mem_k, mem_v):
    bsz, t_len, _ = x.shape
    tm = min(SEQ_TILE, t_len)
    tile = lambda n: pl.BlockSpec((1, tm, n), lambda b, t: (b, t, 0))
    f32o = lambda n: jax.ShapeDtypeStruct((bsz, t_len, n), F32)
    return pl.pallas_call(
        functools.partial(_proj_b_seq_kernel, tm=tm),
        grid=(bsz, t_len // tm),
        in_specs=[tile(D_MODEL),
                  pl.BlockSpec((1, GDN_CONV_W - 1, 3 * GDN_DIM), lambda b, t: (b, 0, 0)),
                  _resident((D_MODEL, W_B_COLS)),
                  _resident((GDN_CONV_W, 3 * GDN_DIM)),
                  _resident((1, GB_LANES)),
                  _resident((1, GB_LANES)),
                  pl.BlockSpec((1, N_MEM, XDIM), lambda b, t: (b, 0, 0)),
                  pl.BlockSpec((1, N_MEM, XDIM), lambda b, t: (b, 0, 0))],
        out_specs=[tile(GDN_DIM), tile(GDN_DIM), tile(GDN_DIM), tile(GB_LANES), tile(GDN_DIM),
                   tile(XDIM),
                   pl.BlockSpec((1, GDN_CONV_W - 1, 3 * GDN_DIM), lambda b, t: (b, 0, 0))],
        out_shape=[f32o(GDN_DIM), f32o(GDN_DIM), f32o(GDN_DIM), f32o(GB_LANES), f32o(GDN_DIM),
                   jax.ShapeDtypeStruct((bsz, t_len, XDIM), BF16),
                   jax.ShapeDtypeStruct((bsz, GDN_CONV_W - 1, 3 * GDN_DIM), F32)],
        scratch_shapes=[pltpu.VMEM((tm + SUBLANES, 3 * GDN_DIM), F32),
                        pltpu.VMEM((XH * N_MEM, XDIM), BF16),
                        pltpu.VMEM((XH * N_MEM, XDIM), BF16)],
        compiler_params=_cparams(("arbitrary", "arbitrary")),
        name="proj_b_seq",
    )(x, hist, w_in, w_conv, a_row, dt_row, mem_k, mem_v)


def _gdn_scan_kernel(q_ref, k_ref, v_ref, gb_ref, zg_ref, nw_ref, s0_ref, o_ref, s_o_ref, s_scr,
                     *, tc, c):
    t = pl.program_id(1)

    @pl.when(t == 0)
    def _():
        s_scr[...] = s0_ref[0]

    ri = lax.broadcasted_iota(jnp.int32, (c, c), 0)
    ci = lax.broadcasted_iota(jnp.int32, (c, c), 1)
    tril = ri >= ci
    strict = ri > ci
    tril_f = tril.astype(F32)
    eye_l = (lax.broadcasted_iota(jnp.int32, (GB_LANES, GB_LANES), 0)
             == lax.broadcasted_iota(jnp.int32, (GB_LANES, GB_LANES), 1)).astype(F32)
    nw = nw_ref[...]
    n_sq = max(c.bit_length() - 2, 0)

    def chunk(ic, carry):
        r0 = pl.multiple_of(ic * c, c)
        rows = pl.ds(r0, c)
        gb = gb_ref[0, rows, :]
        gc_all = _dot_hi(tril_f, gb)
        gc_t = _dot_nt_hi(eye_l, gc_all)
        for h in range(GDN_H):
            hs = slice(h * GDN_DK, (h + 1) * GDN_DK)
            q = q_ref[0, rows, hs] * (GDN_DK ** -0.5)
            k = k_ref[0, rows, hs]
            v = v_ref[0, rows, hs]
            beta = gb[:, h:h + 1]
            gcol = gc_all[:, GDN_H + h:GDN_H + h + 1]
            grow = gc_t[GDN_H + h:GDN_H + h + 1, :]
            glast = gc_all[c - 1:c, GDN_H + h:GDN_H + h + 1]
            decay = jnp.exp(jnp.where(tril, gcol - grow, -jnp.inf))
            egc = jnp.exp(gcol)
            kb = k * beta
            kt = k.astype(BF16)
            aq = _dot_nt(jnp.concatenate([kb, q], axis=0).astype(BF16), kt)
            lmat = jnp.where(strict, aq[:c] * decay, 0.0)
            qk = aq[c:] * decay
            sol = jnp.concatenate([v * beta, kb * egc], axis=-1)
            sol = sol - _dot_hi(lmat, sol)
            p = lmat
            for _ in range(n_sq):
                p = _dot_hi(p, p)
                sol = sol + _dot_hi(p, sol)
            u = sol[:, :GDN_DV]
            w = sol[:, GDN_DV:]
            s = s_scr[h]
            sb = s.astype(BF16)
            ws = _dot(jnp.concatenate([w, q * egc], axis=0).astype(BF16), sb)
            v_new = u - ws[:c]
            vb = v_new.astype(BF16)
            o = ws[c:] + _dot(qk.astype(BF16), vb)
            k_dec = (k * jnp.exp(glast - gcol)).astype(BF16)
            s_scr[h] = s * jnp.exp(glast) + _dot_tn(k_dec, vb)
            o = o * lax.rsqrt(jnp.mean(o * o, -1, keepdims=True) + RMS_EPS)
            o_ref[0, rows, hs] = (o * nw * zg_ref[0, rows, hs]).astype(o_ref.dtype)
        return carry

    lax.fori_loop(0, tc // c, chunk, 0)
    s_o_ref[0] = s_scr[...]


def _gdn_scan(q, k, v, gb, zg, norm_w, s0, *, c):
    bsz, t_len, _ = q.shape
    tc = min(SCAN_TILE, t_len)
    tile = lambda n: pl.BlockSpec((1, tc, n), lambda b, t: (b, t, 0))
    st = pl.BlockSpec((1, GDN_H, GDN_DK, GDN_DV), lambda b, t: (b, 0, 0, 0))
    return pl.pallas_call(
        functools.partial(_gdn_scan_kernel, tc=tc, c=c),
        grid=(bsz, t_len // tc),
        in_specs=[tile(GDN_DIM), tile(GDN_DIM), tile(GDN_DIM), tile(GB_LANES), tile(GDN_DIM),
                  _resident((1, GDN_DV)), st],
        out_specs=[tile(GDN_DIM), st],
        out_shape=[jax.ShapeDtypeStruct((bsz, t_len, GDN_DIM), BF16),
                   jax.ShapeDtypeStruct((bsz, GDN_H, GDN_DK, GDN_DV), F32)],
        scratch_shapes=[pltpu.VMEM((GDN_H, GDN_DK, GDN_DV), F32)],
        compiler_params=_cparams(("arbitrary", "arbitrary")),
        name="gdn_scan",
    )(q, k, v, gb, zg, norm_w, s0)


def _out_ln_kernel(a1_ref, a2_ref, w_ref, x_ref, g_ref, b_ref, o_ref):
    n1 = a1_ref.shape[-1]
    mix = (_dot(a1_ref[...].astype(BF16), w_ref[:n1, :])
           + _dot(a2_ref[...].astype(BF16), w_ref[n1:, :]))
    o_ref[...] = _layer_norm(ALPHA * x_ref[...] + mix, g_ref[...], b_ref[...])


def _out_ln(a1, a2, w, x, ln_g, ln_b):
    rows = x.shape[0]
    tr = min(ROW_TILE, rows)
    n1, n2 = a1.shape[1], a2.shape[1]
    tile = lambda n: pl.BlockSpec((tr, n), lambda i: (i, 0))
    return pl.pallas_call(
        _out_ln_kernel,
        grid=(rows // tr,),
        in_specs=[tile(n1), tile(n2), _resident((n1 + n2, D_MODEL)), tile(D_MODEL),
                  _resident((1, D_MODEL)), _resident((1, D_MODEL))],
        out_specs=tile(D_MODEL),
        out_shape=jax.ShapeDtypeStruct((rows, D_MODEL), F32),
        compiler_params=_cparams(("arbitrary",)),
        name="out_ln",
    )(a1, a2, w, x, ln_g, ln_b)


def _mix_a_front_sample_kernel(x_ref, hist_ref, win_ref, wc_ref, y_o, qm_o, hist_o, *, t_len, bsz):
    h = _dot(x_ref[...].astype(BF16), win_ref[...])
    u = h[:, 2 * SC_DIM:3 * SC_DIM] * h[:, :SC_DIM]
    slabs = [hist_ref[:, j * SC_DIM:(j + 1) * SC_DIM] for j in range(SC_W - 1)]
    slabs += [u[t * bsz:(t + 1) * bsz] for t in range(t_len)]
    wc = wc_ref[...]
    for t in range(t_len):
        conv = wc[0:1] * slabs[t]
        for j in range(1, SC_W):
            conv = conv + wc[j:j + 1] * slabs[t + j]
        y_o[t * bsz:(t + 1) * bsz, :] = (h[t * bsz:(t + 1) * bsz, SC_DIM:2 * SC_DIM] * conv).astype(BF16)
    for j in range(SC_W - 1):
        hist_o[j] = slabs[t_len + j]
    qm_o[...] = h[:, 3 * SC_DIM:]


def _mix_a_front_sample(x2d, hist2d, w_in, w_conv, *, t_len, bsz):
    rows = x2d.shape[0]
    return pl.pallas_call(
        functools.partial(_mix_a_front_sample_kernel, t_len=t_len, bsz=bsz),
        out_shape=[jax.ShapeDtypeStruct((rows, SC_DIM), BF16),
                   jax.ShapeDtypeStruct((rows, XDIM), F32),
                   jax.ShapeDtypeStruct((SC_W - 1, bsz, SC_DIM), F32)],
        compiler_params=pltpu.CompilerParams(vmem_limit_bytes=VMEM_LIMIT),
        name="mix_a_front_sample",
    )(x2d, hist2d, w_in, w_conv)


def _proj_b_front_sample_kernel(x_ref, hist_ref, win_ref, wc_ref, arow_ref, dtrow_ref,
                                q_o, k_o, v_o, gb_o, zg_o, qm_o, hist_o, *, t_len, bsz):
    nqkv = 3 * GDN_DIM
    h = _dot(x_ref[...].astype(BF16), win_ref[...])
    slabs = [hist_ref[:, j * nqkv:(j + 1) * nqkv] for j in range(GDN_CONV_W - 1)]
    slabs += [h[t * bsz:(t + 1) * bsz, :nqkv] for t in range(t_len)]
    wc = wc_ref[...]
    for t in range(t_len):
        conv = wc[0:1] * slabs[t]
        for j in range(1, GDN_CONV_W):
            conv = conv + wc[j:j + 1] * slabs[t + j]
        _qkv_post(conv, q_o, k_o, v_o, (t,))
    for j in range(GDN_CONV_W - 1):
        hist_o[j] = slabs[t_len + j]
    zg_o[...] = _silu(h[:, nqkv:nqkv + GDN_DIM])
    qm_o[...] = h[:, nqkv + GDN_DIM:nqkv + GDN_DIM + XDIM]
    gb_o[...] = _gdn_gates(h[:, nqkv + GDN_DIM + XDIM:], arow_ref[...], dtrow_ref[...])


def _proj_b_front_sample(x2d, hist2d, w_in, w_conv, a_row, dt_row, *, t_len, bsz):
    rows = x2d.shape[0]
    tmaj = jax.ShapeDtypeStruct((t_len, bsz, GDN_DIM), F32)
    return pl.pallas_call(
        functools.partial(_proj_b_front_sample_kernel, t_len=t_len, bsz=bsz),
        out_shape=[tmaj, tmaj, tmaj,
                   jax.ShapeDtypeStruct((rows, GB_LANES), F32),
                   jax.ShapeDtypeStruct((rows, GDN_DIM), F32),
                   jax.ShapeDtypeStruct((rows, XDIM), F32),
                   jax.ShapeDtypeStruct((GDN_CONV_W - 1, bsz, 3 * GDN_DIM), F32)],
        compiler_params=pltpu.CompilerParams(vmem_limit_bytes=VMEM_LIMIT),
        name="proj_b_front_sample",
    )(x2d, hist2d, w_in, w_conv, a_row, dt_row)


def _ffn_sample_kernel(x_ref, hg0, hg1, hu0, hu1, wg_ref, wu_ref, cg_ref, cu_ref, wdn_ref,
                       g_ref, b_ref, o_ref, hgo, huo, acc, *, t_len, bsz):
    j = pl.program_id(0)

    @pl.when(j == 0)
    def _():
        acc[...] = jnp.zeros_like(acc)

    xb = x_ref[...].astype(BF16)
    conv = []
    for w_ref, c_ref, hists, h_o in ((wg_ref, cg_ref, (hg0, hg1), hgo), (wu_ref, cu_ref, (hu0, hu1), huo)):
        h = _dot(xb, w_ref[...])
        slabs = [hists[0][...], hists[1][...]] + [h[t * bsz:(t + 1) * bsz] for t in range(t_len)]
        wc = c_ref[...]
        outs = []
        for t in range(t_len):
            c = wc[0:1] * slabs[t]
            for w in range(1, FFN_W):
                c = c + wc[w:w + 1] * slabs[t + w]
            outs.append(c)
        for w in range(FFN_W - 1):
            h_o[w] = slabs[t_len + w]
        conv.append(jnp.concatenate(outs, axis=0))
    act = (_silu(conv[0]) * conv[1]).astype(BF16)
    acc[...] += _dot(act, wdn_ref[...])

    @pl.when(j == pl.num_programs(0) - 1)
    def _():
        o_ref[...] = _layer_norm(ALPHA * x_ref[...] + acc[...], g_ref[...], b_ref[...])


def _ffn_sample(x2d, hist2d, w_up, w_conv, w_down, ln_g, ln_b, *, t_len, bsz):
    rows = x2d.shape[0]
    tf = FF_TILE
    nj = D_FF // tf
    hspec = lambda r, off: pl.BlockSpec((bsz, tf), lambda j: (0, r * (2 * D_FF // tf) + off + j))
    hout = jax.ShapeDtypeStruct((FFN_W - 1, bsz, D_FF), F32)
    return pl.pallas_call(
        functools.partial(_ffn_sample_kernel, t_len=t_len, bsz=bsz),
        grid=(nj,),
        in_specs=[_resident((rows, D_MODEL)),
                  hspec(0, 0), hspec(1, 0), hspec(0, nj), hspec(1, nj),
                  pl.BlockSpec((D_MODEL, tf), lambda j: (0, j)),
                  pl.BlockSpec((D_MODEL, tf), lambda j: (0, nj + j)),
                  pl.BlockSpec((FFN_W, tf), lambda j: (0, j)),
                  pl.BlockSpec((FFN_W, tf), lambda j: (0, nj + j)),
                  pl.BlockSpec((tf, D_MODEL), lambda j: (j, 0)),
                  _resident((1, D_MODEL)), _resident((1, D_MODEL))],
        out_specs=[pl.BlockSpec((rows, D_MODEL), lambda j: (0, 0)),
                   pl.BlockSpec((FFN_W - 1, bsz, tf), lambda j: (0, 0, j)),
                   pl.BlockSpec((FFN_W - 1, bsz, tf), lambda j: (0, 0, j))],
        out_shape=[jax.ShapeDtypeStruct((rows, D_MODEL), F32), hout, hout],
        scratch_shapes=[pltpu.VMEM((rows, D_MODEL), F32)],
        compiler_params=_cparams(("arbitrary",)),
        name="ffn_sample",
    )(x2d, hist2d, hist2d, hist2d, hist2d, w_up, w_up, w_conv, w_conv, w_down, ln_g, ln_b)


def _prep_weights(w_in_a, w_out_a, w_in_b, a_log, dt_bias, w_out_b, w_mem_kv, w_up, w_down):
    nqkvz = 4 * GDN_DIM
    w_b = jnp.concatenate(
        [w_in_b[..., :nqkvz], w_in_b[..., nqkvz + 2 * GDN_H:],
         jnp.pad(w_in_b[..., nqkvz:nqkvz + 2 * GDN_H], ((0, 0), (0, 0), (0, GB_LANES - 2 * GDN_H)))],
        axis=-1).astype(BF16)
    pad = GB_LANES - 2 * GDN_H
    a_row = jnp.pad(-jnp.exp(a_log.astype(F32)), ((0, 0), (GDN_H, pad)))[:, None, :]
    dt_row = jnp.pad(dt_bias.astype(F32), ((0, 0), (GDN_H, pad)))[:, None, :]
    return dict(w_in_a=w_in_a.astype(BF16), w_out_a=w_out_a.astype(BF16), w_in_b=w_b,
                w_out_b=w_out_b.astype(BF16), w_mem_kv=w_mem_kv.astype(BF16),
                w_up=w_up.astype(BF16), w_down=w_down.astype(BF16), a_row=a_row, dt_row=dt_row)


def _trunk_prompt(x, mem_k, mem_v, wts, conv_a, conv_b, gdn_norm_w, ln1_g, ln1_b, ln2_g, ln2_b,
                  w_conv_ffn):
    bsz, t_len, _ = x.shape
    z = lambda *s: jnp.zeros(s, F32)
    new_sc, new_gc, new_gs, new_ffn = [], [], [], []
    for i in range(DEPTH):
        j = i // 2
        g1, b1 = ln1_g[i][None], ln1_b[i][None]
        if i % 2 == 0:
            x, hs = _mix_a_seq(x, z(bsz, SC_W - 1, SC_DIM), wts["w_in_a"][j], conv_a[j],
                               wts["w_out_a"][j], mem_k[i], mem_v[i], g1, b1)
            new_sc.append(hs)
        else:
            q, k, v, gb, zg, om, hg = _proj_b_seq(
                x, z(bsz, GDN_CONV_W - 1, 3 * GDN_DIM), wts["w_in_b"][j], conv_b[j],
                wts["a_row"][j], wts["dt_row"][j], mem_k[i], mem_v[i])
            o, sg = _gdn_scan(q, k, v, gb, zg, gdn_norm_w[j][None], z(bsz, GDN_H, GDN_DK, GDN_DV),
                              c=GDN_CHUNK)
            rows = bsz * t_len
            x = _out_ln(o.reshape(rows, GDN_DIM), om.reshape(rows, XDIM), wts["w_out_b"][j],
                        x.reshape(rows, D_MODEL), g1, b1).reshape(bsz, t_len, D_MODEL)
            new_gc.append(hg)
            new_gs.append(sg)
        x, hf = _ffn_seq(x, z(bsz, FFN_W - 1, 2 * D_FF), wts["w_up"][i], w_conv_ffn[i],
                         wts["w_down"][i], ln2_g[i][None], ln2_b[i][None])
        new_ffn.append(hf)
    return x, jnp.stack(new_sc), jnp.stack(new_gc), jnp.stack(new_gs), jnp.stack(new_ffn)


def _trunk_sample(x, mem_k, mem_v, sc_hist, gdn_hist, gdn_s, ffn_hist, wts, conv_a, conv_b,
                  gdn_norm_w, ln1_g, ln1_b, ln2_g, ln2_b, w_conv_ffn):
    bsz, t_len, _ = x.shape
    rows = bsz * t_len
    t_pad = SUBLANES
    x2 = jnp.transpose(x, (1, 0, 2)).reshape(rows, D_MODEL)
    to_bm = lambda a: jnp.transpose(a, (1, 0, 2))
    new_sc, new_gc, new_gs, new_ffn = [], [], [], []
    for i in range(DEPTH):
        j = i // 2
        g1, b1 = ln1_g[i][None], ln1_b[i][None]
        mk = mem_k[i].reshape(bsz, N_MEM, XDIM)
        mv = mem_v[i].reshape(bsz, N_MEM, XDIM)
        if i % 2 == 0:
            y, qm, hs = _mix_a_front_sample(x2, sc_hist[j].reshape(bsz, -1), wts["w_in_a"][j],
                                            conv_a[j], t_len=t_len, bsz=bsz)
            om = _attn_sample(qm.reshape(t_len, bsz, XDIM), mk, mv).reshape(rows, XDIM)
            x2 = _out_ln(y, om, wts["w_out_a"][j], x2, g1, b1)
            new_sc.append(to_bm(hs))
        else:
            q, k, v, gb, zg, qm, hg = _proj_b_front_sample(
                x2, gdn_hist[j].reshape(bsz, -1), wts["w_in_b"][j], conv_b[j],
                wts["a_row"][j], wts["dt_row"][j], t_len=t_len, bsz=bsz)
            om = _attn_sample(qm.reshape(t_len, bsz, XDIM), mk, mv).reshape(rows, XDIM)
            padt = lambda a: jnp.pad(to_bm(a.reshape(t_len, bsz, -1)), ((0, 0), (0, t_pad - t_len), (0, 0)))
            o, sg = _gdn_scan(padt(q), padt(k), padt(v), padt(gb), padt(zg), gdn_norm_w[j][None],
                              gdn_s[j], c=t_pad)
            o = jnp.transpose(o[:, :t_len], (1, 0, 2)).reshape(rows, GDN_DIM)
            x2 = _out_ln(o, om, wts["w_out_b"][j], x2, g1, b1)
            new_gc.append(to_bm(hg))
            new_gs.append(sg)
        x2, hgo, huo = _ffn_sample(x2, ffn_hist[i].reshape(bsz, -1), wts["w_up"][i], w_conv_ffn[i],
                                   wts["w_down"][i], ln2_g[i][None], ln2_b[i][None],
                                   t_len=t_len, bsz=bsz)
        new_ffn.append(to_bm(jnp.concatenate([hgo, huo], axis=-1)))
    y = jnp.transpose(x2.reshape(t_len, bsz, D_MODEL), (1, 0, 2))
    return y, jnp.stack(new_sc), jnp.stack(new_gc), jnp.stack(new_gs), jnp.stack(new_ffn)


def kernel(x_prompt, x_sample, mem_prompt, cache_mem_k, cache_mem_v, state_shortconv, state_gdn_conv,
           state_gdn, state_ffn_conv, w_in_a, conv_a, w_out_a, w_in_b, conv_b, a_log, dt_bias,
           gdn_norm_w, w_out_b, w_mem_kv, ln1_g, ln1_b, ln2_g, ln2_b, w_up, w_conv_ffn, w_down):
    wts = _prep_weights(w_in_a, w_out_a, w_in_b, a_log, dt_bias, w_out_b, w_mem_kv, w_up, w_down)
    shared = (wts, conv_a, conv_b, gdn_norm_w, ln1_g, ln1_b, ln2_g, ln2_b, w_conv_ffn)
    bsz = x_prompt.shape[0]
    k2, v2 = _mem_kv(mem_prompt.reshape(bsz * N_MEM, D_MODEL), wts["w_mem_kv"])
    mem_k_prompt = k2.reshape(DEPTH, bsz, N_MEM, XDIM)
    mem_v_prompt = v2.reshape(DEPTH, bsz, N_MEM, XDIM)
    y_prompt, sc_p, gc_p, gs_p, ffn_p = _trunk_prompt(x_prompt, mem_k_prompt, mem_v_prompt, *shared)
    y_sample, sc_s, gc_s, gs_s, ffn_s = _trunk_sample(
        x_sample, cache_mem_k, cache_mem_v, state_shortconv, state_gdn_conv, state_gdn,
        state_ffn_conv, *shared)
    shape5 = (DEPTH, bsz, N_MEM, XH, XD)
    return (y_prompt, y_sample, mem_k_prompt.reshape(shape5), mem_v_prompt.reshape(shape5),
            sc_p, gc_p, gs_p, ffn_p, sc_s, gc_s, gs_s, ffn_s)
```

```python
import functools

import jax
import jax.numpy as jnp
from jax import lax
from jax.experimental import pallas as pl
from jax.experimental.pallas import tpu as pltpu

F32 = jnp.float32
BF16 = jnp.bfloat16

DEPTH = 4
D_MODEL = 1024
SC_DIM = 768
SC_W = 3
GDN_H = 6
GDN_DK = 128
GDN_DV = 128
GDN_DIM = GDN_H * GDN_DK
GDN_CONV_W = 4
GDN_CHUNK = 64
N_MEM = 256
XH = 4
XD = 64
XDIM = XH * XD
D_FF = 2816
FFN_W = 3
ALPHA = (2.0 * DEPTH) ** 0.25
LN_EPS = 1e-5
RMS_EPS = 1e-6

V7X_VMEM_BYTES = 64 * 1024 * 1024
VMEM_LIMIT = V7X_VMEM_BYTES - 8 * 1024 * 1024
SUBLANES = 8
LANES = 128

GB_LANES = LANES

SEQ_TILE = 512
SCAN_SEQS = 2
SCAN_CHUNKS = 2
SCAN_SEQS_SAMPLE = 4
ROW_TILE = 512
FF_TILE = 1408
ATTN_BATCH_BLOCK = 16
W_B_COLS = 3 * GDN_DIM + GDN_DIM + XDIM + GB_LANES


def _cparams(sem):
    return pltpu.CompilerParams(dimension_semantics=sem, vmem_limit_bytes=VMEM_LIMIT)


def _resident(shape):
    nd = len(shape)
    return pl.BlockSpec(shape, lambda *_: (0,) * nd, pipeline_mode=pl.Buffered(1))


def _silu(x):
    return x * (1.0 / (1.0 + jnp.exp(-x)))


def _sigmoid(x):
    return 1.0 / (1.0 + jnp.exp(-x))


def _softplus(x):
    return jnp.maximum(x, 0.0) + jnp.log(1.0 + jnp.exp(-jnp.abs(x)))


def _layer_norm(v, g, b):
    mu = jnp.mean(v, -1, keepdims=True)
    d = v - mu
    var = jnp.mean(d * d, -1, keepdims=True)
    return d * lax.rsqrt(var + LN_EPS) * g + b


def _dot(a, b):
    return jnp.dot(a, b, preferred_element_type=F32)


def _dot_nt(a, b):
    return lax.dot_general(a, b, (((1,), (1,)), ((), ())), preferred_element_type=F32)


def _dot_tn(a, b):
    return lax.dot_general(a, b, (((0,), (0,)), ((), ())), preferred_element_type=F32)


def _dot_hi(a, b):
    return jnp.dot(a, b, preferred_element_type=F32, precision=lax.Precision.HIGHEST)


def _dot_nt_hi(a, b):
    return lax.dot_general(a, b, (((1,), (1,)), ((), ())), preferred_element_type=F32,
                           precision=lax.Precision.HIGHEST)


def _kv_kernel(m_ref, w_ref, k_ref, v_ref):
    kv = _dot(m_ref[...].astype(BF16), w_ref[0])
    k_ref[0] = kv[:, :XDIM]
    v_ref[0] = kv[:, XDIM:]


def _mem_kv(mem2d, w_kv):
    rows = mem2d.shape[0]
    out = jax.ShapeDtypeStruct((DEPTH, rows, XDIM), F32)
    return pl.pallas_call(
        _kv_kernel,
        grid=(DEPTH,),
        in_specs=[_resident((rows, D_MODEL)),
                  pl.BlockSpec((1, D_MODEL, 2 * XDIM), lambda l: (l, 0, 0))],
        out_specs=[pl.BlockSpec((1, rows, XDIM), lambda l: (l, 0, 0))] * 2,
        out_shape=[out, out],
        compiler_params=_cparams(("arbitrary",)),
        name="mem_kv",
    )(mem2d, w_kv)


def _head_blockdiag(kv):
    lane_head = lax.broadcasted_iota(jnp.int32, kv.shape, 1) // XD
    return jnp.concatenate(
        [jnp.where(lane_head == h, kv, 0.0).astype(BF16) for h in range(XH)], axis=0)


def _softmax_rows(s):
    m = jnp.max(s, -1, keepdims=True)
    e = jnp.exp(s - m)
    return e / jnp.sum(e, -1, keepdims=True)


def _mem_attn_seq(qm, kbd, vbd):
    s = _dot_nt(qm.astype(BF16), kbd) * (XD ** -0.5)
    p = jnp.concatenate(
        [_softmax_rows(s[:, h * N_MEM:(h + 1) * N_MEM]).astype(BF16) for h in range(XH)], axis=-1)
    return _dot(p, vbd)


def _attn_sample_kernel(q_ref, k_ref, v_ref, o_ref, *, bb, t_len):
    lane_head = lax.broadcasted_iota(jnp.int32, (SUBLANES, XDIM), 1) // XD
    row_ok = lax.broadcasted_iota(jnp.int32, (SUBLANES, XDIM), 0) < t_len

    def body(i, carry):
        rows = [q_ref[t, pl.ds(i, 1), :] for t in range(t_len)]
        rows.append(jnp.zeros((SUBLANES - t_len, XDIM), F32))
        q8 = jnp.concatenate(rows, axis=0)
        qblk = jnp.concatenate(
            [jnp.where(lane_head == h, q8, 0.0).astype(BF16) for h in range(XH)], axis=0)
        s = _dot_nt(qblk, k_ref[i].astype(BF16)) * (XD ** -0.5)
        p = _softmax_rows(s).astype(BF16)
        o_full = _dot(p, v_ref[i].astype(BF16))
        o8 = jnp.zeros((SUBLANES, XDIM), F32)
        for h in range(XH):
            o8 = o8 + jnp.where(lane_head == h, o_full[h * SUBLANES:(h + 1) * SUBLANES], 0.0)
        o8 = jnp.where(row_ok, o8, 0.0).astype(o_ref.dtype)
        for t in range(t_len):
            o_ref[t, pl.ds(i, 1), :] = o8[t:t + 1]
        return carry

    lax.fori_loop(0, bb, body, 0)


def _attn_sample(qm_tm, mem_k, mem_v):
    t_len, bsz, _ = qm_tm.shape
    bb = min(ATTN_BATCH_BLOCK, bsz)
    return pl.pallas_call(
        functools.partial(_attn_sample_kernel, bb=bb, t_len=t_len),
        grid=(bsz // bb,),
        in_specs=[pl.BlockSpec((t_len, bb, XDIM), lambda i: (0, i, 0)),
                  pl.BlockSpec((bb, N_MEM, XDIM), lambda i: (i, 0, 0)),
                  pl.BlockSpec((bb, N_MEM, XDIM), lambda i: (i, 0, 0))],
        out_specs=pl.BlockSpec((t_len, bb, XDIM), lambda i: (0, i, 0)),
        out_shape=jax.ShapeDtypeStruct((t_len, bsz, XDIM), F32),
        compiler_params=_cparams(("arbitrary",)),
        name="attn_sample",
    )(qm_tm, mem_k, mem_v)


def _mix_a_seq_kernel(x_ref, hist_ref, win_ref, wc_ref, wout_ref, k_ref, v_ref, g_ref, b_ref,
                      o_ref, hist_o_ref, ubuf, kbd, vbd, *, tm):
    t = pl.program_id(1)
    lo = SUBLANES - (SC_W - 1)

    @pl.when(t == 0)
    def _():
        ubuf[lo:SUBLANES, :] = hist_ref[0]
        kbd[...] = _head_blockdiag(k_ref[0])
        vbd[...] = _head_blockdiag(v_ref[0])

    x = x_ref[0]
    h = _dot(x.astype(BF16), win_ref[...])
    xin = h[:, :SC_DIM]
    bg = h[:, SC_DIM:2 * SC_DIM]
    cg = h[:, 2 * SC_DIM:3 * SC_DIM]
    qm = h[:, 3 * SC_DIM:]
    u = cg * xin
    ubuf[SUBLANES:SUBLANES + tm, :] = u
    wc = wc_ref[...]
    conv = wc[SC_W - 1:SC_W] * u
    for j in range(SC_W - 1):
        conv = conv + wc[j:j + 1] * ubuf[lo + j:lo + j + tm, :]
    y = bg * conv
    last = ubuf[tm + lo:tm + SUBLANES, :]
    ubuf[lo:SUBLANES, :] = last
    hist_o_ref[0] = last

    o_mem = _mem_attn_seq(qm, kbd[...], vbd[...])
    mix = _dot(y.astype(BF16), wout_ref[:SC_DIM, :]) + _dot(o_mem.astype(BF16), wout_ref[SC_DIM:, :])
    o_ref[0] = _layer_norm(ALPHA * x + mix, g_ref[...], b_ref[...])


def _mix_a_seq(x, hist, w_in, w_conv, w_out, mem_k, mem_v, ln_g, ln_b):
    bsz, t_len, _ = x.shape
    tm = min(SEQ_TILE, t_len)
    n_in = w_in.shape[1]
    return pl.pallas_call(
        functools.partial(_mix_a_seq_kernel, tm=tm),
        grid=(bsz, t_len // tm),
        in_specs=[pl.BlockSpec((1, tm, D_MODEL), lambda b, t: (b, t, 0)),
                  pl.BlockSpec((1, SC_W - 1, SC_DIM), lambda b, t: (b, 0, 0)),
                  _resident((D_MODEL, n_in)),
                  _resident((SC_W, SC_DIM)),
                  _resident((SC_DIM + XDIM, D_MODEL)),
                  pl.BlockSpec((1, N_MEM, XDIM), lambda b, t: (b, 0, 0)),
                  pl.BlockSpec((1, N_MEM, XDIM), lambda b, t: (b, 0, 0)),
                  _resident((1, D_MODEL)),
                  _resident((1, D_MODEL))],
        out_specs=[pl.BlockSpec((1, tm, D_MODEL), lambda b, t: (b, t, 0)),
                   pl.BlockSpec((1, SC_W - 1, SC_DIM), lambda b, t: (b, 0, 0))],
        out_shape=[jax.ShapeDtypeStruct((bsz, t_len, D_MODEL), F32),
                   jax.ShapeDtypeStruct((bsz, SC_W - 1, SC_DIM), F32)],
        scratch_shapes=[pltpu.VMEM((tm + SUBLANES, SC_DIM), F32),
                        pltpu.VMEM((XH * N_MEM, XDIM), BF16),
                        pltpu.VMEM((XH * N_MEM, XDIM), BF16)],
        compiler_params=_cparams(("arbitrary", "arbitrary")),
        name="mix_a_seq",
    )(x, hist, w_in, w_conv, w_out, mem_k, mem_v, ln_g, ln_b)


def _ffn_seq_kernel(x_ref, hist_ref, wup_ref, wc_ref, wdn_ref, g_ref, b_ref,
                    o_ref, hist_o_ref, gbuf, ubuf, *, tm, tf):
    t = pl.program_id(1)
    lo = SUBLANES - (FFN_W - 1)
    nj = D_FF // tf
    halves = ((gbuf, 0), (ubuf, D_FF))

    @pl.when(t == 0)
    def _():
        for buf, off in halves:
            for j in range(nj):
                buf[j, lo:SUBLANES, :] = hist_ref[0, :, off + j * tf:off + (j + 1) * tf]

    x = x_ref[0]
    xb = x.astype(BF16)
    acc = None
    for j in range(nj):
        conv = []
        for buf, off in halves:
            c0 = off + j * tf
            h = _dot(xb, wup_ref[:, c0:c0 + tf])
            buf[j, SUBLANES:SUBLANES + tm, :] = h
            c = wc_ref[FFN_W - 1:FFN_W, c0:c0 + tf] * h
            for w in range(FFN_W - 1):
                c = c + wc_ref[w:w + 1, c0:c0 + tf] * buf[j, lo + w:lo + w + tm, :]
            last = buf[j, tm + lo:tm + SUBLANES, :]
            buf[j, lo:SUBLANES, :] = last
            hist_o_ref[0, :, c0:c0 + tf] = last
            conv.append(c)
        act = (_silu(conv[0]) * conv[1]).astype(BF16)
        d = _dot(act, wdn_ref[j * tf:(j + 1) * tf, :])
        acc = d if acc is None else acc + d
    o_ref[0] = _layer_norm(ALPHA * x + acc, g_ref[...], b_ref[...])


def _ffn_seq(x, hist, w_up, w_conv, w_down, ln_g, ln_b):
    bsz, t_len, _ = x.shape
    tm = min(SEQ_TILE, t_len)
    tf = FF_TILE
    return pl.pallas_call(
        functools.partial(_ffn_seq_kernel, tm=tm, tf=tf),
        grid=(bsz, t_len // tm),
        in_specs=[pl.BlockSpec((1, tm, D_MODEL), lambda b, t: (b, t, 0)),
                  pl.BlockSpec((1, FFN_W - 1, 2 * D_FF), lambda b, t: (b, 0, 0)),
                  _resident((D_MODEL, 2 * D_FF)),
                  _resident((FFN_W, 2 * D_FF)),
                  _resident((D_FF, D_MODEL)),
                  _resident((1, D_MODEL)),
                  _resident((1, D_MODEL))],
        out_specs=[pl.BlockSpec((1, tm, D_MODEL), lambda b, t: (b, t, 0)),
                   pl.BlockSpec((1, FFN_W - 1, 2 * D_FF), lambda b, t: (b, 0, 0))],
        out_shape=[jax.ShapeDtypeStruct((bsz, t_len, D_MODEL), F32),
                   jax.ShapeDtypeStruct((bsz, FFN_W - 1, 2 * D_FF), F32)],
        scratch_shapes=[pltpu.VMEM((D_FF // tf, tm + SUBLANES, tf), F32),
                        pltpu.VMEM((D_FF // tf, tm + SUBLANES, tf), F32)],
        compiler_params=_cparams(("arbitrary", "arbitrary")),
        name="ffn_seq",
    )(x, hist, w_up, w_conv, w_down, ln_g, ln_b)


def _gdn_gates(ba, a_row, dt_row):
    lane = lax.broadcasted_iota(jnp.int32, ba.shape, 1)
    return jnp.where(lane < GDN_H, _sigmoid(ba), a_row * _softplus(ba + dt_row))


def _qkv_post(c, q_ref, k_ref, v_ref, idx):
    c = _silu(c)
    for h in range(GDN_H):
        for ref, base in ((q_ref, 0), (k_ref, GDN_DIM)):
            a = c[:, base + h * GDN_DK:base + (h + 1) * GDN_DK]
            a = a * lax.rsqrt(jnp.sum(a * a, -1, keepdims=True) + RMS_EPS)
            ref[idx + (slice(None), slice(h * GDN_DK, (h + 1) * GDN_DK))] = a
    v_ref[idx + (slice(None), slice(None))] = c[:, 2 * GDN_DIM:]


def _proj_b_seq_kernel(x_ref, hist_ref, win_ref, wc_ref, arow_ref, dtrow_ref, k_ref, v_ref,
                       q_o, k_o, v_o, gb_o, zg_o, om_o, hist_o_ref, sbuf, kbd, vbd, *, tm):
    t = pl.program_id(1)
    lo = SUBLANES - (GDN_CONV_W - 1)
    nqkv = 3 * GDN_DIM

    @pl.when(t == 0)
    def _():
        sbuf[lo:SUBLANES, :] = hist_ref[0]
        kbd[...] = _head_blockdiag(k_ref[0])
        vbd[...] = _head_blockdiag(v_ref[0])

    h = _dot(x_ref[0].astype(BF16), win_ref[...])
    qkv = h[:, :nqkv]
    sbuf[SUBLANES:SUBLANES + tm, :] = qkv
    wc = wc_ref[...]
    conv = wc[GDN_CONV_W - 1:GDN_CONV_W] * qkv
    for j in range(GDN_CONV_W - 1):
        conv = conv + wc[j:j + 1] * sbuf[lo + j:lo + j + tm, :]
    last = sbuf[tm + lo:tm + SUBLANES, :]
    sbuf[lo:SUBLANES, :] = last
    hist_o_ref[0] = last

    _qkv_post(conv, q_o, k_o, v_o, (0,))
    zg_o[0] = _silu(h[:, nqkv:nqkv + GDN_DIM])
    qm = h[:, nqkv + GDN_DIM:nqkv + GDN_DIM + XDIM]
    gb_o[0] = _gdn_gates(h[:, nqkv + GDN_DIM + XDIM:], arow_ref[...], dtrow_ref[...])
    om_o[0] = _mem_attn_seq(qm, kbd[...], vbd[...]).astype(BF16)


def _proj_b_seq(x, hist, w_in, w_conv, a_row, dt_row, mem_k, mem_v):
    bsz, t_len, _ = x.shape
    tm = min(SEQ_TILE, t_len)
    tile = lambda n: pl.BlockSpec((1, tm, n), lambda b, t: (b, t, 0))
    f32o = lambda n: jax.ShapeDtypeStruct((bsz, t_len, n), F32)
    return pl.pallas_call(
        functools.partial(_proj_b_seq_kernel, tm=tm),
        grid=(bsz, t_len // tm),
        in_specs=[tile(D_MODEL),
                  pl.BlockSpec((1, GDN_CONV_W - 1, 3 * GDN_DIM), lambda b, t: (b, 0, 0)),
                  _resident((D_MODEL, W_B_COLS)),
                  _resident((GDN_CONV_W, 3 * GDN_DIM)),
                  _resident((1, GB_LANES)),
                  _resident((1, GB_LANES)),
                  pl.BlockSpec((1, N_MEM, XDIM), lambda b, t: (b, 0, 0)),
                  pl.BlockSpec((1, N_MEM, XDIM), lambda b, t: (b, 0, 0))],
        out_specs=[tile(GDN_DIM), tile(GDN_DIM), tile(GDN_DIM), tile(GB_LANES), tile(GDN_DIM),
                   tile(XDIM),
                   pl.BlockSpec((1, GDN_CONV_W - 1, 3 * GDN_DIM), lambda b, t: (b, 0, 0))],
        out_shape=[f32o(GDN_DIM), f32o(GDN_DIM), f32o(GDN_DIM), f32o(GB_LANES), f32o(GDN_DIM),
                   jax.ShapeDtypeStruct((bsz, t_len, XDIM), BF16),
                   jax.ShapeDtypeStruct((bsz, GDN_CONV_W - 1, 3 * GDN_DIM), F32)],
        scratch_shapes=[pltpu.VMEM((tm + SUBLANES, 3 * GDN_DIM), F32),
                        pltpu.VMEM((XH * N_MEM, XDIM), BF16),
                        pltpu.VMEM((XH * N_MEM, XDIM), BF16)],
        compiler_params=_cparams(("arbitrary", "arbitrary")),
        name="proj_b_seq",
    )(x, hist, w_in, w_conv, a_row, dt_row, mem_k, mem_v)


def _gdn_scan_kernel(q_ref, k_ref, v_ref, gb_ref, zg_ref, nw_ref, s0_ref, o_ref, s_o_ref, s_scr,
                     *, bb, nc, c):
    t = pl.program_id(1)
    tc = nc * c

    @pl.when(t == 0)
    def _():
        s_scr[...] = s0_ref[...]

    ri = lax.broadcasted_iota(jnp.int32, (c, c), 0)
    ci = lax.broadcasted_iota(jnp.int32, (c, c), 1)
    tril = ri >= ci
    strict = ri > ci
    rt = lax.broadcasted_iota(jnp.int32, (tc, tc), 0)
    ct = lax.broadcasted_iota(jnp.int32, (tc, tc), 1)
    tril_chunks = ((rt >= ct) & ((rt // c) == (ct // c))).astype(F32)
    eye_l = (lax.broadcasted_iota(jnp.int32, (GB_LANES, GB_LANES), 0)
             == lax.broadcasted_iota(jnp.int32, (GB_LANES, GB_LANES), 1)).astype(F32)
    nw = nw_ref[...]
    n_sq = max(c.bit_length() - 2, 0)

    gcs = []
    for i in range(bb):
        gc_all = _dot_hi(tril_chunks, gb_ref[i])
        gcs.append((gc_all, _dot_nt_hi(eye_l, gc_all)))

    probs = [(i, ic, h) for i in range(bb) for ic in range(nc) for h in range(GDN_H)]
    st = {}
    for p in probs:
        i, ic, h = p
        rows = slice(ic * c, (ic + 1) * c)
        hs = slice(h * GDN_DK, (h + 1) * GDN_DK)
        gc_all, gc_t = gcs[i]
        q = q_ref[i, rows, hs] * (GDN_DK ** -0.5)
        k = k_ref[i, rows, hs]
        gcol = gc_all[rows, GDN_H + h:GDN_H + h + 1]
        grow = gc_t[GDN_H + h:GDN_H + h + 1, rows]
        glast = gc_all[(ic + 1) * c - 1:(ic + 1) * c, GDN_H + h:GDN_H + h + 1]
        egc = jnp.exp(gcol)
        kb = k * gb_ref[i, rows, h:h + 1]
        st[p] = dict(
            decay=jnp.exp(jnp.where(tril, gcol - grow, -jnp.inf)),
            aq=_dot_nt(jnp.concatenate([kb, q], axis=0).astype(BF16), k.astype(BF16)),
            sol=jnp.concatenate([v_ref[i, rows, hs] * gb_ref[i, rows, h:h + 1], kb * egc], axis=-1),
            qd=(q * egc).astype(BF16),
            k_dec=(k * jnp.exp(glast - gcol)).astype(BF16),
            e_last=jnp.exp(glast))
    for p in probs:
        d = st[p]
        d["pb"] = jnp.where(strict, d["aq"][:c] * d["decay"], 0.0).astype(BF16)
        d["qk"] = (d["aq"][c:] * d["decay"]).astype(BF16)
        d["sol"] = d["sol"] - _dot(d["pb"], d["sol"].astype(BF16))
    for _ in range(n_sq):
        for p in probs:
            st[p]["pb"] = _dot(st[p]["pb"], st[p]["pb"]).astype(BF16)
        for p in probs:
            st[p]["sol"] = st[p]["sol"] + _dot(st[p]["pb"], st[p]["sol"].astype(BF16))
    for p in probs:
        d = st[p]
        d["wq"] = jnp.concatenate([d["sol"][:, GDN_DV:].astype(BF16), d["qd"]], axis=0)
        d["u"] = d["sol"][:, :GDN_DV]

    for ic in range(nc):
        rows = slice(ic * c, (ic + 1) * c)
        seqs = [(i, h) for i in range(bb) for h in range(GDN_H)]
        s_old = {ih: s_scr[ih[0], ih[1]] for ih in seqs}
        ws = {ih: _dot(st[ih[0], ic, ih[1]]["wq"], s_old[ih].astype(BF16)) for ih in seqs}
        vb = {ih: (st[ih[0], ic, ih[1]]["u"] - ws[ih][:c]).astype(BF16) for ih in seqs}
        for ih in seqs:
            d = st[ih[0], ic, ih[1]]
            s_scr[ih[0], ih[1]] = s_old[ih] * d["e_last"] + _dot_tn(d["k_dec"], vb[ih])
        for ih in seqs:
            i, h = ih
            hs = slice(h * GDN_DK, (h + 1) * GDN_DK)
            o = ws[ih][c:] + _dot(st[i, ic, h]["qk"], vb[ih])
            o = o * lax.rsqrt(jnp.mean(o * o, -1, keepdims=True) + RMS_EPS)
            o_ref[i, rows, hs] = (o * nw * zg_ref[i, rows, hs]).astype(o_ref.dtype)
    s_o_ref[...] = s_scr[...]


def _gdn_scan(q, k, v, gb, zg, norm_w, s0, *, c, bb, nc):
    bsz, t_len, _ = q.shape
    tc = nc * c
    tile = lambda n: pl.BlockSpec((bb, tc, n), lambda b, t: (b, t, 0))
    st = pl.BlockSpec((bb, GDN_H, GDN_DK, GDN_DV), lambda b, t: (b, 0, 0, 0))
    return pl.pallas_call(
        functools.partial(_gdn_scan_kernel, bb=bb, nc=nc, c=c),
        grid=(bsz // bb, t_len // tc),
        in_specs=[tile(GDN_DIM), tile(GDN_DIM), tile(GDN_DIM), tile(GB_LANES), tile(GDN_DIM),
                  _resident((1, GDN_DV)), st],
        out_specs=[tile(GDN_DIM), st],
        out_shape=[jax.ShapeDtypeStruct((bsz, t_len, GDN_DIM), BF16),
                   jax.ShapeDtypeStruct((bsz, GDN_H, GDN_DK, GDN_DV), F32)],
        scratch_shapes=[pltpu.VMEM((bb, GDN_H, GDN_DK, GDN_DV), F32)],
        compiler_params=_cparams(("arbitrary", "arbitrary")),
        name="gdn_scan",
    )(q, k, v, gb, zg, norm_w, s0)


def _out_ln_kernel(a1_ref, a2_ref, w_ref, x_ref, g_ref, b_ref, o_ref):
    n1 = a1_ref.shape[-1]
    mix = (_dot(a1_ref[...].astype(BF16), w_ref[:n1, :])
           + _dot(a2_ref[...].astype(BF16), w_ref[n1:, :]))
    o_ref[...] = _layer_norm(ALPHA * x_ref[...] + mix, g_ref[...], b_ref[...])


def _out_ln(a1, a2, w, x, ln_g, ln_b):
    rows = x.shape[0]
    tr = min(ROW_TILE, rows)
    n1, n2 = a1.shape[1], a2.shape[1]
    tile = lambda n: pl.BlockSpec((tr, n), lambda i: (i, 0))
    return pl.pallas_call(
        _out_ln_kernel,
        grid=(rows // tr,),
        in_specs=[tile(n1), tile(n2), _resident((n1 + n2, D_MODEL)), tile(D_MODEL),
                  _resident((1, D_MODEL)), _resident((1, D_MODEL))],
        out_specs=tile(D_MODEL),
        out_shape=jax.ShapeDtypeStruct((rows, D_MODEL), F32),
        compiler_params=_cparams(("arbitrary",)),
        name="out_ln",
    )(a1, a2, w, x, ln_g, ln_b)


def _mix_a_front_sample_kernel(x_ref, hist_ref, win_ref, wc_ref, y_o, qm_o, hist_o, *, t_len, bsz):
    h = _dot(x_ref[...].astype(BF16), win_ref[...])
    u = h[:, 2 * SC_DIM:3 * SC_DIM] * h[:, :SC_DIM]
    slabs = [hist_ref[:, j * SC_DIM:(j + 1) * SC_DIM] for j in range(SC_W - 1)]
    slabs += [u[t * bsz:(t + 1) * bsz] for t in range(t_len)]
    wc = wc_ref[...]
    for t in range(t_len):
        conv = wc[0:1] * slabs[t]
        for j in range(1, SC_W):
            conv = conv + wc[j:j + 1] * slabs[t + j]
        y_o[t * bsz:(t + 1) * bsz, :] = (h[t * bsz:(t + 1) * bsz, SC_DIM:2 * SC_DIM] * conv).astype(BF16)
    for j in range(SC_W - 1):
        hist_o[j] = slabs[t_len + j]
    qm_o[...] = h[:, 3 * SC_DIM:]


def _mix_a_front_sample(x2d, hist2d, w_in, w_conv, *, t_len, bsz):
    rows = x2d.shape[0]
    return pl.pallas_call(
        functools.partial(_mix_a_front_sample_kernel, t_len=t_len, bsz=bsz),
        out_shape=[jax.ShapeDtypeStruct((rows, SC_DIM), BF16),
                   jax.ShapeDtypeStruct((rows, XDIM), F32),
                   jax.ShapeDtypeStruct((SC_W - 1, bsz, SC_DIM), F32)],
        compiler_params=pltpu.CompilerParams(vmem_limit_bytes=VMEM_LIMIT),
        name="mix_a_front_sample",
    )(x2d, hist2d, w_in, w_conv)


def _proj_b_front_sample_kernel(x_ref, hist_ref, win_ref, wc_ref, arow_ref, dtrow_ref,
                                q_o, k_o, v_o, gb_o, zg_o, qm_o, hist_o, *, t_len, bsz):
    nqkv = 3 * GDN_DIM
    h = _dot(x_ref[...].astype(BF16), win_ref[...])
    slabs = [hist_ref[:, j * nqkv:(j + 1) * nqkv] for j in range(GDN_CONV_W - 1)]
    slabs += [h[t * bsz:(t + 1) * bsz, :nqkv] for t in range(t_len)]
    wc = wc_ref[...]
    for t in range(t_len):
        conv = wc[0:1] * slabs[t]
        for j in range(1, GDN_CONV_W):
            conv = conv + wc[j:j + 1] * slabs[t + j]
        _qkv_post(conv, q_o, k_o, v_o, (t,))
    for j in range(GDN_CONV_W - 1):
        hist_o[j] = slabs[t_len + j]
    zg_o[...] = _silu(h[:, nqkv:nqkv + GDN_DIM])
    qm_o[...] = h[:, nqkv + GDN_DIM:nqkv + GDN_DIM + XDIM]
    gb_o[...] = _gdn_gates(h[:, nqkv + GDN_DIM + XDIM:], arow_ref[...], dtrow_ref[...])


def _proj_b_front_sample(x2d, hist2d, w_in, w_conv, a_row, dt_row, *, t_len, bsz):
    rows = x2d.shape[0]
    tmaj = jax.ShapeDtypeStruct((t_len, bsz, GDN_DIM), F32)
    return pl.pallas_call(
        functools.partial(_proj_b_front_sample_kernel, t_len=t_len, bsz=bsz),
        out_shape=[tmaj, tmaj, tmaj,
                   jax.ShapeDtypeStruct((rows, GB_LANES), F32),
                   jax.ShapeDtypeStruct((rows, GDN_DIM), F32),
                   jax.ShapeDtypeStruct((rows, XDIM), F32),
                   jax.ShapeDtypeStruct((GDN_CONV_W - 1, bsz, 3 * GDN_DIM), F32)],
        compiler_params=pltpu.CompilerParams(vmem_limit_bytes=VMEM_LIMIT),
        name="proj_b_front_sample",
    )(x2d, hist2d, w_in, w_conv, a_row, dt_row)


def _ffn_sample_kernel(x_ref, hg0, hg1, hu0, hu1, wg_ref, wu_ref, cg_ref, cu_ref, wdn_ref,
                       g_ref, b_ref, o_ref, hgo, huo, acc, *, t_len, bsz):
    j = pl.program_id(0)

    @pl.when(j == 0)
    def _():
        acc[...] = jnp.zeros_like(acc)

    xb = x_ref[...].astype(BF16)
    conv = []
    for w_ref, c_ref, hists, h_o in ((wg_ref, cg_ref, (hg0, hg1), hgo), (wu_ref, cu_ref, (hu0, hu1), huo)):
        h = _dot(xb, w_ref[...])
        slabs = [hists[0][...], hists[1][...]] + [h[t * bsz:(t + 1) * bsz] for t in range(t_len)]
        wc = c_ref[...]
        outs = []
        for t in range(t_len):
            c = wc[0:1] * slabs[t]
            for w in range(1, FFN_W):
                c = c + wc[w:w + 1] * slabs[t + w]
            outs.append(c)
        for w in range(FFN_W - 1):
            h_o[w] = slabs[t_len + w]
        conv.append(jnp.concatenate(outs, axis=0))
    act = (_silu(conv[0]) * conv[1]).astype(BF16)
    acc[...] += _dot(act, wdn_ref[...])

    @pl.when(j == pl.num_programs(0) - 1)
    def _():
        o_ref[...] = _layer_norm(ALPHA * x_ref[...] + acc[...], g_ref[...], b_ref[...])


def _ffn_sample(x2d, hist2d, w_up, w_conv, w_down, ln_g, ln_b, *, t_len, bsz):
    rows = x2d.shape[0]
    tf = FF_TILE
    nj = D_FF // tf
    hspec = lambda r, off: pl.BlockSpec((bsz, tf), lambda j: (0, r * (2 * D_FF // tf) + off + j))
    hout = jax.ShapeDtypeStruct((FFN_W - 1, bsz, D_FF), F32)
    return pl.pallas_call(
        functools.partial(_ffn_sample_kernel, t_len=t_len, bsz=bsz),
        grid=(nj,),
        in_specs=[_resident((rows, D_MODEL)),
                  hspec(0, 0), hspec(1, 0), hspec(0, nj), hspec(1, nj),
                  pl.BlockSpec((D_MODEL, tf), lambda j: (0, j)),
                  pl.BlockSpec((D_MODEL, tf), lambda j: (0, nj + j)),
                  pl.BlockSpec((FFN_W, tf), lambda j: (0, j)),
                  pl.BlockSpec((FFN_W, tf), lambda j: (0, nj + j)),
                  pl.BlockSpec((tf, D_MODEL), lambda j: (j, 0)),
                  _resident((1, D_MODEL)), _resident((1, D_MODEL))],
        out_specs=[pl.BlockSpec((rows, D_MODEL), lambda j: (0, 0)),
                   pl.BlockSpec((FFN_W - 1, bsz, tf), lambda j: (0, 0, j)),
                   pl.BlockSpec((FFN_W - 1, bsz, tf), lambda j: (0, 0, j))],
        out_shape=[jax.ShapeDtypeStruct((rows, D_MODEL), F32), hout, hout],
        scratch_shapes=[pltpu.VMEM((rows, D_MODEL), F32)],
        compiler_params=_cparams(("arbitrary",)),
        name="ffn_sample",
    )(x2d, hist2d, hist2d, hist2d, hist2d, w_up, w_up, w_conv, w_conv, w_down, ln_g, ln_b)


def _prep_weights(w_in_a, w_out_a, w_in_b, a_log, dt_bias, w_out_b, w_mem_kv, w_up, w_down):
    nqkvz = 4 * GDN_DIM
    w_b = jnp.concatenate(
        [w_in_b[..., :nqkvz], w_in_b[..., nqkvz + 2 * GDN_H:],
         jnp.pad(w_in_b[..., nqkvz:nqkvz + 2 * GDN_H], ((0, 0), (0, 0), (0, GB_LANES - 2 * GDN_H)))],
        axis=-1).astype(BF16)
    pad = GB_LANES - 2 * GDN_H
    a_row = jnp.pad(-jnp.exp(a_log.astype(F32)), ((0, 0), (GDN_H, pad)))[:, None, :]
    dt_row = jnp.pad(dt_bias.astype(F32), ((0, 0), (GDN_H, pad)))[:, None, :]
    return dict(w_in_a=w_in_a.astype(BF16), w_out_a=w_out_a.astype(BF16), w_in_b=w_b,
                w_out_b=w_out_b.astype(BF16), w_mem_kv=w_mem_kv.astype(BF16),
                w_up=w_up.astype(BF16), w_down=w_down.astype(BF16), a_row=a_row, dt_row=dt_row)


def _trunk_prompt(x, mem_k, mem_v, wts, conv_a, conv_b, gdn_norm_w, ln1_g, ln1_b, ln2_g, ln2_b,
                  w_conv_ffn):
    bsz, t_len, _ = x.shape
    z = lambda *s: jnp.zeros(s, F32)
    new_sc, new_gc, new_gs, new_ffn = [], [], [], []
    for i in range(DEPTH):
        j = i // 2
        g1, b1 = ln1_g[i][None], ln1_b[i][None]
        if i % 2 == 0:
            x, hs = _mix_a_seq(x, z(bsz, SC_W - 1, SC_DIM), wts["w_in_a"][j], conv_a[j],
                               wts["w_out_a"][j], mem_k[i], mem_v[i], g1, b1)
            new_sc.append(hs)
        else:
            q, k, v, gb, zg, om, hg = _proj_b_seq(
                x, z(bsz, GDN_CONV_W - 1, 3 * GDN_DIM), wts["w_in_b"][j], conv_b[j],
                wts["a_row"][j], wts["dt_row"][j], mem_k[i], mem_v[i])
            o, sg = _gdn_scan(q, k, v, gb, zg, gdn_norm_w[j][None], z(bsz, GDN_H, GDN_DK, GDN_DV),
                              c=GDN_CHUNK, bb=SCAN_SEQS, nc=SCAN_CHUNKS)
            rows = bsz * t_len
            x = _out_ln(o.reshape(rows, GDN_DIM), om.reshape(rows, XDIM), wts["w_out_b"][j],
                        x.reshape(rows, D_MODEL), g1, b1).reshape(bsz, t_len, D_MODEL)
            new_gc.append(hg)
            new_gs.append(sg)
        x, hf = _ffn_seq(x, z(bsz, FFN_W - 1, 2 * D_FF), wts["w_up"][i], w_conv_ffn[i],
                         wts["w_down"][i], ln2_g[i][None], ln2_b[i][None])
        new_ffn.append(hf)
    return x, jnp.stack(new_sc), jnp.stack(new_gc), jnp.stack(new_gs), jnp.stack(new_ffn)


def _trunk_sample(x, mem_k, mem_v, sc_hist, gdn_hist, gdn_s, ffn_hist, wts, conv_a, conv_b,
                  gdn_norm_w, ln1_g, ln1_b, ln2_g, ln2_b, w_conv_ffn):
    bsz, t_len, _ = x.shape
    rows = bsz * t_len
    t_pad = SUBLANES
    x2 = jnp.transpose(x, (1, 0, 2)).reshape(rows, D_MODEL)
    to_bm = lambda a: jnp.transpose(a, (1, 0, 2))
    new_sc, new_gc, new_gs, new_ffn = [], [], [], []
    for i in range(DEPTH):
        j = i // 2
        g1, b1 = ln1_g[i][None], ln1_b[i][None]
        mk = mem_k[i].reshape(bsz, N_MEM, XDIM)
        mv = mem_v[i].reshape(bsz, N_MEM, XDIM)
        if i % 2 == 0:
            y, qm, hs = _mix_a_front_sample(x2, sc_hist[j].reshape(bsz, -1), wts["w_in_a"][j],
                                            conv_a[j], t_len=t_len, bsz=bsz)
            om = _attn_sample(qm.reshape(t_len, bsz, XDIM), mk, mv).reshape(rows, XDIM)
            x2 = _out_ln(y, om, wts["w_out_a"][j], x2, g1, b1)
            new_sc.append(to_bm(hs))
        else:
            q, k, v, gb, zg, qm, hg = _proj_b_front_sample(
                x2, gdn_hist[j].reshape(bsz, -1), wts["w_in_b"][j], conv_b[j],
                wts["a_row"][j], wts["dt_row"][j], t_len=t_len, bsz=bsz)
            om = _attn_sample(qm.reshape(t_len, bsz, XDIM), mk, mv).reshape(rows, XDIM)
            padt = lambda a: jnp.pad(to_bm(a.reshape(t_len, bsz, -1)), ((0, 0), (0, t_pad - t_len), (0, 0)))
            o, sg = _gdn_scan(padt(q), padt(k), padt(v), padt(gb), padt(zg), gdn_norm_w[j][None],
                              gdn_s[j], c=t_pad, bb=SCAN_SEQS_SAMPLE, nc=1)
            o = jnp.transpose(o[:, :t_len], (1, 0, 2)).reshape(rows, GDN_DIM)
            x2 = _out_ln(o, om, wts["w_out_b"][j], x2, g1, b1)
            new_gc.append(to_bm(hg))
            new_gs.append(sg)
        x2, hgo, huo = _ffn_sample(x2, ffn_hist[i].reshape(bsz, -1), wts["w_up"][i], w_conv_ffn[i],
                                   wts["w_down"][i], ln2_g[i][None], ln2_b[i][None],
                                   t_len=t_len, bsz=bsz)
        new_ffn.append(to_bm(jnp.concatenate([hgo, huo], axis=-1)))
    y = jnp.transpose(x2.reshape(t_len, bsz, D_MODEL), (1, 0, 2))
    return y, jnp.stack(new_sc), jnp.stack(new_gc), jnp.stack(new_gs), jnp.stack(new_ffn)


def kernel(x_prompt, x_sample, mem_prompt, cache_mem_k, cache_mem_v, state_shortconv, state_gdn_conv,
           state_gdn, state_ffn_conv, w_in_a, conv_a, w_out_a, w_in_b, conv_b, a_log, dt_bias,
           gdn_norm_w, w_out_b, w_mem_kv, ln1_g, ln1_b, ln2_g, ln2_b, w_up, w_conv_ffn, w_down):
    wts = _prep_weights(w_in_a, w_out_a, w_in_b, a_log, dt_bias, w_out_b, w_mem_kv, w_up, w_down)
    shared = (wts, conv_a, conv_b, gdn_norm_w, ln1_g, ln1_b, ln2_g, ln2_b, w_conv_ffn)
    bsz = x_prompt.shape[0]
    k2, v2 = _mem_kv(mem_prompt.reshape(bsz * N_MEM, D_MODEL), wts["w_mem_kv"])
    mem_k_prompt = k2.reshape(DEPTH, bsz, N_MEM, XDIM)
    mem_v_prompt = v2.reshape(DEPTH, bsz, N_MEM, XDIM)
    y_prompt, sc_p, gc_p, gs_p, ffn_p = _trunk_prompt(x_prompt, mem_k_prompt, mem_v_prompt, *shared)
    y_sample, sc_s, gc_s, gs_s, ffn_s = _trunk_sample(
        x_sample, cache_mem_k, cache_mem_v, state_shortconv, state_gdn_conv, state_gdn,
        state_ffn_conv, *shared)
    shape5 = (DEPTH, bsz, N_MEM, XH, XD)
    return (y_prompt, y_sample, mem_k_prompt.reshape(shape5), mem_v_prompt.reshape(shape5),
            sc_p, gc_p, gs_p, ffn_p, sc_s, gc_s, gs_s, ffn_s)
```

```python
import functools

import jax
import jax.numpy as jnp
from jax import lax
from jax.experimental import pallas as pl
from jax.experimental.pallas import tpu as pltpu

F32 = jnp.float32
BF16 = jnp.bfloat16

DEPTH = 4
D_MODEL = 1024
SC_DIM = 768
SC_W = 3
GDN_H = 6
GDN_DK = 128
GDN_DV = 128
GDN_DIM = GDN_H * GDN_DK
GDN_CONV_W = 4
GDN_CHUNK = 64
N_MEM = 256
XH = 4
XD = 64
XDIM = XH * XD
D_FF = 2816
FFN_W = 3
ALPHA = (2.0 * DEPTH) ** 0.25
LN_EPS = 1e-5
RMS_EPS = 1e-6

V7X_VMEM_BYTES = 64 * 1024 * 1024
VMEM_LIMIT = V7X_VMEM_BYTES - 8 * 1024 * 1024
SUBLANES = 8
LANES = 128

GB_LANES = LANES

SEQ_TILE = 512
SCAN_SEQS = 2
SCAN_CHUNKS = 2
SCAN_SEQS_SAMPLE = 4
ROW_TILE = 512
FF_TILE = 1408
ATTN_BATCH_BLOCK = 16
W_B_COLS = 3 * GDN_DIM + GDN_DIM + XDIM + GB_LANES


def _cparams(sem):
    return pltpu.CompilerParams(dimension_semantics=sem, vmem_limit_bytes=VMEM_LIMIT)


def _resident(shape):
    nd = len(shape)
    return pl.BlockSpec(shape, lambda *_: (0,) * nd, pipeline_mode=pl.Buffered(1))


def _layer_block(shape, layer):
    nd = len(shape)
    return pl.BlockSpec((None,) + tuple(shape), lambda *_: (layer,) + (0,) * nd,
                        pipeline_mode=pl.Buffered(1))


def _silu(x):
    return x * (1.0 / (1.0 + jnp.exp(-x)))


def _sigmoid(x):
    return 1.0 / (1.0 + jnp.exp(-x))


def _softplus(x):
    return jnp.maximum(x, 0.0) + jnp.log(1.0 + jnp.exp(-jnp.abs(x)))


def _layer_norm(v, g, b):
    mu = jnp.mean(v, -1, keepdims=True)
    d = v - mu
    var = jnp.mean(d * d, -1, keepdims=True)
    return d * lax.rsqrt(var + LN_EPS) * g + b


def _dot(a, b):
    return jnp.dot(a, b, preferred_element_type=F32)


def _dot_nt(a, b):
    return lax.dot_general(a, b, (((1,), (1,)), ((), ())), preferred_element_type=F32)


def _dot_tn(a, b):
    return lax.dot_general(a, b, (((0,), (0,)), ((), ())), preferred_element_type=F32)


def _dot_split_rhs(a, x):
    hi = x.astype(BF16)
    lo = (x - hi.astype(F32)).astype(BF16)
    return _dot(a, hi) + _dot(a, lo)


def _dot_hi(a, b):
    return jnp.dot(a, b, preferred_element_type=F32, precision=lax.Precision.HIGHEST)


def _dot_nt_hi(a, b):
    return lax.dot_general(a, b, (((1,), (1,)), ((), ())), preferred_element_type=F32,
                           precision=lax.Precision.HIGHEST)


def _kv_kernel(m_ref, w_ref, k_ref, v_ref):
    kv = _dot(m_ref[...].astype(BF16), w_ref[0])
    k_ref[0] = kv[:, :XDIM]
    v_ref[0] = kv[:, XDIM:]


def _mem_kv(mem2d, w_kv):
    rows = mem2d.shape[0]
    out = jax.ShapeDtypeStruct((DEPTH, rows, XDIM), F32)
    return pl.pallas_call(
        _kv_kernel,
        grid=(DEPTH,),
        in_specs=[_resident((rows, D_MODEL)),
                  pl.BlockSpec((1, D_MODEL, 2 * XDIM), lambda l: (l, 0, 0))],
        out_specs=[pl.BlockSpec((1, rows, XDIM), lambda l: (l, 0, 0))] * 2,
        out_shape=[out, out],
        compiler_params=_cparams(("arbitrary",)),
        name="mem_kv",
    )(mem2d, w_kv)


def _head_blockdiag(kv):
    lane_head = lax.broadcasted_iota(jnp.int32, kv.shape, 1) // XD
    return jnp.concatenate(
        [jnp.where(lane_head == h, kv, 0.0).astype(BF16) for h in range(XH)], axis=0)


def _softmax_rows(s):
    m = jnp.max(s, -1, keepdims=True)
    e = jnp.exp(s - m)
    return e / jnp.sum(e, -1, keepdims=True)


def _mem_attn_seq(qm, kbd, vbd):
    s = _dot_nt(qm.astype(BF16), kbd) * (XD ** -0.5)
    p = jnp.concatenate(
        [_softmax_rows(s[:, h * N_MEM:(h + 1) * N_MEM]).astype(BF16) for h in range(XH)], axis=-1)
    return _dot(p, vbd)


def _attn_sample_kernel(q_ref, k_ref, v_ref, o_ref, *, bb, t_len):
    def body(i, carry):
        rows = [q_ref[t, pl.ds(i, 1), :] for t in range(t_len)]
        rows.append(jnp.zeros((SUBLANES - t_len, XDIM), F32))
        q8 = jnp.concatenate(rows, axis=0)
        outs = []
        for h in range(XH):
            qh = q8[:, h * XD:(h + 1) * XD].astype(BF16)
            s = _dot(qh, k_ref[0, i, h].astype(BF16)) * (XD ** -0.5)
            p = _softmax_rows(s).astype(BF16)
            outs.append(_dot_nt(p, v_ref[0, i, h].astype(BF16)))
        o8 = jnp.concatenate(outs, axis=-1)
        for t in range(t_len):
            o_ref[t, pl.ds(i, 1), :] = o8[t:t + 1]
        return carry

    lax.fori_loop(0, bb, body, 0)


def _attn_sample(qm_tm, mem_kt, mem_vt, layer):
    t_len, bsz, _ = qm_tm.shape
    bb = min(ATTN_BATCH_BLOCK, bsz)
    kv = pl.BlockSpec((1, bb, XH, XD, N_MEM), lambda i: (layer, i, 0, 0, 0))
    return pl.pallas_call(
        functools.partial(_attn_sample_kernel, bb=bb, t_len=t_len),
        grid=(bsz // bb,),
        in_specs=[pl.BlockSpec((t_len, bb, XDIM), lambda i: (0, i, 0)), kv, kv],
        out_specs=pl.BlockSpec((t_len, bb, XDIM), lambda i: (0, i, 0)),
        out_shape=jax.ShapeDtypeStruct((t_len, bsz, XDIM), F32),
        compiler_params=_cparams(("arbitrary",)),
        name="attn_sample",
    )(qm_tm, mem_kt, mem_vt)


def _mix_a_seq_kernel(x_ref, hist_ref, win_ref, wc_ref, wout_ref, k_ref, v_ref, g_ref, b_ref,
                      o_ref, hist_o_ref, ubuf, kbd, vbd, *, tm):
    t = pl.program_id(1)
    lo = SUBLANES - (SC_W - 1)

    @pl.when(t == 0)
    def _():
        ubuf[lo:SUBLANES, :] = hist_ref[0]
        kbd[...] = _head_blockdiag(k_ref[0])
        vbd[...] = _head_blockdiag(v_ref[0])

    x = x_ref[0]
    h = _dot(x.astype(BF16), win_ref[...])
    xin = h[:, :SC_DIM]
    bg = h[:, SC_DIM:2 * SC_DIM]
    cg = h[:, 2 * SC_DIM:3 * SC_DIM]
    qm = h[:, 3 * SC_DIM:]
    u = cg * xin
    ubuf[SUBLANES:SUBLANES + tm, :] = u
    wc = wc_ref[...]
    conv = wc[SC_W - 1:SC_W] * u
    for j in range(SC_W - 1):
        conv = conv + wc[j:j + 1] * ubuf[lo + j:lo + j + tm, :]
    y = bg * conv
    last = ubuf[tm + lo:tm + SUBLANES, :]
    ubuf[lo:SUBLANES, :] = last
    hist_o_ref[0] = last

    o_mem = _mem_attn_seq(qm, kbd[...], vbd[...])
    mix = _dot(y.astype(BF16), wout_ref[:SC_DIM, :]) + _dot(o_mem.astype(BF16), wout_ref[SC_DIM:, :])
    o_ref[0] = _layer_norm(ALPHA * x + mix, g_ref[...], b_ref[...])


def _mix_a_seq(x, hist, w_in, w_conv, w_out, mem_k, mem_v, ln_g, ln_b, *, j, i):
    bsz, t_len, _ = x.shape
    tm = min(SEQ_TILE, t_len)
    mem = pl.BlockSpec((None, 1, N_MEM, XDIM), lambda b, t: (i, b, 0, 0))
    return pl.pallas_call(
        functools.partial(_mix_a_seq_kernel, tm=tm),
        grid=(bsz, t_len // tm),
        in_specs=[pl.BlockSpec((1, tm, D_MODEL), lambda b, t: (b, t, 0)),
                  pl.BlockSpec((1, SC_W - 1, SC_DIM), lambda b, t: (b, 0, 0)),
                  _layer_block(w_in.shape[1:], j),
                  _layer_block((SC_W, SC_DIM), j),
                  _layer_block((SC_DIM + XDIM, D_MODEL), j),
                  mem, mem,
                  _layer_block((1, D_MODEL), i),
                  _layer_block((1, D_MODEL), i)],
        out_specs=[pl.BlockSpec((1, tm, D_MODEL), lambda b, t: (b, t, 0)),
                   pl.BlockSpec((1, SC_W - 1, SC_DIM), lambda b, t: (b, 0, 0))],
        out_shape=[jax.ShapeDtypeStruct((bsz, t_len, D_MODEL), F32),
                   jax.ShapeDtypeStruct((bsz, SC_W - 1, SC_DIM), F32)],
        scratch_shapes=[pltpu.VMEM((tm + SUBLANES, SC_DIM), F32),
                        pltpu.VMEM((XH * N_MEM, XDIM), BF16),
                        pltpu.VMEM((XH * N_MEM, XDIM), BF16)],
        compiler_params=_cparams(("arbitrary", "arbitrary")),
        name="mix_a_seq",
    )(x, hist, w_in, w_conv, w_out, mem_k, mem_v, ln_g, ln_b)


def _ffn_seq_kernel(x_ref, hist_ref, wup_ref, wc_ref, wdn_ref, g_ref, b_ref,
                    o_ref, hist_o_ref, gbuf, ubuf, *, tm, tf):
    t = pl.program_id(1)
    lo = SUBLANES - (FFN_W - 1)
    nj = D_FF // tf
    halves = ((gbuf, 0), (ubuf, D_FF))

    @pl.when(t == 0)
    def _():
        for buf, off in halves:
            for j in range(nj):
                buf[j, lo:SUBLANES, :] = hist_ref[0, :, off + j * tf:off + (j + 1) * tf]

    x = x_ref[0]
    xb = x.astype(BF16)
    acc = None
    for j in range(nj):
        conv = []
        for buf, off in halves:
            c0 = off + j * tf
            h = _dot(xb, wup_ref[:, c0:c0 + tf])
            buf[j, SUBLANES:SUBLANES + tm, :] = h
            c = wc_ref[FFN_W - 1:FFN_W, c0:c0 + tf] * h
            for w in range(FFN_W - 1):
                c = c + wc_ref[w:w + 1, c0:c0 + tf] * buf[j, lo + w:lo + w + tm, :]
            last = buf[j, tm + lo:tm + SUBLANES, :]
            buf[j, lo:SUBLANES, :] = last
            hist_o_ref[0, :, c0:c0 + tf] = last
            conv.append(c)
        act = (_silu(conv[0]) * conv[1]).astype(BF16)
        d = _dot(act, wdn_ref[j * tf:(j + 1) * tf, :])
        acc = d if acc is None else acc + d
    o_ref[0] = _layer_norm(ALPHA * x + acc, g_ref[...], b_ref[...])


def _ffn_seq(x, hist, w_up, w_conv, w_down, ln_g, ln_b, *, i):
    bsz, t_len, _ = x.shape
    tm = min(SEQ_TILE, t_len)
    tf = FF_TILE
    return pl.pallas_call(
        functools.partial(_ffn_seq_kernel, tm=tm, tf=tf),
        grid=(bsz, t_len // tm),
        in_specs=[pl.BlockSpec((1, tm, D_MODEL), lambda b, t: (b, t, 0)),
                  pl.BlockSpec((1, FFN_W - 1, 2 * D_FF), lambda b, t: (b, 0, 0)),
                  _layer_block((D_MODEL, 2 * D_FF), i),
                  _layer_block((FFN_W, 2 * D_FF), i),
                  _layer_block((D_FF, D_MODEL), i),
                  _layer_block((1, D_MODEL), i),
                  _layer_block((1, D_MODEL), i)],
        out_specs=[pl.BlockSpec((1, tm, D_MODEL), lambda b, t: (b, t, 0)),
                   pl.BlockSpec((1, FFN_W - 1, 2 * D_FF), lambda b, t: (b, 0, 0))],
        out_shape=[jax.ShapeDtypeStruct((bsz, t_len, D_MODEL), F32),
                   jax.ShapeDtypeStruct((bsz, FFN_W - 1, 2 * D_FF), F32)],
        scratch_shapes=[pltpu.VMEM((D_FF // tf, tm + SUBLANES, tf), F32),
                        pltpu.VMEM((D_FF // tf, tm + SUBLANES, tf), F32)],
        compiler_params=_cparams(("arbitrary", "arbitrary")),
        name="ffn_seq",
    )(x, hist, w_up, w_conv, w_down, ln_g, ln_b)


def _gdn_gates(ba, a_row, dt_row):
    lane = lax.broadcasted_iota(jnp.int32, ba.shape, 1)
    return jnp.where(lane < GDN_H, _sigmoid(ba), a_row * _softplus(ba + dt_row))


def _qkv_post(c, q_ref, k_ref, v_ref, idx):
    c = _silu(c)
    for h in range(GDN_H):
        for ref, base in ((q_ref, 0), (k_ref, GDN_DIM)):
            a = c[:, base + h * GDN_DK:base + (h + 1) * GDN_DK]
            a = a * lax.rsqrt(jnp.sum(a * a, -1, keepdims=True) + RMS_EPS)
            ref[idx + (slice(None), slice(h * GDN_DK, (h + 1) * GDN_DK))] = a
    v_ref[idx + (slice(None), slice(None))] = c[:, 2 * GDN_DIM:]


def _proj_b_seq_kernel(x_ref, hist_ref, win_ref, wc_ref, arow_ref, dtrow_ref, k_ref, v_ref,
                       q_o, k_o, v_o, gb_o, zg_o, om_o, hist_o_ref, sbuf, kbd, vbd, *, tm):
    t = pl.program_id(1)
    lo = SUBLANES - (GDN_CONV_W - 1)
    nqkv = 3 * GDN_DIM

    @pl.when(t == 0)
    def _():
        sbuf[lo:SUBLANES, :] = hist_ref[0]
        kbd[...] = _head_blockdiag(k_ref[0])
        vbd[...] = _head_blockdiag(v_ref[0])

    h = _dot(x_ref[0].astype(BF16), win_ref[...])
    qkv = h[:, :nqkv]
    sbuf[SUBLANES:SUBLANES + tm, :] = qkv
    wc = wc_ref[...]
    conv = wc[GDN_CONV_W - 1:GDN_CONV_W] * qkv
    for j in range(GDN_CONV_W - 1):
        conv = conv + wc[j:j + 1] * sbuf[lo + j:lo + j + tm, :]
    last = sbuf[tm + lo:tm + SUBLANES, :]
    sbuf[lo:SUBLANES, :] = last
    hist_o_ref[0] = last

    _qkv_post(conv, q_o, k_o, v_o, (0,))
    zg_o[0] = _silu(h[:, nqkv:nqkv + GDN_DIM])
    qm = h[:, nqkv + GDN_DIM:nqkv + GDN_DIM + XDIM]
    gb_o[0] = _gdn_gates(h[:, nqkv + GDN_DIM + XDIM:], arow_ref[...], dtrow_ref[...])
    om_o[0] = _mem_attn_seq(qm, kbd[...], vbd[...]).astype(BF16)


def _proj_b_seq(x, hist, w_in, w_conv, a_row, dt_row, mem_k, mem_v, *, j, i):
    bsz, t_len, _ = x.shape
    tm = min(SEQ_TILE, t_len)
    tile = lambda n: pl.BlockSpec((1, tm, n), lambda b, t: (b, t, 0))
    f32o = lambda n: jax.ShapeDtypeStruct((bsz, t_len, n), F32)
    mem = pl.BlockSpec((None, 1, N_MEM, XDIM), lambda b, t: (i, b, 0, 0))
    return pl.pallas_call(
        functools.partial(_proj_b_seq_kernel, tm=tm),
        grid=(bsz, t_len // tm),
        in_specs=[tile(D_MODEL),
                  pl.BlockSpec((1, GDN_CONV_W - 1, 3 * GDN_DIM), lambda b, t: (b, 0, 0)),
                  _layer_block((D_MODEL, W_B_COLS), j),
                  _layer_block((GDN_CONV_W, 3 * GDN_DIM), j),
                  _layer_block((1, GB_LANES), j),
                  _layer_block((1, GB_LANES), j),
                  mem, mem],
        out_specs=[tile(GDN_DIM), tile(GDN_DIM), tile(GDN_DIM), tile(GB_LANES), tile(GDN_DIM),
                   tile(XDIM),
                   pl.BlockSpec((1, GDN_CONV_W - 1, 3 * GDN_DIM), lambda b, t: (b, 0, 0))],
        out_shape=[f32o(GDN_DIM), f32o(GDN_DIM), f32o(GDN_DIM), f32o(GB_LANES), f32o(GDN_DIM),
                   jax.ShapeDtypeStruct((bsz, t_len, XDIM), BF16),
                   jax.ShapeDtypeStruct((bsz, GDN_CONV_W - 1, 3 * GDN_DIM), F32)],
        scratch_shapes=[pltpu.VMEM((tm + SUBLANES, 3 * GDN_DIM), F32),
                        pltpu.VMEM((XH * N_MEM, XDIM), BF16),
                        pltpu.VMEM((XH * N_MEM, XDIM), BF16)],
        compiler_params=_cparams(("arbitrary", "arbitrary")),
        name="proj_b_seq",
    )(x, hist, w_in, w_conv, a_row, dt_row, mem_k, mem_v)


def _gdn_scan_kernel(q_ref, k_ref, v_ref, gb_ref, zg_ref, nw_ref, s0_ref, o_ref, s_o_ref, s_scr,
                     *, bb, nc, c):
    t = pl.program_id(1)
    tc = nc * c

    @pl.when(t == 0)
    def _():
        s_scr[...] = s0_ref[0]

    ri = lax.broadcasted_iota(jnp.int32, (c, c), 0)
    ci = lax.broadcasted_iota(jnp.int32, (c, c), 1)
    tril = ri >= ci
    strict = ri > ci
    rt = lax.broadcasted_iota(jnp.int32, (tc, tc), 0)
    ct = lax.broadcasted_iota(jnp.int32, (tc, tc), 1)
    tril_chunks = ((rt >= ct) & ((rt // c) == (ct // c))).astype(F32)
    eye_l = (lax.broadcasted_iota(jnp.int32, (GB_LANES, GB_LANES), 0)
             == lax.broadcasted_iota(jnp.int32, (GB_LANES, GB_LANES), 1)).astype(F32)
    nw = nw_ref[...]
    n_sq = max(c.bit_length() - 2, 0)

    gcs = []
    for i in range(bb):
        gc_all = _dot_hi(tril_chunks, gb_ref[i])
        gcs.append((gc_all, _dot_nt_hi(eye_l, gc_all)))

    probs = [(i, ic, h) for i in range(bb) for ic in range(nc) for h in range(GDN_H)]
    st = {}
    for p in probs:
        i, ic, h = p
        rows = slice(ic * c, (ic + 1) * c)
        hs = slice(h * GDN_DK, (h + 1) * GDN_DK)
        gc_all, gc_t = gcs[i]
        q = q_ref[i, rows, hs] * (GDN_DK ** -0.5)
        k = k_ref[i, rows, hs]
        gcol = gc_all[rows, GDN_H + h:GDN_H + h + 1]
        grow = gc_t[GDN_H + h:GDN_H + h + 1, rows]
        glast = gc_all[(ic + 1) * c - 1:(ic + 1) * c, GDN_H + h:GDN_H + h + 1]
        egc = jnp.exp(gcol)
        kb = k * gb_ref[i, rows, h:h + 1]
        st[p] = dict(
            decay=jnp.exp(jnp.where(tril, gcol - grow, -jnp.inf)),
            aq=_dot_nt(jnp.concatenate([kb, q], axis=0).astype(BF16), k.astype(BF16)),
            sol=jnp.concatenate([v_ref[i, rows, hs] * gb_ref[i, rows, h:h + 1], kb * egc], axis=-1),
            qd=(q * egc).astype(BF16),
            k_dec=(k * jnp.exp(glast - gcol)).astype(BF16),
            e_last=jnp.exp(glast))
    for p in probs:
        d = st[p]
        d["pb"] = jnp.where(strict, d["aq"][:c] * d["decay"], 0.0).astype(BF16)
        d["qk"] = (d["aq"][c:] * d["decay"]).astype(BF16)
        d["sol"] = d["sol"] - _dot_split_rhs(d["pb"], d["sol"])
    for _ in range(n_sq):
        for p in probs:
            st[p]["pb"] = _dot(st[p]["pb"], st[p]["pb"]).astype(BF16)
        for p in probs:
            st[p]["sol"] = st[p]["sol"] + _dot_split_rhs(st[p]["pb"], st[p]["sol"])
    for p in probs:
        d = st[p]
        d["wq"] = jnp.concatenate([d["sol"][:, GDN_DV:].astype(BF16), d["qd"]], axis=0)
        d["u"] = d["sol"][:, :GDN_DV]

    for ic in range(nc):
        rows = slice(ic * c, (ic + 1) * c)
        seqs = [(i, h) for i in range(bb) for h in range(GDN_H)]
        s_old = {ih: s_scr[ih[0], ih[1]] for ih in seqs}
        ws = {ih: _dot(st[ih[0], ic, ih[1]]["wq"], s_old[ih].astype(BF16)) for ih in seqs}
        vb = {ih: (st[ih[0], ic, ih[1]]["u"] - ws[ih][:c]).astype(BF16) for ih in seqs}
        for ih in seqs:
            d = st[ih[0], ic, ih[1]]
            s_scr[ih[0], ih[1]] = s_old[ih] * d["e_last"] + _dot_tn(d["k_dec"], vb[ih])
        for ih in seqs:
            i, h = ih
            hs = slice(h * GDN_DK, (h + 1) * GDN_DK)
            o = ws[ih][c:] + _dot(st[i, ic, h]["qk"], vb[ih])
            o = o * lax.rsqrt(jnp.mean(o * o, -1, keepdims=True) + RMS_EPS)
            o_ref[i, rows, hs] = (o * nw * zg_ref[i, rows, hs]).astype(o_ref.dtype)
    s_o_ref[...] = s_scr[...]


def _gdn_scan(q, k, v, gb, zg, norm_w, s0_all, *, j, layer, c, bb, nc):
    bsz, t_len, _ = q.shape
    tc = nc * c
    tile = lambda n: pl.BlockSpec((bb, tc, n), lambda b, t: (b, t, 0))
    st = pl.BlockSpec((bb, GDN_H, GDN_DK, GDN_DV), lambda b, t: (b, 0, 0, 0))
    st_in = pl.BlockSpec((1, bb, GDN_H, GDN_DK, GDN_DV), lambda b, t: (layer, b, 0, 0, 0))
    return pl.pallas_call(
        functools.partial(_gdn_scan_kernel, bb=bb, nc=nc, c=c),
        grid=(bsz // bb, t_len // tc),
        in_specs=[tile(GDN_DIM), tile(GDN_DIM), tile(GDN_DIM), tile(GB_LANES), tile(GDN_DIM),
                  _layer_block((1, GDN_DV), j), st_in],
        out_specs=[tile(GDN_DIM), st],
        out_shape=[jax.ShapeDtypeStruct((bsz, t_len, GDN_DIM), BF16),
                   jax.ShapeDtypeStruct((bsz, GDN_H, GDN_DK, GDN_DV), F32)],
        scratch_shapes=[pltpu.VMEM((bb, GDN_H, GDN_DK, GDN_DV), F32)],
        compiler_params=_cparams(("arbitrary", "arbitrary")),
        name="gdn_scan",
    )(q, k, v, gb, zg, norm_w, s0_all)


def _out_ln_kernel(a1_ref, a2_ref, w_ref, x_ref, g_ref, b_ref, o_ref):
    n1 = a1_ref.shape[-1]
    mix = (_dot(a1_ref[...].astype(BF16), w_ref[:n1, :])
           + _dot(a2_ref[...].astype(BF16), w_ref[n1:, :]))
    o_ref[...] = _layer_norm(ALPHA * x_ref[...] + mix, g_ref[...], b_ref[...])


def _out_ln(a1, a2, w, x, ln_g, ln_b, *, j, i):
    rows = x.shape[0]
    tr = min(ROW_TILE, rows)
    n1, n2 = a1.shape[1], a2.shape[1]
    tile = lambda n: pl.BlockSpec((tr, n), lambda r: (r, 0))
    return pl.pallas_call(
        _out_ln_kernel,
        grid=(rows // tr,),
        in_specs=[tile(n1), tile(n2), _layer_block((n1 + n2, D_MODEL), j), tile(D_MODEL),
                  _layer_block((1, D_MODEL), i), _layer_block((1, D_MODEL), i)],
        out_specs=tile(D_MODEL),
        out_shape=jax.ShapeDtypeStruct((rows, D_MODEL), F32),
        compiler_params=_cparams(("arbitrary",)),
        name="out_ln",
    )(a1, a2, w, x, ln_g, ln_b)


def _mix_a_front_sample_kernel(x_ref, hist_ref, win_ref, wc_ref, y_o, qm_o, hist_o, *, t_len, bsz):
    h = _dot(x_ref[...].astype(BF16), win_ref[...])
    u = h[:, 2 * SC_DIM:3 * SC_DIM] * h[:, :SC_DIM]
    slabs = [hist_ref[:, j * SC_DIM:(j + 1) * SC_DIM] for j in range(SC_W - 1)]
    slabs += [u[t * bsz:(t + 1) * bsz] for t in range(t_len)]
    wc = wc_ref[...]
    for t in range(t_len):
        conv = wc[0:1] * slabs[t]
        for j in range(1, SC_W):
            conv = conv + wc[j:j + 1] * slabs[t + j]
        y_o[t * bsz:(t + 1) * bsz, :] = (h[t * bsz:(t + 1) * bsz, SC_DIM:2 * SC_DIM] * conv).astype(BF16)
    for j in range(SC_W - 1):
        hist_o[j] = slabs[t_len + j]
    qm_o[...] = h[:, 3 * SC_DIM:]


def _whole(shape):
    nd = len(shape)
    return pl.BlockSpec(tuple(shape), lambda *_: (0,) * nd)


def _mix_a_front_sample(x2d, hist, w_in, w_conv, *, j, t_len, bsz):
    rows = x2d.shape[0]
    out_shape = [jax.ShapeDtypeStruct((rows, SC_DIM), BF16),
                 jax.ShapeDtypeStruct((rows, XDIM), F32),
                 jax.ShapeDtypeStruct((SC_W - 1, bsz, SC_DIM), F32)]
    return pl.pallas_call(
        functools.partial(_mix_a_front_sample_kernel, t_len=t_len, bsz=bsz),
        grid=(1,),
        in_specs=[_whole(x2d.shape), _layer_block(hist.shape[1:], j), _layer_block(w_in.shape[1:], j),
                  _layer_block(w_conv.shape[1:], j)],
        out_specs=[_whole(s.shape) for s in out_shape],
        out_shape=out_shape,
        compiler_params=_cparams(("arbitrary",)),
        name="mix_a_front_sample",
    )(x2d, hist, w_in, w_conv)


def _proj_b_front_sample_kernel(x_ref, hist_ref, win_ref, wc_ref, arow_ref, dtrow_ref,
                                q_o, k_o, v_o, gb_o, zg_o, qm_o, hist_o, *, t_len, bsz):
    nqkv = 3 * GDN_DIM
    h = _dot(x_ref[...].astype(BF16), win_ref[...])
    slabs = [hist_ref[:, j * nqkv:(j + 1) * nqkv] for j in range(GDN_CONV_W - 1)]
    slabs += [h[t * bsz:(t + 1) * bsz, :nqkv] for t in range(t_len)]
    wc = wc_ref[...]
    for t in range(t_len):
        conv = wc[0:1] * slabs[t]
        for j in range(1, GDN_CONV_W):
            conv = conv + wc[j:j + 1] * slabs[t + j]
        _qkv_post(conv, q_o, k_o, v_o, (t,))
    for j in range(GDN_CONV_W - 1):
        hist_o[j] = slabs[t_len + j]
    zg_o[...] = _silu(h[:, nqkv:nqkv + GDN_DIM])
    qm_o[...] = h[:, nqkv + GDN_DIM:nqkv + GDN_DIM + XDIM]
    gb_o[...] = _gdn_gates(h[:, nqkv + GDN_DIM + XDIM:], arow_ref[...], dtrow_ref[...])


def _proj_b_front_sample(x2d, hist, w_in, w_conv, a_row, dt_row, *, j, t_len, bsz):
    rows = x2d.shape[0]
    tmaj = jax.ShapeDtypeStruct((t_len, bsz, GDN_DIM), F32)
    out_shape = [tmaj, tmaj, tmaj,
                 jax.ShapeDtypeStruct((rows, GB_LANES), F32),
                 jax.ShapeDtypeStruct((rows, GDN_DIM), F32),
                 jax.ShapeDtypeStruct((rows, XDIM), F32),
                 jax.ShapeDtypeStruct((GDN_CONV_W - 1, bsz, 3 * GDN_DIM), F32)]
    return pl.pallas_call(
        functools.partial(_proj_b_front_sample_kernel, t_len=t_len, bsz=bsz),
        grid=(1,),
        in_specs=[_whole(x2d.shape), _layer_block(hist.shape[1:], j), _layer_block(w_in.shape[1:], j),
                  _layer_block(w_conv.shape[1:], j), _layer_block((1, GB_LANES), j),
                  _layer_block((1, GB_LANES), j)],
        out_specs=[_whole(s.shape) for s in out_shape],
        out_shape=out_shape,
        compiler_params=_cparams(("arbitrary",)),
        name="proj_b_front_sample",
    )(x2d, hist, w_in, w_conv, a_row, dt_row)


def _ffn_sample_kernel(x_ref, hg0, hg1, hu0, hu1, wg_ref, wu_ref, cg_ref, cu_ref, wdn_ref,
                       g_ref, b_ref, o_ref, hgo, huo, acc, *, t_len, bsz):
    j = pl.program_id(0)

    @pl.when(j == 0)
    def _():
        acc[...] = jnp.zeros_like(acc)

    xb = x_ref[...].astype(BF16)
    conv = []
    for w_ref, c_ref, hists, h_o in ((wg_ref, cg_ref, (hg0, hg1), hgo), (wu_ref, cu_ref, (hu0, hu1), huo)):
        h = _dot(xb, w_ref[...])
        slabs = [hists[0][...], hists[1][...]] + [h[t * bsz:(t + 1) * bsz] for t in range(t_len)]
        wc = c_ref[...]
        outs = []
        for t in range(t_len):
            c = wc[0:1] * slabs[t]
            for w in range(1, FFN_W):
                c = c + wc[w:w + 1] * slabs[t + w]
            outs.append(c)
        for w in range(FFN_W - 1):
            h_o[w] = slabs[t_len + w]
        conv.append(jnp.concatenate(outs, axis=0))
    act = (_silu(conv[0]) * conv[1]).astype(BF16)
    acc[...] += _dot(act, wdn_ref[...])

    @pl.when(j == pl.num_programs(0) - 1)
    def _():
        o_ref[...] = _layer_norm(ALPHA * x_ref[...] + acc[...], g_ref[...], b_ref[...])


def _ffn_sample(x2d, hist, w_up, w_conv, w_down, ln_g, ln_b, *, i, t_len, bsz):
    rows = x2d.shape[0]
    tf = FF_TILE
    nj = D_FF // tf
    hspec = lambda r, off: pl.BlockSpec((None, bsz, tf),
                                        lambda j: (i, 0, r * (2 * D_FF // tf) + off + j))
    hout = jax.ShapeDtypeStruct((FFN_W - 1, bsz, D_FF), F32)
    return pl.pallas_call(
        functools.partial(_ffn_sample_kernel, t_len=t_len, bsz=bsz),
        grid=(nj,),
        in_specs=[_resident((rows, D_MODEL)),
                  hspec(0, 0), hspec(1, 0), hspec(0, nj), hspec(1, nj),
                  pl.BlockSpec((None, D_MODEL, tf), lambda j: (i, 0, j)),
                  pl.BlockSpec((None, D_MODEL, tf), lambda j: (i, 0, nj + j)),
                  pl.BlockSpec((None, FFN_W, tf), lambda j: (i, 0, j)),
                  pl.BlockSpec((None, FFN_W, tf), lambda j: (i, 0, nj + j)),
                  pl.BlockSpec((None, tf, D_MODEL), lambda j: (i, j, 0)),
                  _layer_block((1, D_MODEL), i), _layer_block((1, D_MODEL), i)],
        out_specs=[pl.BlockSpec((rows, D_MODEL), lambda j: (0, 0)),
                   pl.BlockSpec((FFN_W - 1, bsz, tf), lambda j: (0, 0, j)),
                   pl.BlockSpec((FFN_W - 1, bsz, tf), lambda j: (0, 0, j))],
        out_shape=[jax.ShapeDtypeStruct((rows, D_MODEL), F32), hout, hout],
        scratch_shapes=[pltpu.VMEM((rows, D_MODEL), F32)],
        compiler_params=_cparams(("arbitrary",)),
        name="ffn_sample",
    )(x2d, hist, hist, hist, hist, w_up, w_up, w_conv, w_conv, w_down, ln_g, ln_b)


def _prep_weights(w_in_a, w_out_a, w_in_b, a_log, dt_bias, w_out_b, w_mem_kv, w_up, w_down):
    nqkvz = 4 * GDN_DIM
    w_b = jnp.concatenate(
        [w_in_b[..., :nqkvz], w_in_b[..., nqkvz + 2 * GDN_H:],
         jnp.pad(w_in_b[..., nqkvz:nqkvz + 2 * GDN_H], ((0, 0), (0, 0), (0, GB_LANES - 2 * GDN_H)))],
        axis=-1).astype(BF16)
    pad = GB_LANES - 2 * GDN_H
    a_row = jnp.pad(-jnp.exp(a_log.astype(F32)), ((0, 0), (GDN_H, pad)))[:, None, :]
    dt_row = jnp.pad(dt_bias.astype(F32), ((0, 0), (GDN_H, pad)))[:, None, :]
    return dict(w_in_a=w_in_a.astype(BF16), w_out_a=w_out_a.astype(BF16), w_in_b=w_b,
                w_out_b=w_out_b.astype(BF16), w_mem_kv=w_mem_kv.astype(BF16),
                w_up=w_up.astype(BF16), w_down=w_down.astype(BF16), a_row=a_row, dt_row=dt_row)


def _trunk_prompt(x, mem_k, mem_v, wts, conv_a, conv_b, gdn_norm_w, ln1_g, ln1_b, ln2_g, ln2_b,
                  w_conv_ffn):
    bsz, t_len, _ = x.shape
    z = lambda *s: jnp.zeros(s, F32)
    new_sc, new_gc, new_gs, new_ffn = [], [], [], []
    for i in range(DEPTH):
        j = i // 2
        if i % 2 == 0:
            x, hs = _mix_a_seq(x, z(bsz, SC_W - 1, SC_DIM), wts["w_in_a"], conv_a, wts["w_out_a"],
                               mem_k, mem_v, ln1_g, ln1_b, j=j, i=i)
            new_sc.append(hs)
        else:
            q, k, v, gb, zg, om, hg = _proj_b_seq(
                x, z(bsz, GDN_CONV_W - 1, 3 * GDN_DIM), wts["w_in_b"], conv_b,
                wts["a_row"], wts["dt_row"], mem_k, mem_v, j=j, i=i)
            o, sg = _gdn_scan(q, k, v, gb, zg, gdn_norm_w, z(1, bsz, GDN_H, GDN_DK, GDN_DV),
                              j=j, layer=0, c=GDN_CHUNK, bb=SCAN_SEQS, nc=SCAN_CHUNKS)
            rows = bsz * t_len
            x = _out_ln(o.reshape(rows, GDN_DIM), om.reshape(rows, XDIM), wts["w_out_b"],
                        x.reshape(rows, D_MODEL), ln1_g, ln1_b, j=j, i=i).reshape(bsz, t_len, D_MODEL)
            new_gc.append(hg)
            new_gs.append(sg)
        x, hf = _ffn_seq(x, z(bsz, FFN_W - 1, 2 * D_FF), wts["w_up"], w_conv_ffn, wts["w_down"],
                         ln2_g, ln2_b, i=i)
        new_ffn.append(hf)
    return x, jnp.stack(new_sc), jnp.stack(new_gc), jnp.stack(new_gs), jnp.stack(new_ffn)


def _trunk_sample(x, mem_k, mem_v, sc_hist, gdn_hist, gdn_s, ffn_hist, wts, conv_a, conv_b,
                  gdn_norm_w, ln1_g, ln1_b, ln2_g, ln2_b, w_conv_ffn):
    bsz, t_len, _ = x.shape
    rows = bsz * t_len
    t_pad = SUBLANES
    x2 = jnp.transpose(x, (1, 0, 2)).reshape(rows, D_MODEL)
    to_bm = lambda a: jnp.transpose(a, (1, 0, 2))
    new_sc, new_gc, new_gs, new_ffn = [], [], [], []
    mem_kt = jnp.transpose(mem_k, (0, 1, 3, 4, 2))
    mem_vt = jnp.transpose(mem_v, (0, 1, 3, 4, 2))
    sc_hist = sc_hist.reshape(sc_hist.shape[0], bsz, -1)
    gdn_hist = gdn_hist.reshape(gdn_hist.shape[0], bsz, -1)
    ffn_hist = ffn_hist.reshape(ffn_hist.shape[0], bsz, -1)
    for i in range(DEPTH):
        j = i // 2
        if i % 2 == 0:
            y, qm, hs = _mix_a_front_sample(x2, sc_hist, wts["w_in_a"], conv_a, j=j,
                                            t_len=t_len, bsz=bsz)
            om = _attn_sample(qm.reshape(t_len, bsz, XDIM), mem_kt, mem_vt, i).reshape(rows, XDIM)
            x2 = _out_ln(y, om, wts["w_out_a"], x2, ln1_g, ln1_b, j=j, i=i)
            new_sc.append(to_bm(hs))
        else:
            q, k, v, gb, zg, qm, hg = _proj_b_front_sample(
                x2, gdn_hist, wts["w_in_b"], conv_b, wts["a_row"], wts["dt_row"], j=j,
                t_len=t_len, bsz=bsz)
            om = _attn_sample(qm.reshape(t_len, bsz, XDIM), mem_kt, mem_vt, i).reshape(rows, XDIM)
            padt = lambda a: jnp.pad(to_bm(a.reshape(t_len, bsz, -1)), ((0, 0), (0, t_pad - t_len), (0, 0)))
            o, sg = _gdn_scan(padt(q), padt(k), padt(v), padt(gb), padt(zg), gdn_norm_w, gdn_s,
                              j=j, layer=j, c=t_pad, bb=SCAN_SEQS_SAMPLE, nc=1)
            o = jnp.transpose(o[:, :t_len], (1, 0, 2)).reshape(rows, GDN_DIM)
            x2 = _out_ln(o, om, wts["w_out_b"], x2, ln1_g, ln1_b, j=j, i=i)
            new_gc.append(to_bm(hg))
            new_gs.append(sg)
        x2, hgo, huo = _ffn_sample(x2, ffn_hist, wts["w_up"], w_conv_ffn, wts["w_down"],
                                   ln2_g, ln2_b, i=i, t_len=t_len, bsz=bsz)
        new_ffn.append(to_bm(jnp.concatenate([hgo, huo], axis=-1)))
    y = jnp.transpose(x2.reshape(t_len, bsz, D_MODEL), (1, 0, 2))
    return y, jnp.stack(new_sc), jnp.stack(new_gc), jnp.stack(new_gs), jnp.stack(new_ffn)


def kernel(x_prompt, x_sample, mem_prompt, cache_mem_k, cache_mem_v, state_shortconv, state_gdn_conv,
           state_gdn, state_ffn_conv, w_in_a, conv_a, w_out_a, w_in_b, conv_b, a_log, dt_bias,
           gdn_norm_w, w_out_b, w_mem_kv, ln1_g, ln1_b, ln2_g, ln2_b, w_up, w_conv_ffn, w_down):
    wts = _prep_weights(w_in_a, w_out_a, w_in_b, a_log, dt_bias, w_out_b, w_mem_kv, w_up, w_down)
    row3 = lambda a: a.reshape(a.shape[0], 1, a.shape[1])
    shared = (wts, conv_a, conv_b, row3(gdn_norm_w), row3(ln1_g), row3(ln1_b), row3(ln2_g),
              row3(ln2_b), w_conv_ffn)
    bsz = x_prompt.shape[0]
    k2, v2 = _mem_kv(mem_prompt.reshape(bsz * N_MEM, D_MODEL), wts["w_mem_kv"])
    mem_k_prompt = k2.reshape(DEPTH, bsz, N_MEM, XDIM)
    mem_v_prompt = v2.reshape(DEPTH, bsz, N_MEM, XDIM)
    y_prompt, sc_p, gc_p, gs_p, ffn_p = _trunk_prompt(x_prompt, mem_k_prompt, mem_v_prompt, *shared)
    y_sample, sc_s, gc_s, gs_s, ffn_s = _trunk_sample(
        x_sample, cache_mem_k, cache_mem_v, state_shortconv, state_gdn_conv, state_gdn,
        state_ffn_conv, *shared)
    shape5 = (DEPTH, bsz, N_MEM, XH, XD)
    return (y_prompt, y_sample, mem_k_prompt.reshape(shape5), mem_v_prompt.reshape(shape5),
            sc_p, gc_p, gs_p, ffn_p, sc_s, gc_s, gs_s, ffn_s)
```

```python
import functools

import jax
import jax.numpy as jnp
from jax import lax
from jax.experimental import pallas as pl
from jax.experimental.pallas import tpu as pltpu

F32 = jnp.float32
BF16 = jnp.bfloat16

DEPTH = 4
D_MODEL = 1024
SC_DIM = 768
SC_W = 3
GDN_H = 6
GDN_DK = 128
GDN_DV = 128
GDN_DIM = GDN_H * GDN_DK
GDN_CONV_W = 4
GDN_CHUNK = 64
N_MEM = 256
XH = 4
XD = 64
XDIM = XH * XD
D_FF = 2816
FFN_W = 3
ALPHA = (2.0 * DEPTH) ** 0.25
LN_EPS = 1e-5
RMS_EPS = 1e-6

V7X_VMEM_BYTES = 64 * 1024 * 1024
VMEM_LIMIT = V7X_VMEM_BYTES - 8 * 1024 * 1024
SUBLANES = 8
LANES = 128

GB_LANES = LANES

SEQ_TILE = 512
SCAN_SEQS = 2
SCAN_CHUNKS = 2
SCAN_SEQS_SAMPLE = 4
ROW_TILE = 512
FF_TILE = 2816
FF_TILE_SAMPLE = 1408
ATTN_BATCH_BLOCK = 16
ATTN_UNROLL = 4
W_B_COLS = 3 * GDN_DIM + GDN_DIM + XDIM + GB_LANES


def _cparams(sem):
    return pltpu.CompilerParams(dimension_semantics=sem, vmem_limit_bytes=VMEM_LIMIT)


def _resident(shape):
    nd = len(shape)
    return pl.BlockSpec(shape, lambda *_: (0,) * nd, pipeline_mode=pl.Buffered(1))


def _layer_block(shape, layer):
    nd = len(shape)
    return pl.BlockSpec((None,) + tuple(shape), lambda *_: (layer,) + (0,) * nd,
                        pipeline_mode=pl.Buffered(1))


def _silu(x):
    return x * (1.0 / (1.0 + jnp.exp(-x)))


def _sigmoid(x):
    return 1.0 / (1.0 + jnp.exp(-x))


def _softplus(x):
    return jnp.maximum(x, 0.0) + jnp.log(1.0 + jnp.exp(-jnp.abs(x)))


def _layer_norm(v, g, b):
    mu = jnp.mean(v, -1, keepdims=True)
    d = v - mu
    var = jnp.mean(d * d, -1, keepdims=True)
    return d * lax.rsqrt(var + LN_EPS) * g + b


def _dot(a, b):
    return jnp.dot(a, b, preferred_element_type=F32)


def _dot_nt(a, b):
    return lax.dot_general(a, b, (((1,), (1,)), ((), ())), preferred_element_type=F32)


def _dot_tn(a, b):
    return lax.dot_general(a, b, (((0,), (0,)), ((), ())), preferred_element_type=F32)


def _dot_split_rhs(a, x):
    hi = x.astype(BF16)
    lo = (x - hi.astype(F32)).astype(BF16)
    return _dot(a, hi) + _dot(a, lo)


def _dot_hi(a, b):
    return jnp.dot(a, b, preferred_element_type=F32, precision=lax.Precision.HIGHEST)


def _dot_nt_hi(a, b):
    return lax.dot_general(a, b, (((1,), (1,)), ((), ())), preferred_element_type=F32,
                           precision=lax.Precision.HIGHEST)


def _kv_kernel(m_ref, w_ref, k_ref, v_ref):
    kv = _dot(m_ref[...].astype(BF16), w_ref[0])
    k_ref[0] = kv[:, :XDIM]
    v_ref[0] = kv[:, XDIM:]


def _mem_kv(mem2d, w_kv):
    rows = mem2d.shape[0]
    out = jax.ShapeDtypeStruct((DEPTH, rows, XDIM), F32)
    return pl.pallas_call(
        _kv_kernel,
        grid=(DEPTH,),
        in_specs=[_resident((rows, D_MODEL)),
                  pl.BlockSpec((1, D_MODEL, 2 * XDIM), lambda l: (l, 0, 0))],
        out_specs=[pl.BlockSpec((1, rows, XDIM), lambda l: (l, 0, 0))] * 2,
        out_shape=[out, out],
        compiler_params=_cparams(("arbitrary",)),
        name="mem_kv",
    )(mem2d, w_kv)


def _head_blockdiag(kv):
    lane_head = lax.broadcasted_iota(jnp.int32, kv.shape, 1) // XD
    return jnp.concatenate(
        [jnp.where(lane_head == h, kv, 0.0).astype(BF16) for h in range(XH)], axis=0)


def _softmax_rows(s):
    m = jnp.max(s, -1, keepdims=True)
    e = jnp.exp(s - m)
    return e / jnp.sum(e, -1, keepdims=True)


def _mem_attn_seq(qm, kbd, vbd):
    s = _dot_nt(qm.astype(BF16), kbd) * (XD ** -0.5)
    p = jnp.concatenate(
        [_softmax_rows(s[:, h * N_MEM:(h + 1) * N_MEM]).astype(BF16) for h in range(XH)], axis=-1)
    return _dot(p, vbd)


def _attn_sample_kernel(q_ref, k_ref, v_ref, o_ref, *, bb, t_len):
    def body(g, carry):
        elems = [g * ATTN_UNROLL + e for e in range(ATTN_UNROLL)]
        q8 = []
        for i in elems:
            rows = [q_ref[t, pl.ds(i, 1), :] for t in range(t_len)]
            rows.append(jnp.zeros((SUBLANES - t_len, XDIM), F32))
            q8.append(jnp.concatenate(rows, axis=0).astype(BF16))
        s = [[_dot(q8[e][:, h * XD:(h + 1) * XD], k_ref[0, i, h].astype(BF16)) * (XD ** -0.5)
              for h in range(XH)] for e, i in enumerate(elems)]
        p = [[_softmax_rows(sh).astype(BF16) for sh in se] for se in s]
        for e, i in enumerate(elems):
            o8 = jnp.concatenate([_dot_nt(p[e][h], v_ref[0, i, h].astype(BF16)) for h in range(XH)],
                                 axis=-1)
            for t in range(t_len):
                o_ref[t, pl.ds(i, 1), :] = o8[t:t + 1]
        return carry

    lax.fori_loop(0, bb // ATTN_UNROLL, body, 0)


def _attn_sample(qm_tm, mem_kt, mem_vt, layer):
    t_len, bsz, _ = qm_tm.shape
    bb = min(ATTN_BATCH_BLOCK, bsz)
    kv = pl.BlockSpec((1, bb, XH, XD, N_MEM), lambda i: (layer, i, 0, 0, 0))
    return pl.pallas_call(
        functools.partial(_attn_sample_kernel, bb=bb, t_len=t_len),
        grid=(bsz // bb,),
        in_specs=[pl.BlockSpec((t_len, bb, XDIM), lambda i: (0, i, 0)), kv, kv],
        out_specs=pl.BlockSpec((t_len, bb, XDIM), lambda i: (0, i, 0)),
        out_shape=jax.ShapeDtypeStruct((t_len, bsz, XDIM), F32),
        compiler_params=_cparams(("arbitrary",)),
        name="attn_sample",
    )(qm_tm, mem_kt, mem_vt)


def _mix_a_seq_kernel(x_ref, hist_ref, win_ref, wc_ref, wout_ref, k_ref, v_ref, g_ref, b_ref,
                      o_ref, hist_o_ref, ubuf, kbd, vbd, *, tm):
    t = pl.program_id(1)
    lo = SUBLANES - (SC_W - 1)

    @pl.when(t == 0)
    def _():
        ubuf[lo:SUBLANES, :] = hist_ref[0]
        kbd[...] = _head_blockdiag(k_ref[0])
        vbd[...] = _head_blockdiag(v_ref[0])

    x = x_ref[0]
    h = _dot(x.astype(BF16), win_ref[...])
    xin = h[:, :SC_DIM]
    bg = h[:, SC_DIM:2 * SC_DIM]
    cg = h[:, 2 * SC_DIM:3 * SC_DIM]
    qm = h[:, 3 * SC_DIM:]
    u = cg * xin
    ubuf[SUBLANES:SUBLANES + tm, :] = u
    wc = wc_ref[...]
    conv = wc[SC_W - 1:SC_W] * u
    for j in range(SC_W - 1):
        conv = conv + wc[j:j + 1] * ubuf[lo + j:lo + j + tm, :]
    y = bg * conv
    last = ubuf[tm + lo:tm + SUBLANES, :]
    ubuf[lo:SUBLANES, :] = last
    hist_o_ref[0] = last

    o_mem = _mem_attn_seq(qm, kbd[...], vbd[...])
    mix = _dot(y.astype(BF16), wout_ref[:SC_DIM, :]) + _dot(o_mem.astype(BF16), wout_ref[SC_DIM:, :])
    o_ref[0] = _layer_norm(ALPHA * x + mix, g_ref[...], b_ref[...])


def _mix_a_seq(x, hist, w_in, w_conv, w_out, mem_k, mem_v, ln_g, ln_b, *, j, i):
    bsz, t_len, _ = x.shape
    tm = min(SEQ_TILE, t_len)
    mem = pl.BlockSpec((None, 1, N_MEM, XDIM), lambda b, t: (i, b, 0, 0))
    return pl.pallas_call(
        functools.partial(_mix_a_seq_kernel, tm=tm),
        grid=(bsz, t_len // tm),
        in_specs=[pl.BlockSpec((1, tm, D_MODEL), lambda b, t: (b, t, 0)),
                  pl.BlockSpec((1, SC_W - 1, SC_DIM), lambda b, t: (b, 0, 0)),
                  _layer_block(w_in.shape[1:], j),
                  _layer_block((SC_W, SC_DIM), j),
                  _layer_block((SC_DIM + XDIM, D_MODEL), j),
                  mem, mem,
                  _layer_block((1, D_MODEL), i),
                  _layer_block((1, D_MODEL), i)],
        out_specs=[pl.BlockSpec((1, tm, D_MODEL), lambda b, t: (b, t, 0)),
                   pl.BlockSpec((1, SC_W - 1, SC_DIM), lambda b, t: (b, 0, 0))],
        out_shape=[jax.ShapeDtypeStruct((bsz, t_len, D_MODEL), F32),
                   jax.ShapeDtypeStruct((bsz, SC_W - 1, SC_DIM), F32)],
        scratch_shapes=[pltpu.VMEM((tm + SUBLANES, SC_DIM), F32),
                        pltpu.VMEM((XH * N_MEM, XDIM), BF16),
                        pltpu.VMEM((XH * N_MEM, XDIM), BF16)],
        compiler_params=_cparams(("arbitrary", "arbitrary")),
        name="mix_a_seq",
    )(x, hist, w_in, w_conv, w_out, mem_k, mem_v, ln_g, ln_b)


def _ffn_seq_kernel(x_ref, hist_ref, wup_ref, wc_ref, wdn_ref, g_ref, b_ref,
                    o_ref, hist_o_ref, gbuf, ubuf, *, tm, tf):
    t = pl.program_id(1)
    lo = SUBLANES - (FFN_W - 1)
    nj = D_FF // tf
    halves = ((gbuf, 0), (ubuf, D_FF))

    @pl.when(t == 0)
    def _():
        for buf, off in halves:
            for j in range(nj):
                buf[j, lo:SUBLANES, :] = hist_ref[0, :, off + j * tf:off + (j + 1) * tf]

    x = x_ref[0]
    xb = x.astype(BF16)

    def up_project(j):
        return [_dot(xb, wup_ref[:, off + j * tf:off + (j + 1) * tf]) for _, off in halves]

    def conv_act(j, hs):
        conv = []
        for (buf, off), h in zip(halves, hs):
            c0 = off + j * tf
            buf[j, SUBLANES:SUBLANES + tm, :] = h
            c = wc_ref[FFN_W - 1:FFN_W, c0:c0 + tf] * h
            for w in range(FFN_W - 1):
                c = c + wc_ref[w:w + 1, c0:c0 + tf] * buf[j, lo + w:lo + w + tm, :]
            last = buf[j, tm + lo:tm + SUBLANES, :]
            buf[j, lo:SUBLANES, :] = last
            hist_o_ref[0, :, c0:c0 + tf] = last
            conv.append(c)
        return (_silu(conv[0]) * conv[1]).astype(BF16)

    acc = None
    hs = up_project(0)
    for j in range(nj):
        hs_next = up_project(j + 1) if j + 1 < nj else None
        d = _dot(conv_act(j, hs), wdn_ref[j * tf:(j + 1) * tf, :])
        acc = d if acc is None else acc + d
        hs = hs_next
    o_ref[0] = _layer_norm(ALPHA * x + acc, g_ref[...], b_ref[...])


def _ffn_seq(x, hist, w_up, w_conv, w_down, ln_g, ln_b, *, i):
    bsz, t_len, _ = x.shape
    tm = min(SEQ_TILE, t_len)
    tf = FF_TILE
    return pl.pallas_call(
        functools.partial(_ffn_seq_kernel, tm=tm, tf=tf),
        grid=(bsz, t_len // tm),
        in_specs=[pl.BlockSpec((1, tm, D_MODEL), lambda b, t: (b, t, 0)),
                  pl.BlockSpec((1, FFN_W - 1, 2 * D_FF), lambda b, t: (b, 0, 0)),
                  _layer_block((D_MODEL, 2 * D_FF), i),
                  _layer_block((FFN_W, 2 * D_FF), i),
                  _layer_block((D_FF, D_MODEL), i),
                  _layer_block((1, D_MODEL), i),
                  _layer_block((1, D_MODEL), i)],
        out_specs=[pl.BlockSpec((1, tm, D_MODEL), lambda b, t: (b, t, 0)),
                   pl.BlockSpec((1, FFN_W - 1, 2 * D_FF), lambda b, t: (b, 0, 0))],
        out_shape=[jax.ShapeDtypeStruct((bsz, t_len, D_MODEL), F32),
                   jax.ShapeDtypeStruct((bsz, FFN_W - 1, 2 * D_FF), F32)],
        scratch_shapes=[pltpu.VMEM((D_FF // tf, tm + SUBLANES, tf), F32),
                        pltpu.VMEM((D_FF // tf, tm + SUBLANES, tf), F32)],
        compiler_params=_cparams(("arbitrary", "arbitrary")),
        name="ffn_seq",
    )(x, hist, w_up, w_conv, w_down, ln_g, ln_b)


def _gdn_gates(ba, a_row, dt_row):
    lane = lax.broadcasted_iota(jnp.int32, ba.shape, 1)
    return jnp.where(lane < GDN_H, _sigmoid(ba), a_row * _softplus(ba + dt_row))


def _qkv_post(c, q_ref, k_ref, v_ref, idx):
    c = _silu(c)
    for h in range(GDN_H):
        for ref, base in ((q_ref, 0), (k_ref, GDN_DIM)):
            a = c[:, base + h * GDN_DK:base + (h + 1) * GDN_DK]
            a = a * lax.rsqrt(jnp.sum(a * a, -1, keepdims=True) + RMS_EPS)
            ref[idx + (slice(None), slice(h * GDN_DK, (h + 1) * GDN_DK))] = a
    v_ref[idx + (slice(None), slice(None))] = c[:, 2 * GDN_DIM:]


def _proj_b_seq_kernel(x_ref, hist_ref, win_ref, wc_ref, arow_ref, dtrow_ref, k_ref, v_ref,
                       q_o, k_o, v_o, gb_o, zg_o, om_o, hist_o_ref, sbuf, kbd, vbd, *, tm):
    t = pl.program_id(1)
    lo = SUBLANES - (GDN_CONV_W - 1)
    nqkv = 3 * GDN_DIM

    @pl.when(t == 0)
    def _():
        sbuf[lo:SUBLANES, :] = hist_ref[0]
        kbd[...] = _head_blockdiag(k_ref[0])
        vbd[...] = _head_blockdiag(v_ref[0])

    h = _dot(x_ref[0].astype(BF16), win_ref[...])
    qkv = h[:, :nqkv]
    sbuf[SUBLANES:SUBLANES + tm, :] = qkv
    wc = wc_ref[...]
    conv = wc[GDN_CONV_W - 1:GDN_CONV_W] * qkv
    for j in range(GDN_CONV_W - 1):
        conv = conv + wc[j:j + 1] * sbuf[lo + j:lo + j + tm, :]
    last = sbuf[tm + lo:tm + SUBLANES, :]
    sbuf[lo:SUBLANES, :] = last
    hist_o_ref[0] = last

    _qkv_post(conv, q_o, k_o, v_o, (0,))
    zg_o[0] = _silu(h[:, nqkv:nqkv + GDN_DIM])
    qm = h[:, nqkv + GDN_DIM:nqkv + GDN_DIM + XDIM]
    gb_o[0] = _gdn_gates(h[:, nqkv + GDN_DIM + XDIM:], arow_ref[...], dtrow_ref[...])
    om_o[0] = _mem_attn_seq(qm, kbd[...], vbd[...]).astype(BF16)


def _proj_b_seq(x, hist, w_in, w_conv, a_row, dt_row, mem_k, mem_v, *, j, i):
    bsz, t_len, _ = x.shape
    tm = min(SEQ_TILE, t_len)
    tile = lambda n: pl.BlockSpec((1, tm, n), lambda b, t: (b, t, 0))
    f32o = lambda n: jax.ShapeDtypeStruct((bsz, t_len, n), F32)
    mem = pl.BlockSpec((None, 1, N_MEM, XDIM), lambda b, t: (i, b, 0, 0))
    return pl.pallas_call(
        functools.partial(_proj_b_seq_kernel, tm=tm),
        grid=(bsz, t_len // tm),
        in_specs=[tile(D_MODEL),
                  pl.BlockSpec((1, GDN_CONV_W - 1, 3 * GDN_DIM), lambda b, t: (b, 0, 0)),
                  _layer_block((D_MODEL, W_B_COLS), j),
                  _layer_block((GDN_CONV_W, 3 * GDN_DIM), j),
                  _layer_block((1, GB_LANES), j),
                  _layer_block((1, GB_LANES), j),
                  mem, mem],
        out_specs=[tile(GDN_DIM), tile(GDN_DIM), tile(GDN_DIM), tile(GB_LANES), tile(GDN_DIM),
                   tile(XDIM),
                   pl.BlockSpec((1, GDN_CONV_W - 1, 3 * GDN_DIM), lambda b, t: (b, 0, 0))],
        out_shape=[f32o(GDN_DIM), f32o(GDN_DIM), f32o(GDN_DIM), f32o(GB_LANES), f32o(GDN_DIM),
                   jax.ShapeDtypeStruct((bsz, t_len, XDIM), BF16),
                   jax.ShapeDtypeStruct((bsz, GDN_CONV_W - 1, 3 * GDN_DIM), F32)],
        scratch_shapes=[pltpu.VMEM((tm + SUBLANES, 3 * GDN_DIM), F32),
                        pltpu.VMEM((XH * N_MEM, XDIM), BF16),
                        pltpu.VMEM((XH * N_MEM, XDIM), BF16)],
        compiler_params=_cparams(("arbitrary", "arbitrary")),
        name="proj_b_seq",
    )(x, hist, w_in, w_conv, a_row, dt_row, mem_k, mem_v)


def _gdn_scan_kernel(q_ref, k_ref, v_ref, gb_ref, zg_ref, nw_ref, s0_ref, o_ref, s_o_ref, s_scr,
                     *, bb, nc, c):
    t = pl.program_id(1)
    tc = nc * c

    @pl.when(t == 0)
    def _():
        s_scr[...] = s0_ref[0]

    ri = lax.broadcasted_iota(jnp.int32, (c, c), 0)
    ci = lax.broadcasted_iota(jnp.int32, (c, c), 1)
    tril = ri >= ci
    strict = ri > ci
    rt = lax.broadcasted_iota(jnp.int32, (tc, tc), 0)
    ct = lax.broadcasted_iota(jnp.int32, (tc, tc), 1)
    tril_chunks = ((rt >= ct) & ((rt // c) == (ct // c))).astype(F32)
    eye_l = (lax.broadcasted_iota(jnp.int32, (GB_LANES, GB_LANES), 0)
             == lax.broadcasted_iota(jnp.int32, (GB_LANES, GB_LANES), 1)).astype(F32)
    nw = nw_ref[...]
    n_sq = max(c.bit_length() - 2, 0)

    gcs = []
    for i in range(bb):
        gc_all = _dot_hi(tril_chunks, gb_ref[i])
        gcs.append((gc_all, _dot_nt_hi(eye_l, gc_all)))

    probs = [(i, ic, h) for i in range(bb) for ic in range(nc) for h in range(GDN_H)]
    st = {}
    for p in probs:
        i, ic, h = p
        rows = slice(ic * c, (ic + 1) * c)
        hs = slice(h * GDN_DK, (h + 1) * GDN_DK)
        gc_all, gc_t = gcs[i]
        q = q_ref[i, rows, hs] * (GDN_DK ** -0.5)
        k = k_ref[i, rows, hs]
        gcol = gc_all[rows, GDN_H + h:GDN_H + h + 1]
        grow = gc_t[GDN_H + h:GDN_H + h + 1, rows]
        glast = gc_all[(ic + 1) * c - 1:(ic + 1) * c, GDN_H + h:GDN_H + h + 1]
        egc = jnp.exp(gcol)
        kb = k * gb_ref[i, rows, h:h + 1]
        st[p] = dict(
            decay=jnp.exp(jnp.where(tril, gcol - grow, -jnp.inf)),
            aq=_dot_nt(jnp.concatenate([kb, q], axis=0).astype(BF16), k.astype(BF16)),
            sol=jnp.concatenate([v_ref[i, rows, hs] * gb_ref[i, rows, h:h + 1], kb * egc], axis=-1),
            qd=(q * egc).astype(BF16),
            k_dec=(k * jnp.exp(glast - gcol)).astype(BF16),
            e_last=jnp.exp(glast))
    for p in probs:
        d = st[p]
        d["pb"] = jnp.where(strict, d["aq"][:c] * d["decay"], 0.0).astype(BF16)
        d["qk"] = (d["aq"][c:] * d["decay"]).astype(BF16)
        d["sol"] = d["sol"] - _dot_split_rhs(d["pb"], d["sol"])
    for _ in range(n_sq):
        for p in probs:
            st[p]["pb"] = _dot(st[p]["pb"], st[p]["pb"]).astype(BF16)
        for p in probs:
            st[p]["sol"] = st[p]["sol"] + _dot_split_rhs(st[p]["pb"], st[p]["sol"])
    for p in probs:
        d = st[p]
        d["wq"] = jnp.concatenate([d["sol"][:, GDN_DV:].astype(BF16), d["qd"]], axis=0)
        d["u"] = d["sol"][:, :GDN_DV]

    for ic in range(nc):
        rows = slice(ic * c, (ic + 1) * c)
        seqs = [(i, h) for i in range(bb) for h in range(GDN_H)]
        s_old = {ih: s_scr[ih[0], ih[1]] for ih in seqs}
        ws = {ih: _dot(st[ih[0], ic, ih[1]]["wq"], s_old[ih].astype(BF16)) for ih in seqs}
        vb = {ih: (st[ih[0], ic, ih[1]]["u"] - ws[ih][:c]).astype(BF16) for ih in seqs}
        for ih in seqs:
            d = st[ih[0], ic, ih[1]]
            s_scr[ih[0], ih[1]] = s_old[ih] * d["e_last"] + _dot_tn(d["k_dec"], vb[ih])
        for ih in seqs:
            i, h = ih
            hs = slice(h * GDN_DK, (h + 1) * GDN_DK)
            o = ws[ih][c:] + _dot(st[i, ic, h]["qk"], vb[ih])
            o = o * lax.rsqrt(jnp.mean(o * o, -1, keepdims=True) + RMS_EPS)
            o_ref[i, rows, hs] = (o * nw * zg_ref[i, rows, hs]).astype(o_ref.dtype)
    s_o_ref[...] = s_scr[...]


def _gdn_scan(q, k, v, gb, zg, norm_w, s0_all, *, j, layer, c, bb, nc):
    bsz, t_len, _ = q.shape
    tc = nc * c
    tile = lambda n: pl.BlockSpec((bb, tc, n), lambda b, t: (b, t, 0))
    st = pl.BlockSpec((bb, GDN_H, GDN_DK, GDN_DV), lambda b, t: (b, 0, 0, 0))
    st_in = pl.BlockSpec((1, bb, GDN_H, GDN_DK, GDN_DV), lambda b, t: (layer, b, 0, 0, 0))
    return pl.pallas_call(
        functools.partial(_gdn_scan_kernel, bb=bb, nc=nc, c=c),
        grid=(bsz // bb, t_len // tc),
        in_specs=[tile(GDN_DIM), tile(GDN_DIM), tile(GDN_DIM), tile(GB_LANES), tile(GDN_DIM),
                  _layer_block((1, GDN_DV), j), st_in],
        out_specs=[tile(GDN_DIM), st],
        out_shape=[jax.ShapeDtypeStruct((bsz, t_len, GDN_DIM), BF16),
                   jax.ShapeDtypeStruct((bsz, GDN_H, GDN_DK, GDN_DV), F32)],
        scratch_shapes=[pltpu.VMEM((bb, GDN_H, GDN_DK, GDN_DV), F32)],
        compiler_params=_cparams(("arbitrary", "arbitrary")),
        name="gdn_scan",
    )(q, k, v, gb, zg, norm_w, s0_all)


def _out_ln_kernel(a1_ref, a2_ref, w_ref, x_ref, g_ref, b_ref, o_ref):
    n1 = a1_ref.shape[-1]
    mix = (_dot(a1_ref[...].astype(BF16), w_ref[:n1, :])
           + _dot(a2_ref[...].astype(BF16), w_ref[n1:, :]))
    o_ref[...] = _layer_norm(ALPHA * x_ref[...] + mix, g_ref[...], b_ref[...])


def _out_ln(a1, a2, w, x, ln_g, ln_b, *, j, i):
    rows = x.shape[0]
    tr = min(ROW_TILE, rows)
    n1, n2 = a1.shape[1], a2.shape[1]
    tile = lambda n: pl.BlockSpec((tr, n), lambda r: (r, 0))
    return pl.pallas_call(
        _out_ln_kernel,
        grid=(rows // tr,),
        in_specs=[tile(n1), tile(n2), _layer_block((n1 + n2, D_MODEL), j), tile(D_MODEL),
                  _layer_block((1, D_MODEL), i), _layer_block((1, D_MODEL), i)],
        out_specs=tile(D_MODEL),
        out_shape=jax.ShapeDtypeStruct((rows, D_MODEL), F32),
        compiler_params=_cparams(("arbitrary",)),
        name="out_ln",
    )(a1, a2, w, x, ln_g, ln_b)


def _mix_a_front_sample_kernel(x_ref, hist_ref, win_ref, wc_ref, y_o, qm_o, hist_o, *, t_len, bsz):
    h = _dot(x_ref[...].astype(BF16), win_ref[...])
    u = h[:, 2 * SC_DIM:3 * SC_DIM] * h[:, :SC_DIM]
    slabs = [hist_ref[:, j, :] for j in range(SC_W - 1)]
    slabs += [u[t * bsz:(t + 1) * bsz] for t in range(t_len)]
    wc = wc_ref[...]
    for t in range(t_len):
        conv = wc[0:1] * slabs[t]
        for j in range(1, SC_W):
            conv = conv + wc[j:j + 1] * slabs[t + j]
        y_o[t * bsz:(t + 1) * bsz, :] = (h[t * bsz:(t + 1) * bsz, SC_DIM:2 * SC_DIM] * conv).astype(BF16)
    for j in range(SC_W - 1):
        hist_o[:, j, :] = slabs[t_len + j]
    qm_o[...] = h[:, 3 * SC_DIM:]


def _whole(shape):
    nd = len(shape)
    return pl.BlockSpec(tuple(shape), lambda *_: (0,) * nd)


def _mix_a_front_sample(x2d, hist, w_in, w_conv, *, j, t_len, bsz):
    rows = x2d.shape[0]
    out_shape = [jax.ShapeDtypeStruct((rows, SC_DIM), BF16),
                 jax.ShapeDtypeStruct((rows, XDIM), F32),
                 jax.ShapeDtypeStruct((bsz, SC_W - 1, SC_DIM), F32)]
    return pl.pallas_call(
        functools.partial(_mix_a_front_sample_kernel, t_len=t_len, bsz=bsz),
        grid=(1,),
        in_specs=[_whole(x2d.shape), _layer_block(hist.shape[1:], j), _layer_block(w_in.shape[1:], j),
                  _layer_block(w_conv.shape[1:], j)],
        out_specs=[_whole(s.shape) for s in out_shape],
        out_shape=out_shape,
        compiler_params=_cparams(("arbitrary",)),
        name="mix_a_front_sample",
    )(x2d, hist, w_in, w_conv)


def _proj_b_front_sample_kernel(x_ref, hist_ref, win_ref, wc_ref, arow_ref, dtrow_ref,
                                q_o, k_o, v_o, gb_o, zg_o, qm_o, hist_o, *, t_len, bsz):
    nqkv = 3 * GDN_DIM
    h = _dot(x_ref[...].astype(BF16), win_ref[...])
    slabs = [hist_ref[j] for j in range(GDN_CONV_W - 1)]
    slabs += [h[t * bsz:(t + 1) * bsz, :nqkv] for t in range(t_len)]
    wc = wc_ref[...]
    for t in range(t_len):
        conv = wc[0:1] * slabs[t]
        for j in range(1, GDN_CONV_W):
            conv = conv + wc[j:j + 1] * slabs[t + j]
        _qkv_post(conv, q_o, k_o, v_o, (t,))
    for j in range(GDN_CONV_W - 1):
        hist_o[j] = slabs[t_len + j]
    zg_o[...] = _silu(h[:, nqkv:nqkv + GDN_DIM])
    qm_o[...] = h[:, nqkv + GDN_DIM:nqkv + GDN_DIM + XDIM]
    gb_o[...] = _gdn_gates(h[:, nqkv + GDN_DIM + XDIM:], arow_ref[...], dtrow_ref[...])


def _proj_b_front_sample(x2d, hist, w_in, w_conv, a_row, dt_row, *, j, t_len, bsz):
    rows = x2d.shape[0]
    tmaj = jax.ShapeDtypeStruct((t_len, bsz, GDN_DIM), F32)
    out_shape = [tmaj, tmaj, tmaj,
                 jax.ShapeDtypeStruct((rows, GB_LANES), F32),
                 jax.ShapeDtypeStruct((rows, GDN_DIM), F32),
                 jax.ShapeDtypeStruct((rows, XDIM), F32),
                 jax.ShapeDtypeStruct((GDN_CONV_W - 1, bsz, 3 * GDN_DIM), F32)]
    return pl.pallas_call(
        functools.partial(_proj_b_front_sample_kernel, t_len=t_len, bsz=bsz),
        grid=(1,),
        in_specs=[_whole(x2d.shape), _layer_block(hist.shape[1:], j), _layer_block(w_in.shape[1:], j),
                  _layer_block(w_conv.shape[1:], j), _layer_block((1, GB_LANES), j),
                  _layer_block((1, GB_LANES), j)],
        out_specs=[_whole(s.shape) for s in out_shape],
        out_shape=out_shape,
        compiler_params=_cparams(("arbitrary",)),
        name="proj_b_front_sample",
    )(x2d, hist, w_in, w_conv, a_row, dt_row)


def _ffn_sample_kernel(x_ref, hist_ref, w_ref, c_ref, wdn_ref, g_ref, b_ref, o_ref, hist_o,
                       acc, gate, *, t_len, bsz, nj):
    s = pl.program_id(0)
    h = _dot(x_ref[...].astype(BF16), w_ref[...])
    slabs = [hist_ref[:, r, :] for r in range(FFN_W - 1)]
    slabs += [h[t * bsz:(t + 1) * bsz] for t in range(t_len)]
    wc = c_ref[...]
    outs = []
    for t in range(t_len):
        c = wc[0:1] * slabs[t]
        for w in range(1, FFN_W):
            c = c + wc[w:w + 1] * slabs[t + w]
        outs.append(c)
    for r in range(FFN_W - 1):
        hist_o[:, r, :] = slabs[t_len + r]
    conv = jnp.concatenate(outs, axis=0)

    @pl.when(s < nj)
    def _():
        gate[s] = _silu(conv)

    @pl.when(s >= nj)
    def _():
        d = _dot((gate[s - nj] * conv).astype(BF16), wdn_ref[...])

        @pl.when(s == nj)
        def _():
            acc[...] = d

        @pl.when(s > nj)
        def _():
            acc[...] += d

    @pl.when(s == 2 * nj - 1)
    def _():
        o_ref[...] = _layer_norm(ALPHA * x_ref[...] + acc[...], g_ref[...], b_ref[...])


def _ffn_sample(x2d, hist, w_up, w_conv, w_down, ln_g, ln_b, *, i, t_len, bsz):
    rows = x2d.shape[0]
    tf = FF_TILE_SAMPLE
    nj = D_FF // tf
    return pl.pallas_call(
        functools.partial(_ffn_sample_kernel, t_len=t_len, bsz=bsz, nj=nj),
        grid=(2 * nj,),
        in_specs=[_resident((rows, D_MODEL)),
                  pl.BlockSpec((None, bsz, FFN_W - 1, tf), lambda s: (i, 0, 0, s)),
                  pl.BlockSpec((None, D_MODEL, tf), lambda s: (i, 0, s)),
                  pl.BlockSpec((None, FFN_W, tf), lambda s: (i, 0, s)),
                  pl.BlockSpec((None, tf, D_MODEL), lambda s: (i, jnp.maximum(s - nj, 0), 0)),
                  _layer_block((1, D_MODEL), i), _layer_block((1, D_MODEL), i)],
        out_specs=[pl.BlockSpec((rows, D_MODEL), lambda s: (0, 0)),
                   pl.BlockSpec((bsz, FFN_W - 1, tf), lambda s: (0, 0, s))],
        out_shape=[jax.ShapeDtypeStruct((rows, D_MODEL), F32),
                   jax.ShapeDtypeStruct((bsz, FFN_W - 1, 2 * D_FF), F32)],
        scratch_shapes=[pltpu.VMEM((rows, D_MODEL), F32), pltpu.VMEM((nj, rows, tf), F32)],
        compiler_params=_cparams(("arbitrary",)),
        name="ffn_sample",
    )(x2d, hist, w_up, w_conv, w_down, ln_g, ln_b)


def _prep_weights(w_in_a, w_out_a, w_in_b, a_log, dt_bias, w_out_b, w_mem_kv, w_up, w_down):
    nqkvz = 4 * GDN_DIM
    w_b = jnp.concatenate(
        [w_in_b[..., :nqkvz], w_in_b[..., nqkvz + 2 * GDN_H:],
         jnp.pad(w_in_b[..., nqkvz:nqkvz + 2 * GDN_H], ((0, 0), (0, 0), (0, GB_LANES - 2 * GDN_H)))],
        axis=-1).astype(BF16)
    pad = GB_LANES - 2 * GDN_H
    a_row = jnp.pad(-jnp.exp(a_log.astype(F32)), ((0, 0), (GDN_H, pad)))[:, None, :]
    dt_row = jnp.pad(dt_bias.astype(F32), ((0, 0), (GDN_H, pad)))[:, None, :]
    return dict(w_in_a=w_in_a.astype(BF16), w_out_a=w_out_a.astype(BF16), w_in_b=w_b,
                w_out_b=w_out_b.astype(BF16), w_mem_kv=w_mem_kv.astype(BF16),
                w_up=w_up.astype(BF16), w_down=w_down.astype(BF16), a_row=a_row, dt_row=dt_row)


def _trunk_prompt(x, mem_k, mem_v, wts, conv_a, conv_b, gdn_norm_w, ln1_g, ln1_b, ln2_g, ln2_b,
                  w_conv_ffn):
    bsz, t_len, _ = x.shape
    z = lambda *s: jnp.zeros(s, F32)
    new_sc, new_gc, new_gs, new_ffn = [], [], [], []
    for i in range(DEPTH):
        j = i // 2
        if i % 2 == 0:
            x, hs = _mix_a_seq(x, z(bsz, SC_W - 1, SC_DIM), wts["w_in_a"], conv_a, wts["w_out_a"],
                               mem_k, mem_v, ln1_g, ln1_b, j=j, i=i)
            new_sc.append(hs)
        else:
            q, k, v, gb, zg, om, hg = _proj_b_seq(
                x, z(bsz, GDN_CONV_W - 1, 3 * GDN_DIM), wts["w_in_b"], conv_b,
                wts["a_row"], wts["dt_row"], mem_k, mem_v, j=j, i=i)
            o, sg = _gdn_scan(q, k, v, gb, zg, gdn_norm_w, z(1, bsz, GDN_H, GDN_DK, GDN_DV),
                              j=j, layer=0, c=GDN_CHUNK, bb=SCAN_SEQS, nc=SCAN_CHUNKS)
            rows = bsz * t_len
            x = _out_ln(o.reshape(rows, GDN_DIM), om.reshape(rows, XDIM), wts["w_out_b"],
                        x.reshape(rows, D_MODEL), ln1_g, ln1_b, j=j, i=i).reshape(bsz, t_len, D_MODEL)
            new_gc.append(hg)
            new_gs.append(sg)
        x, hf = _ffn_seq(x, z(bsz, FFN_W - 1, 2 * D_FF), wts["w_up"], w_conv_ffn, wts["w_down"],
                         ln2_g, ln2_b, i=i)
        new_ffn.append(hf)
    return x, jnp.stack(new_sc), jnp.stack(new_gc), jnp.stack(new_gs), jnp.stack(new_ffn)


def _trunk_sample(x, mem_k, mem_v, sc_hist, gdn_hist, gdn_s, ffn_hist, wts, conv_a, conv_b,
                  gdn_norm_w, ln1_g, ln1_b, ln2_g, ln2_b, w_conv_ffn):
    bsz, t_len, _ = x.shape
    rows = bsz * t_len
    t_pad = SUBLANES
    x2 = jnp.transpose(x, (1, 0, 2)).reshape(rows, D_MODEL)
    to_bm = lambda a: jnp.transpose(a, (1, 0, 2))
    new_sc, new_gc, new_gs, new_ffn = [], [], [], []
    mem_kt = jnp.transpose(mem_k, (0, 1, 3, 4, 2))
    mem_vt = jnp.transpose(mem_v, (0, 1, 3, 4, 2))
    gdn_hist = jnp.transpose(gdn_hist, (0, 2, 1, 3))
    for i in range(DEPTH):
        j = i // 2
        if i % 2 == 0:
            y, qm, hs = _mix_a_front_sample(x2, sc_hist, wts["w_in_a"], conv_a, j=j,
                                            t_len=t_len, bsz=bsz)
            om = _attn_sample(qm.reshape(t_len, bsz, XDIM), mem_kt, mem_vt, i).reshape(rows, XDIM)
            x2 = _out_ln(y, om, wts["w_out_a"], x2, ln1_g, ln1_b, j=j, i=i)
            new_sc.append(hs)
        else:
            q, k, v, gb, zg, qm, hg = _proj_b_front_sample(
                x2, gdn_hist, wts["w_in_b"], conv_b, wts["a_row"], wts["dt_row"], j=j,
                t_len=t_len, bsz=bsz)
            om = _attn_sample(qm.reshape(t_len, bsz, XDIM), mem_kt, mem_vt, i).reshape(rows, XDIM)
            padt = lambda a: jnp.pad(to_bm(a.reshape(t_len, bsz, -1)), ((0, 0), (0, t_pad - t_len), (0, 0)))
            o, sg = _gdn_scan(padt(q), padt(k), padt(v), padt(gb), padt(zg), gdn_norm_w, gdn_s,
                              j=j, layer=j, c=t_pad, bb=SCAN_SEQS_SAMPLE, nc=1)
            o = jnp.transpose(o[:, :t_len], (1, 0, 2)).reshape(rows, GDN_DIM)
            x2 = _out_ln(o, om, wts["w_out_b"], x2, ln1_g, ln1_b, j=j, i=i)
            new_gc.append(hg)
            new_gs.append(sg)
        x2, hf = _ffn_sample(x2, ffn_hist, wts["w_up"], w_conv_ffn, wts["w_down"],
                             ln2_g, ln2_b, i=i, t_len=t_len, bsz=bsz)
        new_ffn.append(hf)
    y = jnp.transpose(x2.reshape(t_len, bsz, D_MODEL), (1, 0, 2))
    gc = jnp.transpose(jnp.stack(new_gc), (0, 2, 1, 3))
    return y, jnp.stack(new_sc), gc, jnp.stack(new_gs), jnp.stack(new_ffn)


def kernel(x_prompt, x_sample, mem_prompt, cache_mem_k, cache_mem_v, state_shortconv, state_gdn_conv,
           state_gdn, state_ffn_conv, w_in_a, conv_a, w_out_a, w_in_b, conv_b, a_log, dt_bias,
           gdn_norm_w, w_out_b, w_mem_kv, ln1_g, ln1_b, ln2_g, ln2_b, w_up, w_conv_ffn, w_down):
    wts = _prep_weights(w_in_a, w_out_a, w_in_b, a_log, dt_bias, w_out_b, w_mem_kv, w_up, w_down)
    row3 = lambda a: a.reshape(a.shape[0], 1, a.shape[1])
    shared = (wts, conv_a, conv_b, row3(gdn_norm_w), row3(ln1_g), row3(ln1_b), row3(ln2_g),
              row3(ln2_b), w_conv_ffn)
    bsz = x_prompt.shape[0]
    k2, v2 = _mem_kv(mem_prompt.reshape(bsz * N_MEM, D_MODEL), wts["w_mem_kv"])
    mem_k_prompt = k2.reshape(DEPTH, bsz, N_MEM, XDIM)
    mem_v_prompt = v2.reshape(DEPTH, bsz, N_MEM, XDIM)
    y_prompt, sc_p, gc_p, gs_p, ffn_p = _trunk_prompt(x_prompt, mem_k_prompt, mem_v_prompt, *shared)
    y_sample, sc_s, gc_s, gs_s, ffn_s = _trunk_sample(
        x_sample, cache_mem_k, cache_mem_v, state_shortconv, state_gdn_conv, state_gdn,
        state_ffn_conv, *shared)
    shape5 = (DEPTH, bsz, N_MEM, XH, XD)
    return (y_prompt, y_sample, mem_k_prompt.reshape(shape5), mem_v_prompt.reshape(shape5),
            sc_p, gc_p, gs_p, ffn_p, sc_s, gc_s, gs_s, ffn_s)
```

```python
import functools

import jax
import jax.numpy as jnp
from jax import lax
from jax.experimental import pallas as pl
from jax.experimental.pallas import tpu as pltpu

F32 = jnp.float32
BF16 = jnp.bfloat16

DEPTH = 4
D_MODEL = 1024
SC_DIM = 768
SC_W = 3
GDN_H = 6
GDN_DK = 128
GDN_DV = 128
GDN_DIM = GDN_H * GDN_DK
GDN_CONV_W = 4
GDN_CHUNK = 64
N_MEM = 256
XH = 4
XD = 64
XDIM = XH * XD
D_FF = 2816
FFN_W = 3
ALPHA = (2.0 * DEPTH) ** 0.25
LN_EPS = 1e-5
RMS_EPS = 1e-6

V7X_VMEM_BYTES = 64 * 1024 * 1024
VMEM_LIMIT = V7X_VMEM_BYTES - 8 * 1024 * 1024
SUBLANES = 8
LANES = 128

GB_LANES = LANES

SEQ_TILE = 512
MIX_TILE = 512
MIX_SUBTILES = 2
SCAN_SEQS = 2
SCAN_CHUNKS = 2
SCAN_SEQS_SAMPLE = 4
ROW_TILE = 512
FF_TILE = 2816
FF_TILE_SAMPLE = 1408
ATTN_BATCH_BLOCK = 16
ATTN_UNROLL = 4
W_B_COLS = 3 * GDN_DIM + GDN_DIM + XDIM + GB_LANES


def _cparams(sem):
    return pltpu.CompilerParams(dimension_semantics=sem, vmem_limit_bytes=VMEM_LIMIT)


def _resident(shape):
    nd = len(shape)
    return pl.BlockSpec(shape, lambda *_: (0,) * nd, pipeline_mode=pl.Buffered(1))


def _layer_block(shape, layer):
    nd = len(shape)
    return pl.BlockSpec((None,) + tuple(shape), lambda *_: (layer,) + (0,) * nd,
                        pipeline_mode=pl.Buffered(1))


def _silu(x):
    return x * (1.0 / (1.0 + jnp.exp(-x)))


def _sigmoid(x):
    return 1.0 / (1.0 + jnp.exp(-x))


def _softplus(x):
    return jnp.maximum(x, 0.0) + jnp.log(1.0 + jnp.exp(-jnp.abs(x)))


def _layer_norm(v, g, b):
    mu = jnp.mean(v, -1, keepdims=True)
    d = v - mu
    var = jnp.mean(d * d, -1, keepdims=True)
    return d * lax.rsqrt(var + LN_EPS) * g + b


def _dot(a, b):
    return jnp.dot(a, b, preferred_element_type=F32)


def _dot_nt(a, b):
    return lax.dot_general(a, b, (((1,), (1,)), ((), ())), preferred_element_type=F32)


def _dot_tn(a, b):
    return lax.dot_general(a, b, (((0,), (0,)), ((), ())), preferred_element_type=F32)


def _dot_split_rhs(a, x):
    hi = x.astype(BF16)
    lo = (x - hi.astype(F32)).astype(BF16)
    return _dot(a, hi) + _dot(a, lo)


def _dot_hi(a, b):
    return jnp.dot(a, b, preferred_element_type=F32, precision=lax.Precision.HIGHEST)


def _dot_nt_hi(a, b):
    return lax.dot_general(a, b, (((1,), (1,)), ((), ())), preferred_element_type=F32,
                           precision=lax.Precision.HIGHEST)


def _kv_kernel(m_ref, w_ref, k_ref, v_ref):
    kv = _dot(m_ref[...].astype(BF16), w_ref[0])
    k_ref[0] = kv[:, :XDIM]
    v_ref[0] = kv[:, XDIM:]


def _mem_kv(mem2d, w_kv):
    rows = mem2d.shape[0]
    out = jax.ShapeDtypeStruct((DEPTH, rows, XDIM), F32)
    return pl.pallas_call(
        _kv_kernel,
        grid=(DEPTH,),
        in_specs=[_resident((rows, D_MODEL)),
                  pl.BlockSpec((1, D_MODEL, 2 * XDIM), lambda l: (l, 0, 0))],
        out_specs=[pl.BlockSpec((1, rows, XDIM), lambda l: (l, 0, 0))] * 2,
        out_shape=[out, out],
        compiler_params=_cparams(("arbitrary",)),
        name="mem_kv",
    )(mem2d, w_kv)


def _head_blockdiag(kv):
    lane_head = lax.broadcasted_iota(jnp.int32, kv.shape, 1) // XD
    return jnp.concatenate(
        [jnp.where(lane_head == h, kv, 0.0).astype(BF16) for h in range(XH)], axis=0)


def _softmax_rows(s):
    m = jnp.max(s, -1, keepdims=True)
    e = jnp.exp(s - m)
    return e / jnp.sum(e, -1, keepdims=True)


def _attn_sample_kernel(q_ref, k_ref, v_ref, o_ref, *, bb, t_len):
    def body(g, carry):
        elems = [g * ATTN_UNROLL + e for e in range(ATTN_UNROLL)]
        q8 = []
        for i in elems:
            rows = [q_ref[t, pl.ds(i, 1), :] for t in range(t_len)]
            rows.append(jnp.zeros((SUBLANES - t_len, XDIM), F32))
            q8.append(jnp.concatenate(rows, axis=0).astype(BF16))
        s = [[_dot(q8[e][:, h * XD:(h + 1) * XD], k_ref[0, i, h].astype(BF16)) * (XD ** -0.5)
              for h in range(XH)] for e, i in enumerate(elems)]
        p = [[_softmax_rows(sh).astype(BF16) for sh in se] for se in s]
        for e, i in enumerate(elems):
            o8 = jnp.concatenate([_dot_nt(p[e][h], v_ref[0, i, h].astype(BF16)) for h in range(XH)],
                                 axis=-1)
            for t in range(t_len):
                o_ref[t, pl.ds(i, 1), :] = o8[t:t + 1]
        return carry

    lax.fori_loop(0, bb // ATTN_UNROLL, body, 0)


def _attn_sample(qm_tm, mem_kt, mem_vt, layer):
    t_len, bsz, _ = qm_tm.shape
    bb = min(ATTN_BATCH_BLOCK, bsz)
    kv = pl.BlockSpec((1, bb, XH, XD, N_MEM), lambda i: (layer, i, 0, 0, 0))
    return pl.pallas_call(
        functools.partial(_attn_sample_kernel, bb=bb, t_len=t_len),
        grid=(bsz // bb,),
        in_specs=[pl.BlockSpec((t_len, bb, XDIM), lambda i: (0, i, 0)), kv, kv],
        out_specs=pl.BlockSpec((t_len, bb, XDIM), lambda i: (0, i, 0)),
        out_shape=jax.ShapeDtypeStruct((t_len, bsz, XDIM), F32),
        compiler_params=_cparams(("arbitrary",)),
        name="attn_sample",
    )(qm_tm, mem_kt, mem_vt)


def _mix_a_seq_kernel(x_ref, hist_ref, win_ref, wc_ref, wout_ref, k_ref, v_ref, g_ref, b_ref,
                      o_ref, hist_o_ref, ubuf, kbd, vbd, *, tm):
    t = pl.program_id(1)
    lo = SUBLANES - (SC_W - 1)

    @pl.when(t == 0)
    def _():
        ubuf[lo:SUBLANES, :] = hist_ref[0]
        kbd[...] = _head_blockdiag(k_ref[0])
        vbd[...] = _head_blockdiag(v_ref[0])

    ts = tm // MIX_SUBTILES
    subs = range(MIX_SUBTILES)
    rows = [slice(s * ts, (s + 1) * ts) for s in subs]
    hs = [_dot(x_ref[0, rows[s], :].astype(BF16), win_ref[...]) for s in subs]
    us = [hs[s][:, 2 * SC_DIM:3 * SC_DIM] * hs[s][:, :SC_DIM] for s in subs]
    for s in subs:
        ubuf[SUBLANES + s * ts:SUBLANES + (s + 1) * ts, :] = us[s]
    wc = wc_ref[...]
    ys = []
    for s in subs:
        conv = wc[SC_W - 1:SC_W] * us[s]
        for j in range(SC_W - 1):
            conv = conv + wc[j:j + 1] * ubuf[lo + j + s * ts:lo + j + (s + 1) * ts, :]
        ys.append((hs[s][:, SC_DIM:2 * SC_DIM] * conv).astype(BF16))
    last = ubuf[tm + lo:tm + SUBLANES, :]
    ubuf[lo:SUBLANES, :] = last
    hist_o_ref[0] = last

    sc = [_dot_nt(hs[s][:, 3 * SC_DIM:].astype(BF16), kbd[...]) * (XD ** -0.5) for s in subs]
    ps = [jnp.concatenate([_softmax_rows(sc[s][:, h * N_MEM:(h + 1) * N_MEM]).astype(BF16)
                           for h in range(XH)], axis=-1) for s in subs]
    oms = [_dot(ps[s], vbd[...]).astype(BF16) for s in subs]
    mix = [_dot(ys[s], wout_ref[:SC_DIM, :]) + _dot(oms[s], wout_ref[SC_DIM:, :]) for s in subs]
    for s in subs:
        o_ref[0, rows[s], :] = _layer_norm(ALPHA * x_ref[0, rows[s], :] + mix[s], g_ref[...], b_ref[...])


def _mix_a_seq(x, hist, w_in, w_conv, w_out, mem_k, mem_v, ln_g, ln_b, *, j, i):
    bsz, t_len, _ = x.shape
    tm = min(MIX_TILE, t_len)
    mem = pl.BlockSpec((None, 1, N_MEM, XDIM), lambda b, t: (i, b, 0, 0))
    return pl.pallas_call(
        functools.partial(_mix_a_seq_kernel, tm=tm),
        grid=(bsz, t_len // tm),
        in_specs=[pl.BlockSpec((1, tm, D_MODEL), lambda b, t: (b, t, 0)),
                  pl.BlockSpec((1, SC_W - 1, SC_DIM), lambda b, t: (b, 0, 0)),
                  _layer_block(w_in.shape[1:], j),
                  _layer_block((SC_W, SC_DIM), j),
                  _layer_block((SC_DIM + XDIM, D_MODEL), j),
                  mem, mem,
                  _layer_block((1, D_MODEL), i),
                  _layer_block((1, D_MODEL), i)],
        out_specs=[pl.BlockSpec((1, tm, D_MODEL), lambda b, t: (b, t, 0)),
                   pl.BlockSpec((1, SC_W - 1, SC_DIM), lambda b, t: (b, 0, 0))],
        out_shape=[jax.ShapeDtypeStruct((bsz, t_len, D_MODEL), F32),
                   jax.ShapeDtypeStruct((bsz, SC_W - 1, SC_DIM), F32)],
        scratch_shapes=[pltpu.VMEM((tm + SUBLANES, SC_DIM), F32),
                        pltpu.VMEM((XH * N_MEM, XDIM), BF16),
                        pltpu.VMEM((XH * N_MEM, XDIM), BF16)],
        compiler_params=_cparams(("arbitrary", "arbitrary")),
        name="mix_a_seq",
    )(x, hist, w_in, w_conv, w_out, mem_k, mem_v, ln_g, ln_b)


def _ffn_seq_kernel(x_ref, hist_ref, wup_ref, wc_ref, wdn_ref, g_ref, b_ref,
                    o_ref, hist_o_ref, gbuf, ubuf, *, tm, tf):
    t = pl.program_id(1)
    lo = SUBLANES - (FFN_W - 1)
    nj = D_FF // tf
    halves = ((gbuf, 0), (ubuf, D_FF))

    @pl.when(t == 0)
    def _():
        for buf, off in halves:
            for j in range(nj):
                buf[j, lo:SUBLANES, :] = hist_ref[0, :, off + j * tf:off + (j + 1) * tf]

    x = x_ref[0]
    xb = x.astype(BF16)

    def up_project(j):
        return [_dot(xb, wup_ref[:, off + j * tf:off + (j + 1) * tf]) for _, off in halves]

    def conv_act(j, hs):
        conv = []
        for (buf, off), h in zip(halves, hs):
            c0 = off + j * tf
            buf[j, SUBLANES:SUBLANES + tm, :] = h
            c = wc_ref[FFN_W - 1:FFN_W, c0:c0 + tf] * h
            for w in range(FFN_W - 1):
                c = c + wc_ref[w:w + 1, c0:c0 + tf] * buf[j, lo + w:lo + w + tm, :]
            last = buf[j, tm + lo:tm + SUBLANES, :]
            buf[j, lo:SUBLANES, :] = last
            hist_o_ref[0, :, c0:c0 + tf] = last
            conv.append(c)
        return (_silu(conv[0]) * conv[1]).astype(BF16)

    acc = None
    hs = up_project(0)
    for j in range(nj):
        hs_next = up_project(j + 1) if j + 1 < nj else None
        d = _dot(conv_act(j, hs), wdn_ref[j * tf:(j + 1) * tf, :])
        acc = d if acc is None else acc + d
        hs = hs_next
    o_ref[0] = _layer_norm(ALPHA * x + acc, g_ref[...], b_ref[...])


def _ffn_seq(x, hist, w_up, w_conv, w_down, ln_g, ln_b, *, i):
    bsz, t_len, _ = x.shape
    tm = min(SEQ_TILE, t_len)
    tf = FF_TILE
    return pl.pallas_call(
        functools.partial(_ffn_seq_kernel, tm=tm, tf=tf),
        grid=(bsz, t_len // tm),
        in_specs=[pl.BlockSpec((1, tm, D_MODEL), lambda b, t: (b, t, 0)),
                  pl.BlockSpec((1, FFN_W - 1, 2 * D_FF), lambda b, t: (b, 0, 0)),
                  _layer_block((D_MODEL, 2 * D_FF), i),
                  _layer_block((FFN_W, 2 * D_FF), i),
                  _layer_block((D_FF, D_MODEL), i),
                  _layer_block((1, D_MODEL), i),
                  _layer_block((1, D_MODEL), i)],
        out_specs=[pl.BlockSpec((1, tm, D_MODEL), lambda b, t: (b, t, 0)),
                   pl.BlockSpec((1, FFN_W - 1, 2 * D_FF), lambda b, t: (b, 0, 0))],
        out_shape=[jax.ShapeDtypeStruct((bsz, t_len, D_MODEL), F32),
                   jax.ShapeDtypeStruct((bsz, FFN_W - 1, 2 * D_FF), F32)],
        scratch_shapes=[pltpu.VMEM((D_FF // tf, tm + SUBLANES, tf), F32),
                        pltpu.VMEM((D_FF // tf, tm + SUBLANES, tf), F32)],
        compiler_params=_cparams(("arbitrary", "arbitrary")),
        name="ffn_seq",
    )(x, hist, w_up, w_conv, w_down, ln_g, ln_b)


def _gdn_gates(ba, a_row, dt_row):
    lane = lax.broadcasted_iota(jnp.int32, ba.shape, 1)
    return jnp.where(lane < GDN_H, _sigmoid(ba), a_row * _softplus(ba + dt_row))


def _qkv_post(c, q_ref, k_ref, v_ref, idx):
    c = _silu(c)
    for h in range(GDN_H):
        for ref, base in ((q_ref, 0), (k_ref, GDN_DIM)):
            a = c[:, base + h * GDN_DK:base + (h + 1) * GDN_DK]
            a = a * lax.rsqrt(jnp.sum(a * a, -1, keepdims=True) + RMS_EPS)
            ref[idx + (slice(h * GDN_DK, (h + 1) * GDN_DK),)] = a
    v_ref[idx + (slice(None),)] = c[:, 2 * GDN_DIM:]


def _proj_b_seq_kernel(x_ref, hist_ref, win_ref, wc_ref, arow_ref, dtrow_ref, k_ref, v_ref,
                       q_o, k_o, v_o, gb_o, zg_o, om_o, hist_o_ref, sbuf, kbd, vbd, *, tm):
    t = pl.program_id(1)
    lo = SUBLANES - (GDN_CONV_W - 1)
    nqkv = 3 * GDN_DIM

    @pl.when(t == 0)
    def _():
        sbuf[lo:SUBLANES, :] = hist_ref[0]
        kbd[...] = _head_blockdiag(k_ref[0])
        vbd[...] = _head_blockdiag(v_ref[0])

    ts = tm // MIX_SUBTILES
    subs = range(MIX_SUBTILES)
    rows = [slice(s * ts, (s + 1) * ts) for s in subs]
    hs = [_dot(x_ref[0, rows[s], :].astype(BF16), win_ref[...]) for s in subs]
    for s in subs:
        sbuf[SUBLANES + s * ts:SUBLANES + (s + 1) * ts, :] = hs[s][:, :nqkv]
    wc = wc_ref[...]
    qm0 = nqkv + GDN_DIM
    sc = []
    for s in subs:
        conv = wc[GDN_CONV_W - 1:GDN_CONV_W] * hs[s][:, :nqkv]
        for j in range(GDN_CONV_W - 1):
            conv = conv + wc[j:j + 1] * sbuf[lo + j + s * ts:lo + j + (s + 1) * ts, :]
        sc.append(_dot_nt(hs[s][:, qm0:qm0 + XDIM].astype(BF16), kbd[...]) * (XD ** -0.5))
        _qkv_post(conv, q_o, k_o, v_o, (0, rows[s]))
        zg_o[0, rows[s], :] = _silu(hs[s][:, nqkv:qm0])
        gb_o[0, rows[s], :] = _gdn_gates(hs[s][:, qm0 + XDIM:], arow_ref[...], dtrow_ref[...])
    last = sbuf[tm + lo:tm + SUBLANES, :]
    sbuf[lo:SUBLANES, :] = last
    hist_o_ref[0] = last
    ps = [jnp.concatenate([_softmax_rows(sc[s][:, h * N_MEM:(h + 1) * N_MEM]).astype(BF16)
                           for h in range(XH)], axis=-1) for s in subs]
    for s in subs:
        om_o[0, rows[s], :] = _dot(ps[s], vbd[...]).astype(BF16)


def _proj_b_seq(x, hist, w_in, w_conv, a_row, dt_row, mem_k, mem_v, *, j, i):
    bsz, t_len, _ = x.shape
    tm = min(MIX_TILE, t_len)
    tile = lambda n: pl.BlockSpec((1, tm, n), lambda b, t: (b, t, 0))
    f32o = lambda n: jax.ShapeDtypeStruct((bsz, t_len, n), F32)
    mem = pl.BlockSpec((None, 1, N_MEM, XDIM), lambda b, t: (i, b, 0, 0))
    return pl.pallas_call(
        functools.partial(_proj_b_seq_kernel, tm=tm),
        grid=(bsz, t_len // tm),
        in_specs=[tile(D_MODEL),
                  pl.BlockSpec((1, GDN_CONV_W - 1, 3 * GDN_DIM), lambda b, t: (b, 0, 0)),
                  _layer_block((D_MODEL, W_B_COLS), j),
                  _layer_block((GDN_CONV_W, 3 * GDN_DIM), j),
                  _layer_block((1, GB_LANES), j),
                  _layer_block((1, GB_LANES), j),
                  mem, mem],
        out_specs=[tile(GDN_DIM), tile(GDN_DIM), tile(GDN_DIM), tile(GB_LANES), tile(GDN_DIM),
                   tile(XDIM),
                   pl.BlockSpec((1, GDN_CONV_W - 1, 3 * GDN_DIM), lambda b, t: (b, 0, 0))],
        out_shape=[f32o(GDN_DIM), f32o(GDN_DIM), f32o(GDN_DIM), f32o(GB_LANES), f32o(GDN_DIM),
                   jax.ShapeDtypeStruct((bsz, t_len, XDIM), BF16),
                   jax.ShapeDtypeStruct((bsz, GDN_CONV_W - 1, 3 * GDN_DIM), F32)],
        scratch_shapes=[pltpu.VMEM((tm + SUBLANES, 3 * GDN_DIM), F32),
                        pltpu.VMEM((XH * N_MEM, XDIM), BF16),
                        pltpu.VMEM((XH * N_MEM, XDIM), BF16)],
        compiler_params=_cparams(("arbitrary", "arbitrary")),
        name="proj_b_seq",
    )(x, hist, w_in, w_conv, a_row, dt_row, mem_k, mem_v)


def _gdn_scan_kernel(q_ref, k_ref, v_ref, gb_ref, zg_ref, nw_ref, s0_ref, o_ref, s_o_ref, s_scr,
                     *, bb, nc, c):
    t = pl.program_id(1)
    tc = nc * c

    @pl.when(t == 0)
    def _():
        s_scr[...] = s0_ref[0]

    ri = lax.broadcasted_iota(jnp.int32, (c, c), 0)
    ci = lax.broadcasted_iota(jnp.int32, (c, c), 1)
    tril = ri >= ci
    strict = ri > ci
    rt = lax.broadcasted_iota(jnp.int32, (tc, tc), 0)
    ct = lax.broadcasted_iota(jnp.int32, (tc, tc), 1)
    tril_chunks = ((rt >= ct) & ((rt // c) == (ct // c))).astype(F32)
    eye_l = (lax.broadcasted_iota(jnp.int32, (GB_LANES, GB_LANES), 0)
             == lax.broadcasted_iota(jnp.int32, (GB_LANES, GB_LANES), 1)).astype(F32)
    nw = nw_ref[...]
    n_sq = max(c.bit_length() - 2, 0)

    gcs = []
    for i in range(bb):
        gc_all = _dot_hi(tril_chunks, gb_ref[i])
        gcs.append((gc_all, _dot_nt_hi(eye_l, gc_all)))

    probs = [(i, ic, h) for i in range(bb) for ic in range(nc) for h in range(GDN_H)]
    st = {}
    for p in probs:
        i, ic, h = p
        rows = slice(ic * c, (ic + 1) * c)
        hs = slice(h * GDN_DK, (h + 1) * GDN_DK)
        gc_all, gc_t = gcs[i]
        q = q_ref[i, rows, hs] * (GDN_DK ** -0.5)
        k = k_ref[i, rows, hs]
        gcol = gc_all[rows, GDN_H + h:GDN_H + h + 1]
        grow = gc_t[GDN_H + h:GDN_H + h + 1, rows]
        glast = gc_all[(ic + 1) * c - 1:(ic + 1) * c, GDN_H + h:GDN_H + h + 1]
        egc = jnp.exp(gcol)
        kb = k * gb_ref[i, rows, h:h + 1]
        st[p] = dict(
            decay=jnp.exp(jnp.where(tril, gcol - grow, -jnp.inf)),
            aq=_dot_nt(jnp.concatenate([kb, q], axis=0).astype(BF16), k.astype(BF16)),
            sol=jnp.concatenate([v_ref[i, rows, hs] * gb_ref[i, rows, h:h + 1], kb * egc], axis=-1),
            qd=(q * egc).astype(BF16),
            k_dec=(k * jnp.exp(glast - gcol)).astype(BF16),
            e_last=jnp.exp(glast))
    for p in probs:
        d = st[p]
        d["pb"] = jnp.where(strict, d["aq"][:c] * d["decay"], 0.0).astype(BF16)
        d["qk"] = (d["aq"][c:] * d["decay"]).astype(BF16)
        d["sol"] = d["sol"] - _dot_split_rhs(d["pb"], d["sol"])
    for _ in range(n_sq):
        for p in probs:
            st[p]["pb"] = _dot(st[p]["pb"], st[p]["pb"]).astype(BF16)
        for p in probs:
            st[p]["sol"] = st[p]["sol"] + _dot_split_rhs(st[p]["pb"], st[p]["sol"])
    for p in probs:
        d = st[p]
        d["wq"] = jnp.concatenate([d["sol"][:, GDN_DV:].astype(BF16), d["qd"]], axis=0)
        d["u"] = d["sol"][:, :GDN_DV]

    for ic in range(nc):
        rows = slice(ic * c, (ic + 1) * c)
        seqs = [(i, h) for i in range(bb) for h in range(GDN_H)]
        s_old = {ih: s_scr[ih[0], ih[1]] for ih in seqs}
        ws = {ih: _dot(st[ih[0], ic, ih[1]]["wq"], s_old[ih].astype(BF16)) for ih in seqs}
        vb = {ih: (st[ih[0], ic, ih[1]]["u"] - ws[ih][:c]).astype(BF16) for ih in seqs}
        for ih in seqs:
            d = st[ih[0], ic, ih[1]]
            s_scr[ih[0], ih[1]] = s_old[ih] * d["e_last"] + _dot_tn(d["k_dec"], vb[ih])
        for ih in seqs:
            i, h = ih
            hs = slice(h * GDN_DK, (h + 1) * GDN_DK)
            o = ws[ih][c:] + _dot(st[i, ic, h]["qk"], vb[ih])
            o = o * lax.rsqrt(jnp.mean(o * o, -1, keepdims=True) + RMS_EPS)
            o_ref[i, rows, hs] = (o * nw * zg_ref[i, rows, hs]).astype(o_ref.dtype)
    s_o_ref[...] = s_scr[...]


def _gdn_scan(q, k, v, gb, zg, norm_w, s0_all, *, j, layer, c, bb, nc):
    bsz, t_len, _ = q.shape
    tc = nc * c
    tile = lambda n: pl.BlockSpec((bb, tc, n), lambda b, t: (b, t, 0))
    st = pl.BlockSpec((bb, GDN_H, GDN_DK, GDN_DV), lambda b, t: (b, 0, 0, 0))
    st_in = pl.BlockSpec((1, bb, GDN_H, GDN_DK, GDN_DV), lambda b, t: (layer, b, 0, 0, 0))
    return pl.pallas_call(
        functools.partial(_gdn_scan_kernel, bb=bb, nc=nc, c=c),
        grid=(bsz // bb, t_len // tc),
        in_specs=[tile(GDN_DIM), tile(GDN_DIM), tile(GDN_DIM), tile(GB_LANES), tile(GDN_DIM),
                  _layer_block((1, GDN_DV), j), st_in],
        out_specs=[tile(GDN_DIM), st],
        out_shape=[jax.ShapeDtypeStruct((bsz, t_len, GDN_DIM), BF16),
                   jax.ShapeDtypeStruct((bsz, GDN_H, GDN_DK, GDN_DV), F32)],
        scratch_shapes=[pltpu.VMEM((bb, GDN_H, GDN_DK, GDN_DV), F32)],
        compiler_params=_cparams(("arbitrary", "arbitrary")),
        name="gdn_scan",
    )(q, k, v, gb, zg, norm_w, s0_all)


def _out_ln_kernel(a1_ref, a2_ref, w_ref, x_ref, g_ref, b_ref, o_ref):
    n1 = a1_ref.shape[-1]
    mix = (_dot(a1_ref[...].astype(BF16), w_ref[:n1, :])
           + _dot(a2_ref[...].astype(BF16), w_ref[n1:, :]))
    o_ref[...] = _layer_norm(ALPHA * x_ref[...] + mix, g_ref[...], b_ref[...])


def _out_ln(a1, a2, w, x, ln_g, ln_b, *, j, i):
    rows = x.shape[0]
    tr = min(ROW_TILE, rows)
    n1, n2 = a1.shape[1], a2.shape[1]
    tile = lambda n: pl.BlockSpec((tr, n), lambda r: (r, 0))
    return pl.pallas_call(
        _out_ln_kernel,
        grid=(rows // tr,),
        in_specs=[tile(n1), tile(n2), _layer_block((n1 + n2, D_MODEL), j), tile(D_MODEL),
                  _layer_block((1, D_MODEL), i), _layer_block((1, D_MODEL), i)],
        out_specs=tile(D_MODEL),
        out_shape=jax.ShapeDtypeStruct((rows, D_MODEL), F32),
        compiler_params=_cparams(("arbitrary",)),
        name="out_ln",
    )(a1, a2, w, x, ln_g, ln_b)


def _mix_a_front_sample_kernel(x_ref, hist_ref, win_ref, wc_ref, y_o, qm_o, hist_o, *, t_len, bsz):
    h = _dot(x_ref[...].astype(BF16), win_ref[...])
    u = h[:, 2 * SC_DIM:3 * SC_DIM] * h[:, :SC_DIM]
    slabs = [hist_ref[:, j, :] for j in range(SC_W - 1)]
    slabs += [u[t * bsz:(t + 1) * bsz] for t in range(t_len)]
    wc = wc_ref[...]
    for t in range(t_len):
        conv = wc[0:1] * slabs[t]
        for j in range(1, SC_W):
            conv = conv + wc[j:j + 1] * slabs[t + j]
        y_o[t * bsz:(t + 1) * bsz, :] = (h[t * bsz:(t + 1) * bsz, SC_DIM:2 * SC_DIM] * conv).astype(BF16)
    for j in range(SC_W - 1):
        hist_o[:, j, :] = slabs[t_len + j]
    qm_o[...] = h[:, 3 * SC_DIM:]


def _whole(shape):
    nd = len(shape)
    return pl.BlockSpec(tuple(shape), lambda *_: (0,) * nd)


def _mix_a_front_sample(x2d, hist, w_in, w_conv, *, j, t_len, bsz):
    rows = x2d.shape[0]
    out_shape = [jax.ShapeDtypeStruct((rows, SC_DIM), BF16),
                 jax.ShapeDtypeStruct((rows, XDIM), F32),
                 jax.ShapeDtypeStruct((bsz, SC_W - 1, SC_DIM), F32)]
    return pl.pallas_call(
        functools.partial(_mix_a_front_sample_kernel, t_len=t_len, bsz=bsz),
        grid=(1,),
        in_specs=[_whole(x2d.shape), _layer_block(hist.shape[1:], j), _layer_block(w_in.shape[1:], j),
                  _layer_block(w_conv.shape[1:], j)],
        out_specs=[_whole(s.shape) for s in out_shape],
        out_shape=out_shape,
        compiler_params=_cparams(("arbitrary",)),
        name="mix_a_front_sample",
    )(x2d, hist, w_in, w_conv)


def _proj_b_front_sample_kernel(x_ref, hist_ref, win_ref, wc_ref, arow_ref, dtrow_ref,
                                q_o, k_o, v_o, gb_o, zg_o, qm_o, hist_o, *, t_len, bsz):
    nqkv = 3 * GDN_DIM
    h = _dot(x_ref[...].astype(BF16), win_ref[...])
    slabs = [hist_ref[j] for j in range(GDN_CONV_W - 1)]
    slabs += [h[t * bsz:(t + 1) * bsz, :nqkv] for t in range(t_len)]
    wc = wc_ref[...]
    for t in range(t_len):
        conv = wc[0:1] * slabs[t]
        for j in range(1, GDN_CONV_W):
            conv = conv + wc[j:j + 1] * slabs[t + j]
        _qkv_post(conv, q_o, k_o, v_o, (t, slice(None)))
    for j in range(GDN_CONV_W - 1):
        hist_o[j] = slabs[t_len + j]
    zg_o[...] = _silu(h[:, nqkv:nqkv + GDN_DIM])
    qm_o[...] = h[:, nqkv + GDN_DIM:nqkv + GDN_DIM + XDIM]
    gb_o[...] = _gdn_gates(h[:, nqkv + GDN_DIM + XDIM:], arow_ref[...], dtrow_ref[...])


def _proj_b_front_sample(x2d, hist, w_in, w_conv, a_row, dt_row, *, j, t_len, bsz):
    rows = x2d.shape[0]
    tmaj = jax.ShapeDtypeStruct((t_len, bsz, GDN_DIM), F32)
    out_shape = [tmaj, tmaj, tmaj,
                 jax.ShapeDtypeStruct((rows, GB_LANES), F32),
                 jax.ShapeDtypeStruct((rows, GDN_DIM), F32),
                 jax.ShapeDtypeStruct((rows, XDIM), F32),
                 jax.ShapeDtypeStruct((GDN_CONV_W - 1, bsz, 3 * GDN_DIM), F32)]
    return pl.pallas_call(
        functools.partial(_proj_b_front_sample_kernel, t_len=t_len, bsz=bsz),
        grid=(1,),
        in_specs=[_whole(x2d.shape), _layer_block(hist.shape[1:], j), _layer_block(w_in.shape[1:], j),
                  _layer_block(w_conv.shape[1:], j), _layer_block((1, GB_LANES), j),
                  _layer_block((1, GB_LANES), j)],
        out_specs=[_whole(s.shape) for s in out_shape],
        out_shape=out_shape,
        compiler_params=_cparams(("arbitrary",)),
        name="proj_b_front_sample",
    )(x2d, hist, w_in, w_conv, a_row, dt_row)


def _ffn_sample_kernel(x_ref, hist_ref, w_ref, c_ref, wdn_ref, g_ref, b_ref, o_ref, hist_o,
                       acc, gate, *, t_len, bsz, nj):
    s = pl.program_id(0)
    h = _dot(x_ref[...].astype(BF16), w_ref[...])
    slabs = [hist_ref[:, r, :] for r in range(FFN_W - 1)]
    slabs += [h[t * bsz:(t + 1) * bsz] for t in range(t_len)]
    wc = c_ref[...]
    outs = []
    for t in range(t_len):
        c = wc[0:1] * slabs[t]
        for w in range(1, FFN_W):
            c = c + wc[w:w + 1] * slabs[t + w]
        outs.append(c)
    for r in range(FFN_W - 1):
        hist_o[:, r, :] = slabs[t_len + r]
    conv = jnp.concatenate(outs, axis=0)

    @pl.when(s < nj)
    def _():
        gate[s] = _silu(conv)

    @pl.when(s >= nj)
    def _():
        d = _dot((gate[s - nj] * conv).astype(BF16), wdn_ref[...])

        @pl.when(s == nj)
        def _():
            acc[...] = d

        @pl.when(s > nj)
        def _():
            acc[...] += d

    @pl.when(s == 2 * nj - 1)
    def _():
        o_ref[...] = _layer_norm(ALPHA * x_ref[...] + acc[...], g_ref[...], b_ref[...])


def _ffn_sample(x2d, hist, w_up, w_conv, w_down, ln_g, ln_b, *, i, t_len, bsz):
    rows = x2d.shape[0]
    tf = FF_TILE_SAMPLE
    nj = D_FF // tf
    return pl.pallas_call(
        functools.partial(_ffn_sample_kernel, t_len=t_len, bsz=bsz, nj=nj),
        grid=(2 * nj,),
        in_specs=[_resident((rows, D_MODEL)),
                  pl.BlockSpec((None, bsz, FFN_W - 1, tf), lambda s: (i, 0, 0, s)),
                  pl.BlockSpec((None, D_MODEL, tf), lambda s: (i, 0, s)),
                  pl.BlockSpec((None, FFN_W, tf), lambda s: (i, 0, s)),
                  pl.BlockSpec((None, tf, D_MODEL), lambda s: (i, jnp.maximum(s - nj, 0), 0)),
                  _layer_block((1, D_MODEL), i), _layer_block((1, D_MODEL), i)],
        out_specs=[pl.BlockSpec((rows, D_MODEL), lambda s: (0, 0)),
                   pl.BlockSpec((bsz, FFN_W - 1, tf), lambda s: (0, 0, s))],
        out_shape=[jax.ShapeDtypeStruct((rows, D_MODEL), F32),
                   jax.ShapeDtypeStruct((bsz, FFN_W - 1, 2 * D_FF), F32)],
        scratch_shapes=[pltpu.VMEM((rows, D_MODEL), F32), pltpu.VMEM((nj, rows, tf), F32)],
        compiler_params=_cparams(("arbitrary",)),
        name="ffn_sample",
    )(x2d, hist, w_up, w_conv, w_down, ln_g, ln_b)


def _prep_weights(w_in_a, w_out_a, w_in_b, a_log, dt_bias, w_out_b, w_mem_kv, w_up, w_down):
    nqkvz = 4 * GDN_DIM
    w_b = jnp.concatenate(
        [w_in_b[..., :nqkvz], w_in_b[..., nqkvz + 2 * GDN_H:],
         jnp.pad(w_in_b[..., nqkvz:nqkvz + 2 * GDN_H], ((0, 0), (0, 0), (0, GB_LANES - 2 * GDN_H)))],
        axis=-1).astype(BF16)
    pad = GB_LANES - 2 * GDN_H
    a_row = jnp.pad(-jnp.exp(a_log.astype(F32)), ((0, 0), (GDN_H, pad)))[:, None, :]
    dt_row = jnp.pad(dt_bias.astype(F32), ((0, 0), (GDN_H, pad)))[:, None, :]
    return dict(w_in_a=w_in_a.astype(BF16), w_out_a=w_out_a.astype(BF16), w_in_b=w_b,
                w_out_b=w_out_b.astype(BF16), w_mem_kv=w_mem_kv.astype(BF16),
                w_up=w_up.astype(BF16), w_down=w_down.astype(BF16), a_row=a_row, dt_row=dt_row)


def _trunk_prompt(x, mem_k, mem_v, wts, conv_a, conv_b, gdn_norm_w, ln1_g, ln1_b, ln2_g, ln2_b,
                  w_conv_ffn):
    bsz, t_len, _ = x.shape
    z = lambda *s: jnp.zeros(s, F32)
    new_sc, new_gc, new_gs, new_ffn = [], [], [], []
    for i in range(DEPTH):
        j = i // 2
        if i % 2 == 0:
            x, hs = _mix_a_seq(x, z(bsz, SC_W - 1, SC_DIM), wts["w_in_a"], conv_a, wts["w_out_a"],
                               mem_k, mem_v, ln1_g, ln1_b, j=j, i=i)
            new_sc.append(hs)
        else:
            q, k, v, gb, zg, om, hg = _proj_b_seq(
                x, z(bsz, GDN_CONV_W - 1, 3 * GDN_DIM), wts["w_in_b"], conv_b,
                wts["a_row"], wts["dt_row"], mem_k, mem_v, j=j, i=i)
            o, sg = _gdn_scan(q, k, v, gb, zg, gdn_norm_w, z(1, bsz, GDN_H, GDN_DK, GDN_DV),
                              j=j, layer=0, c=GDN_CHUNK, bb=SCAN_SEQS, nc=SCAN_CHUNKS)
            rows = bsz * t_len
            x = _out_ln(o.reshape(rows, GDN_DIM), om.reshape(rows, XDIM), wts["w_out_b"],
                        x.reshape(rows, D_MODEL), ln1_g, ln1_b, j=j, i=i).reshape(bsz, t_len, D_MODEL)
            new_gc.append(hg)
            new_gs.append(sg)
        x, hf = _ffn_seq(x, z(bsz, FFN_W - 1, 2 * D_FF), wts["w_up"], w_conv_ffn, wts["w_down"],
                         ln2_g, ln2_b, i=i)
        new_ffn.append(hf)
    return x, jnp.stack(new_sc), jnp.stack(new_gc), jnp.stack(new_gs), jnp.stack(new_ffn)


def _trunk_sample(x, mem_k, mem_v, sc_hist, gdn_hist, gdn_s, ffn_hist, wts, conv_a, conv_b,
                  gdn_norm_w, ln1_g, ln1_b, ln2_g, ln2_b, w_conv_ffn):
    bsz, t_len, _ = x.shape
    rows = bsz * t_len
    t_pad = SUBLANES
    x2 = jnp.transpose(x, (1, 0, 2)).reshape(rows, D_MODEL)
    to_bm = lambda a: jnp.transpose(a, (1, 0, 2))
    new_sc, new_gc, new_gs, new_ffn = [], [], [], []
    mem_kt = jnp.transpose(mem_k, (0, 1, 3, 4, 2))
    mem_vt = jnp.transpose(mem_v, (0, 1, 3, 4, 2))
    gdn_hist = jnp.transpose(gdn_hist, (0, 2, 1, 3))
    for i in range(DEPTH):
        j = i // 2
        if i % 2 == 0:
            y, qm, hs = _mix_a_front_sample(x2, sc_hist, wts["w_in_a"], conv_a, j=j,
                                            t_len=t_len, bsz=bsz)
            om = _attn_sample(qm.reshape(t_len, bsz, XDIM), mem_kt, mem_vt, i).reshape(rows, XDIM)
            x2 = _out_ln(y, om, wts["w_out_a"], x2, ln1_g, ln1_b, j=j, i=i)
            new_sc.append(hs)
        else:
            q, k, v, gb, zg, qm, hg = _proj_b_front_sample(
                x2, gdn_hist, wts["w_in_b"], conv_b, wts["a_row"], wts["dt_row"], j=j,
                t_len=t_len, bsz=bsz)
            om = _attn_sample(qm.reshape(t_len, bsz, XDIM), mem_kt, mem_vt, i).reshape(rows, XDIM)
            padt = lambda a: jnp.pad(to_bm(a.reshape(t_len, bsz, -1)), ((0, 0), (0, t_pad - t_len), (0, 0)))
            o, sg = _gdn_scan(padt(q), padt(k), padt(v), padt(gb), padt(zg), gdn_norm_w, gdn_s,
                              j=j, layer=j, c=t_pad, bb=SCAN_SEQS_SAMPLE, nc=1)
            o = jnp.transpose(o[:, :t_len], (1, 0, 2)).reshape(rows, GDN_DIM)
            x2 = _out_ln(o, om, wts["w_out_b"], x2, ln1_g, ln1_b, j=j, i=i)
            new_gc.append(hg)
            new_gs.append(sg)
        x2, hf = _ffn_sample(x2, ffn_hist, wts["w_up"], w_conv_ffn, wts["w_down"],
                             ln2_g, ln2_b, i=i, t_len=t_len, bsz=bsz)
        new_ffn.append(hf)
    y = jnp.transpose(x2.reshape(t_len, bsz, D_MODEL), (1, 0, 2))
    gc = jnp.transpose(jnp.stack(new_gc), (0, 2, 1, 3))
    return y, jnp.stack(new_sc), gc, jnp.stack(new_gs), jnp.stack(new_ffn)


def kernel(x_prompt, x_sample, mem_prompt, cache_mem_k, cache_mem_v, state_shortconv, state_gdn_conv,
           state_gdn, state_ffn_conv, w_in_a, conv_a, w_out_a, w_in_b, conv_b, a_log, dt_bias,
           gdn_norm_w, w_out_b, w_mem_kv, ln1_g, ln1_b, ln2_g, ln2_b, w_up, w_conv_ffn, w_down):
    wts = _prep_weights(w_in_a, w_out_a, w_in_b, a_log, dt_bias, w_out_b, w_mem_kv, w_up, w_down)
    row3 = lambda a: a.reshape(a.shape[0], 1, a.shape[1])
    shared = (wts, conv_a, conv_b, row3(gdn_norm_w), row3(ln1_g), row3(ln1_b), row3(ln2_g),
              row3(ln2_b), w_conv_ffn)
    bsz = x_prompt.shape[0]
    k2, v2 = _mem_kv(mem_prompt.reshape(bsz * N_MEM, D_MODEL), wts["w_mem_kv"])
    mem_k_prompt = k2.reshape(DEPTH, bsz, N_MEM, XDIM)
    mem_v_prompt = v2.reshape(DEPTH, bsz, N_MEM, XDIM)
    y_prompt, sc_p, gc_p, gs_p, ffn_p = _trunk_prompt(x_prompt, mem_k_prompt, mem_v_prompt, *shared)
    y_sample, sc_s, gc_s, gs_s, ffn_s = _trunk_sample(
        x_sample, cache_mem_k, cache_mem_v, state_shortconv, state_gdn_conv, state_gdn,
        state_ffn_conv, *shared)
    shape5 = (DEPTH, bsz, N_MEM, XH, XD)
    return (y_prompt, y_sample, mem_k_prompt.reshape(shape5), mem_v_prompt.reshape(shape5),
            sc_p, gc_p, gs_p, ffn_p, sc_s, gc_s, gs_s, ffn_s)
```

```python
import functools

import jax
import jax.numpy as jnp
from jax import lax
from jax.experimental import pallas as pl
from jax.experimental.pallas import tpu as pltpu

F32 = jnp.float32
BF16 = jnp.bfloat16

DEPTH = 4
D_MODEL = 1024
SC_DIM = 768
SC_W = 3
GDN_H = 6
GDN_DK = 128
GDN_DV = 128
GDN_DIM = GDN_H * GDN_DK
GDN_CONV_W = 4
GDN_CHUNK = 64
N_MEM = 256
XH = 4
XD = 64
XDIM = XH * XD
D_FF = 2816
FFN_W = 3
ALPHA = (2.0 * DEPTH) ** 0.25
LN_EPS = 1e-5
RMS_EPS = 1e-6

V7X_VMEM_BYTES = 64 * 1024 * 1024
VMEM_LIMIT = V7X_VMEM_BYTES - 8 * 1024 * 1024
SUBLANES = 8
LANES = 128

GB_LANES = LANES

SEQ_TILE = 512
MIX_TILE = 512
MIX_SUBTILES = 2
SCAN_SEQS = 2
SCAN_CHUNKS = 4
SCAN_SEQS_SAMPLE = 4
SOLVE_BLOCK = 16
ROW_TILE = 512
FF_TILE = 2816
FF_TILE_SAMPLE = 1408
ATTN_BATCH_BLOCK = 16
ATTN_UNROLL = 4
W_B_COLS = 3 * GDN_DIM + GDN_DIM + XDIM + GB_LANES


def _cparams(sem):
    return pltpu.CompilerParams(dimension_semantics=sem, vmem_limit_bytes=VMEM_LIMIT)


def _resident(shape):
    nd = len(shape)
    return pl.BlockSpec(shape, lambda *_: (0,) * nd, pipeline_mode=pl.Buffered(1))


def _layer_block(shape, layer):
    nd = len(shape)
    return pl.BlockSpec((None,) + tuple(shape), lambda *_: (layer,) + (0,) * nd,
                        pipeline_mode=pl.Buffered(1))


def _silu(x):
    return x * (1.0 / (1.0 + jnp.exp(-x)))


def _sigmoid(x):
    return 1.0 / (1.0 + jnp.exp(-x))


def _softplus(x):
    return jnp.maximum(x, 0.0) + jnp.log(1.0 + jnp.exp(-jnp.abs(x)))


def _layer_norm(v, g, b):
    mu = jnp.mean(v, -1, keepdims=True)
    d = v - mu
    var = jnp.mean(d * d, -1, keepdims=True)
    return d * lax.rsqrt(var + LN_EPS) * g + b


def _dot(a, b):
    return jnp.dot(a, b, preferred_element_type=F32)


def _dot_nt(a, b):
    return lax.dot_general(a, b, (((1,), (1,)), ((), ())), preferred_element_type=F32)


def _dot_tn(a, b):
    return lax.dot_general(a, b, (((0,), (0,)), ((), ())), preferred_element_type=F32)


def _dot_hi(a, b):
    return jnp.dot(a, b, preferred_element_type=F32, precision=lax.Precision.HIGHEST)


def _dot_nt_hi(a, b):
    return lax.dot_general(a, b, (((1,), (1,)), ((), ())), preferred_element_type=F32,
                           precision=lax.Precision.HIGHEST)


def _kv_kernel(m_ref, w_ref, k_ref, v_ref):
    kv = _dot(m_ref[...].astype(BF16), w_ref[0])
    k_ref[0] = kv[:, :XDIM]
    v_ref[0] = kv[:, XDIM:]


def _mem_kv(mem2d, w_kv):
    rows = mem2d.shape[0]
    out = jax.ShapeDtypeStruct((DEPTH, rows, XDIM), F32)
    return pl.pallas_call(
        _kv_kernel,
        grid=(DEPTH,),
        in_specs=[_resident((rows, D_MODEL)),
                  pl.BlockSpec((1, D_MODEL, 2 * XDIM), lambda l: (l, 0, 0))],
        out_specs=[pl.BlockSpec((1, rows, XDIM), lambda l: (l, 0, 0))] * 2,
        out_shape=[out, out],
        compiler_params=_cparams(("arbitrary",)),
        name="mem_kv",
    )(mem2d, w_kv)


def _head_blockdiag(kv):
    lane_head = lax.broadcasted_iota(jnp.int32, kv.shape, 1) // XD
    return jnp.concatenate(
        [jnp.where(lane_head == h, kv, 0.0).astype(BF16) for h in range(XH)], axis=0)


def _softmax_rows(s):
    m = jnp.max(s, -1, keepdims=True)
    e = jnp.exp(s - m)
    return e / jnp.sum(e, -1, keepdims=True)


def _attn_sample_kernel(q_ref, k_ref, v_ref, o_ref, *, bb, t_len):
    def body(g, carry):
        elems = [g * ATTN_UNROLL + e for e in range(ATTN_UNROLL)]
        q8 = []
        for i in elems:
            rows = [q_ref[t, pl.ds(i, 1), :] for t in range(t_len)]
            rows.append(jnp.zeros((SUBLANES - t_len, XDIM), F32))
            q8.append(jnp.concatenate(rows, axis=0).astype(BF16))
        s = [[_dot(q8[e][:, h * XD:(h + 1) * XD], k_ref[0, i, h].astype(BF16)) * (XD ** -0.5)
              for h in range(XH)] for e, i in enumerate(elems)]
        p = [[_softmax_rows(sh).astype(BF16) for sh in se] for se in s]
        for e, i in enumerate(elems):
            o8 = jnp.concatenate([_dot_nt(p[e][h], v_ref[0, i, h].astype(BF16)) for h in range(XH)],
                                 axis=-1)
            for t in range(t_len):
                o_ref[t, pl.ds(i, 1), :] = o8[t:t + 1]
        return carry

    lax.fori_loop(0, bb // ATTN_UNROLL, body, 0)


def _attn_sample(qm_tm, mem_kt, mem_vt, layer):
    t_len, bsz, _ = qm_tm.shape
    bb = min(ATTN_BATCH_BLOCK, bsz)
    kv = pl.BlockSpec((1, bb, XH, XD, N_MEM), lambda i: (layer, i, 0, 0, 0))
    return pl.pallas_call(
        functools.partial(_attn_sample_kernel, bb=bb, t_len=t_len),
        grid=(bsz // bb,),
        in_specs=[pl.BlockSpec((t_len, bb, XDIM), lambda i: (0, i, 0)), kv, kv],
        out_specs=pl.BlockSpec((t_len, bb, XDIM), lambda i: (0, i, 0)),
        out_shape=jax.ShapeDtypeStruct((t_len, bsz, XDIM), F32),
        compiler_params=_cparams(("arbitrary",)),
        name="attn_sample",
    )(qm_tm, mem_kt, mem_vt)


def _mix_a_seq_kernel(x_ref, hist_ref, win_ref, wc_ref, wout_ref, k_ref, v_ref, g_ref, b_ref,
                      o_ref, hist_o_ref, ubuf, kbd, vbd, *, tm):
    t = pl.program_id(1)
    lo = SUBLANES - (SC_W - 1)

    @pl.when(t == 0)
    def _():
        ubuf[lo:SUBLANES, :] = hist_ref[0]
        kbd[...] = _head_blockdiag(k_ref[0])
        vbd[...] = _head_blockdiag(v_ref[0])

    ts = tm // MIX_SUBTILES
    subs = range(MIX_SUBTILES)
    rows = [slice(s * ts, (s + 1) * ts) for s in subs]
    hs = [_dot(x_ref[0, rows[s], :].astype(BF16), win_ref[...]) for s in subs]
    us = [hs[s][:, 2 * SC_DIM:3 * SC_DIM] * hs[s][:, :SC_DIM] for s in subs]
    for s in subs:
        ubuf[SUBLANES + s * ts:SUBLANES + (s + 1) * ts, :] = us[s]
    wc = wc_ref[...]
    ys = []
    for s in subs:
        conv = wc[SC_W - 1:SC_W] * us[s]
        for j in range(SC_W - 1):
            conv = conv + wc[j:j + 1] * ubuf[lo + j + s * ts:lo + j + (s + 1) * ts, :]
        ys.append((hs[s][:, SC_DIM:2 * SC_DIM] * conv).astype(BF16))
    last = ubuf[tm + lo:tm + SUBLANES, :]
    ubuf[lo:SUBLANES, :] = last
    hist_o_ref[0] = last

    sc = [_dot_nt(hs[s][:, 3 * SC_DIM:].astype(BF16), kbd[...]) * (XD ** -0.5) for s in subs]
    ps = [jnp.concatenate([_softmax_rows(sc[s][:, h * N_MEM:(h + 1) * N_MEM]).astype(BF16)
                           for h in range(XH)], axis=-1) for s in subs]
    oms = [_dot(ps[s], vbd[...]).astype(BF16) for s in subs]
    mix = [_dot(ys[s], wout_ref[:SC_DIM, :]) + _dot(oms[s], wout_ref[SC_DIM:, :]) for s in subs]
    for s in subs:
        o_ref[0, rows[s], :] = _layer_norm(ALPHA * x_ref[0, rows[s], :] + mix[s], g_ref[...], b_ref[...])


def _mix_a_seq(x, hist, w_in, w_conv, w_out, mem_k, mem_v, ln_g, ln_b, *, j, i):
    bsz, t_len, _ = x.shape
    tm = min(MIX_TILE, t_len)
    mem = pl.BlockSpec((None, 1, N_MEM, XDIM), lambda b, t: (i, b, 0, 0))
    return pl.pallas_call(
        functools.partial(_mix_a_seq_kernel, tm=tm),
        grid=(bsz, t_len // tm),
        in_specs=[pl.BlockSpec((1, tm, D_MODEL), lambda b, t: (b, t, 0)),
                  pl.BlockSpec((1, SC_W - 1, SC_DIM), lambda b, t: (b, 0, 0)),
                  _layer_block(w_in.shape[1:], j),
                  _layer_block((SC_W, SC_DIM), j),
                  _layer_block((SC_DIM + XDIM, D_MODEL), j),
                  mem, mem,
                  _layer_block((1, D_MODEL), i),
                  _layer_block((1, D_MODEL), i)],
        out_specs=[pl.BlockSpec((1, tm, D_MODEL), lambda b, t: (b, t, 0)),
                   pl.BlockSpec((1, SC_W - 1, SC_DIM), lambda b, t: (b, 0, 0))],
        out_shape=[jax.ShapeDtypeStruct((bsz, t_len, D_MODEL), F32),
                   jax.ShapeDtypeStruct((bsz, SC_W - 1, SC_DIM), F32)],
        scratch_shapes=[pltpu.VMEM((tm + SUBLANES, SC_DIM), F32),
                        pltpu.VMEM((XH * N_MEM, XDIM), BF16),
                        pltpu.VMEM((XH * N_MEM, XDIM), BF16)],
        compiler_params=_cparams(("arbitrary", "arbitrary")),
        name="mix_a_seq",
    )(x, hist, w_in, w_conv, w_out, mem_k, mem_v, ln_g, ln_b)


def _ffn_seq_kernel(x_ref, hist_ref, wup_ref, wc_ref, wdn_ref, g_ref, b_ref,
                    o_ref, hist_o_ref, gbuf, ubuf, *, tm, tf):
    t = pl.program_id(1)
    lo = SUBLANES - (FFN_W - 1)
    nj = D_FF // tf
    halves = ((gbuf, 0), (ubuf, D_FF))

    @pl.when(t == 0)
    def _():
        for buf, off in halves:
            for j in range(nj):
                buf[j, lo:SUBLANES, :] = hist_ref[0, :, off + j * tf:off + (j + 1) * tf]

    x = x_ref[0]
    xb = x.astype(BF16)

    def up_project(j):
        return [_dot(xb, wup_ref[:, off + j * tf:off + (j + 1) * tf]) for _, off in halves]

    def conv_act(j, hs):
        conv = []
        for (buf, off), h in zip(halves, hs):
            c0 = off + j * tf
            buf[j, SUBLANES:SUBLANES + tm, :] = h
            c = wc_ref[FFN_W - 1:FFN_W, c0:c0 + tf] * h
            for w in range(FFN_W - 1):
                c = c + wc_ref[w:w + 1, c0:c0 + tf] * buf[j, lo + w:lo + w + tm, :]
            last = buf[j, tm + lo:tm + SUBLANES, :]
            buf[j, lo:SUBLANES, :] = last
            hist_o_ref[0, :, c0:c0 + tf] = last
            conv.append(c)
        return (_silu(conv[0]) * conv[1]).astype(BF16)

    acc = None
    hs = up_project(0)
    for j in range(nj):
        hs_next = up_project(j + 1) if j + 1 < nj else None
        d = _dot(conv_act(j, hs), wdn_ref[j * tf:(j + 1) * tf, :])
        acc = d if acc is None else acc + d
        hs = hs_next
    o_ref[0] = _layer_norm(ALPHA * x + acc, g_ref[...], b_ref[...])


def _ffn_seq(x, hist, w_up, w_conv, w_down, ln_g, ln_b, *, i):
    bsz, t_len, _ = x.shape
    tm = min(SEQ_TILE, t_len)
    tf = FF_TILE
    return pl.pallas_call(
        functools.partial(_ffn_seq_kernel, tm=tm, tf=tf),
        grid=(bsz, t_len // tm),
        in_specs=[pl.BlockSpec((1, tm, D_MODEL), lambda b, t: (b, t, 0)),
                  pl.BlockSpec((1, FFN_W - 1, 2 * D_FF), lambda b, t: (b, 0, 0)),
                  _layer_block((D_MODEL, 2 * D_FF), i),
                  _layer_block((FFN_W, 2 * D_FF), i),
                  _layer_block((D_FF, D_MODEL), i),
                  _layer_block((1, D_MODEL), i),
                  _layer_block((1, D_MODEL), i)],
        out_specs=[pl.BlockSpec((1, tm, D_MODEL), lambda b, t: (b, t, 0)),
                   pl.BlockSpec((1, FFN_W - 1, 2 * D_FF), lambda b, t: (b, 0, 0))],
        out_shape=[jax.ShapeDtypeStruct((bsz, t_len, D_MODEL), F32),
                   jax.ShapeDtypeStruct((bsz, FFN_W - 1, 2 * D_FF), F32)],
        scratch_shapes=[pltpu.VMEM((D_FF // tf, tm + SUBLANES, tf), F32),
                        pltpu.VMEM((D_FF // tf, tm + SUBLANES, tf), F32)],
        compiler_params=_cparams(("arbitrary", "arbitrary")),
        name="ffn_seq",
    )(x, hist, w_up, w_conv, w_down, ln_g, ln_b)


def _gdn_gates(ba, a_row, dt_row):
    lane = lax.broadcasted_iota(jnp.int32, ba.shape, 1)
    return jnp.where(lane < GDN_H, _sigmoid(ba), a_row * _softplus(ba + dt_row))


def _qkv_post(c, q_ref, k_ref, v_ref, idx):
    c = _silu(c)
    for h in range(GDN_H):
        for ref, base in ((q_ref, 0), (k_ref, GDN_DIM)):
            a = c[:, base + h * GDN_DK:base + (h + 1) * GDN_DK]
            a = a * lax.rsqrt(jnp.sum(a * a, -1, keepdims=True) + RMS_EPS)
            ref[idx + (slice(h * GDN_DK, (h + 1) * GDN_DK),)] = a
    v_ref[idx + (slice(None),)] = c[:, 2 * GDN_DIM:]


def _proj_b_seq_kernel(x_ref, hist_ref, win_ref, wc_ref, arow_ref, dtrow_ref, k_ref, v_ref,
                       q_o, k_o, v_o, gb_o, zg_o, om_o, hist_o_ref, sbuf, kbd, vbd, *, tm):
    t = pl.program_id(1)
    lo = SUBLANES - (GDN_CONV_W - 1)
    nqkv = 3 * GDN_DIM

    @pl.when(t == 0)
    def _():
        sbuf[lo:SUBLANES, :] = hist_ref[0]
        kbd[...] = _head_blockdiag(k_ref[0])
        vbd[...] = _head_blockdiag(v_ref[0])

    ts = tm // MIX_SUBTILES
    subs = range(MIX_SUBTILES)
    rows = [slice(s * ts, (s + 1) * ts) for s in subs]
    hs = [_dot(x_ref[0, rows[s], :].astype(BF16), win_ref[...]) for s in subs]
    for s in subs:
        sbuf[SUBLANES + s * ts:SUBLANES + (s + 1) * ts, :] = hs[s][:, :nqkv]
    wc = wc_ref[...]
    qm0 = nqkv + GDN_DIM
    sc = []
    for s in subs:
        conv = wc[GDN_CONV_W - 1:GDN_CONV_W] * hs[s][:, :nqkv]
        for j in range(GDN_CONV_W - 1):
            conv = conv + wc[j:j + 1] * sbuf[lo + j + s * ts:lo + j + (s + 1) * ts, :]
        sc.append(_dot_nt(hs[s][:, qm0:qm0 + XDIM].astype(BF16), kbd[...]) * (XD ** -0.5))
        _qkv_post(conv, q_o, k_o, v_o, (0, rows[s]))
        zg_o[0, rows[s], :] = _silu(hs[s][:, nqkv:qm0])
        gb_o[0, rows[s], :] = _gdn_gates(hs[s][:, qm0 + XDIM:], arow_ref[...], dtrow_ref[...])
    last = sbuf[tm + lo:tm + SUBLANES, :]
    sbuf[lo:SUBLANES, :] = last
    hist_o_ref[0] = last
    ps = [jnp.concatenate([_softmax_rows(sc[s][:, h * N_MEM:(h + 1) * N_MEM]).astype(BF16)
                           for h in range(XH)], axis=-1) for s in subs]
    for s in subs:
        om_o[0, rows[s], :] = _dot(ps[s], vbd[...]).astype(BF16)


def _proj_b_seq(x, hist, w_in, w_conv, a_row, dt_row, mem_k, mem_v, *, j, i):
    bsz, t_len, _ = x.shape
    tm = min(MIX_TILE, t_len)
    tile = lambda n: pl.BlockSpec((1, tm, n), lambda b, t: (b, t, 0))
    f32o = lambda n: jax.ShapeDtypeStruct((bsz, t_len, n), F32)
    mem = pl.BlockSpec((None, 1, N_MEM, XDIM), lambda b, t: (i, b, 0, 0))
    return pl.pallas_call(
        functools.partial(_proj_b_seq_kernel, tm=tm),
        grid=(bsz, t_len // tm),
        in_specs=[tile(D_MODEL),
                  pl.BlockSpec((1, GDN_CONV_W - 1, 3 * GDN_DIM), lambda b, t: (b, 0, 0)),
                  _layer_block((D_MODEL, W_B_COLS), j),
                  _layer_block((GDN_CONV_W, 3 * GDN_DIM), j),
                  _layer_block((1, GB_LANES), j),
                  _layer_block((1, GB_LANES), j),
                  mem, mem],
        out_specs=[tile(GDN_DIM), tile(GDN_DIM), tile(GDN_DIM), tile(GB_LANES), tile(GDN_DIM),
                   tile(XDIM),
                   pl.BlockSpec((1, GDN_CONV_W - 1, 3 * GDN_DIM), lambda b, t: (b, 0, 0))],
        out_shape=[f32o(GDN_DIM), f32o(GDN_DIM), f32o(GDN_DIM), f32o(GB_LANES), f32o(GDN_DIM),
                   jax.ShapeDtypeStruct((bsz, t_len, XDIM), BF16),
                   jax.ShapeDtypeStruct((bsz, GDN_CONV_W - 1, 3 * GDN_DIM), F32)],
        scratch_shapes=[pltpu.VMEM((tm + SUBLANES, 3 * GDN_DIM), F32),
                        pltpu.VMEM((XH * N_MEM, XDIM), BF16),
                        pltpu.VMEM((XH * N_MEM, XDIM), BF16)],
        compiler_params=_cparams(("arbitrary", "arbitrary")),
        name="proj_b_seq",
    )(x, hist, w_in, w_conv, a_row, dt_row, mem_k, mem_v)


def _gdn_scan_kernel(q_ref, k_ref, v_ref, gb_ref, zg_ref, nw_ref, s0_ref, o_ref, s_o_ref, s_scr,
                     *, bb, nc, c):
    t = pl.program_id(1)
    tc = nc * c

    @pl.when(t == 0)
    def _():
        s_scr[...] = s0_ref[0]

    ri = lax.broadcasted_iota(jnp.int32, (c, c), 0)
    ci = lax.broadcasted_iota(jnp.int32, (c, c), 1)
    tril = ri >= ci
    strict = ri > ci
    rt = lax.broadcasted_iota(jnp.int32, (tc, tc), 0)
    ct = lax.broadcasted_iota(jnp.int32, (tc, tc), 1)
    tril_chunks = ((rt >= ct) & ((rt // c) == (ct // c))).astype(F32)
    eye_l = (lax.broadcasted_iota(jnp.int32, (GB_LANES, GB_LANES), 0)
             == lax.broadcasted_iota(jnp.int32, (GB_LANES, GB_LANES), 1)).astype(F32)
    nw = nw_ref[...]

    gcs = []
    for i in range(bb):
        gc_all = _dot_hi(tril_chunks, gb_ref[i])
        gcs.append((gc_all, _dot_nt_hi(eye_l, gc_all)))

    probs = [(i, ic, h) for i in range(bb) for ic in range(nc) for h in range(GDN_H)]
    st = {}
    for p in probs:
        i, ic, h = p
        rows = slice(ic * c, (ic + 1) * c)
        hs = slice(h * GDN_DK, (h + 1) * GDN_DK)
        gc_all, gc_t = gcs[i]
        q = q_ref[i, rows, hs] * (GDN_DK ** -0.5)
        k = k_ref[i, rows, hs]
        gcol = gc_all[rows, GDN_H + h:GDN_H + h + 1]
        grow = gc_t[GDN_H + h:GDN_H + h + 1, rows]
        glast = gc_all[(ic + 1) * c - 1:(ic + 1) * c, GDN_H + h:GDN_H + h + 1]
        egc = jnp.exp(gcol)
        kb = k * gb_ref[i, rows, h:h + 1]
        st[p] = dict(
            decay=jnp.exp(jnp.where(tril, gcol - grow, -jnp.inf)),
            aq=_dot_nt(jnp.concatenate([kb, q], axis=0).astype(BF16), k.astype(BF16)),
            sol=jnp.concatenate([v_ref[i, rows, hs] * gb_ref[i, rows, h:h + 1], kb * egc], axis=-1),
            qd=(q * egc).astype(BF16),
            k_dec=(k * jnp.exp(glast - gcol)).astype(BF16),
            e_last=jnp.exp(glast))
    sb = min(SOLVE_BLOCK, c)
    nblk = c // sb
    same_blk = (ri // sb) == (ci // sb)
    lane_sb = lax.broadcasted_iota(jnp.int32, (sb, c), 1)
    eye_ss = ((lane_sb % sb) == lax.broadcasted_iota(jnp.int32, (sb, c), 0)).astype(F32)
    blk_rows = [slice(n * sb, (n + 1) * sb) for n in range(nblk)]
    tile_rows = lambda a: jnp.concatenate([a] * nblk, axis=0) if nblk > 1 else a

    def split2(x):
        hi = x.astype(BF16)
        return jnp.concatenate([hi, (x - hi.astype(F32)).astype(BF16)], axis=-1)

    def fold2(y):
        n = y.shape[-1] // 2
        return y[:, :n] + y[:, n:]

    for p in probs:
        d = st[p]
        d["lmat"] = jnp.where(strict, d["aq"][:c] * d["decay"], 0.0)
        d["qk"] = (d["aq"][c:] * d["decay"]).astype(BF16)
        d["pss"] = sum(jnp.where(lane_sb // sb == n, d["lmat"][blk_rows[n]], 0.0) for n in range(nblk))
        d["pbd"] = jnp.where(same_blk, d["lmat"], 0.0).astype(BF16)
        d["tss"] = eye_ss - d["pss"]
    for _ in range(max(sb.bit_length() - 2, 0)):
        for p in probs:
            st[p]["pss"] = _dot(st[p]["pss"].astype(BF16), st[p]["pbd"])
        for p in probs:
            d = st[p]
            d["pbd"] = jnp.where(same_blk, tile_rows(d["pss"]), 0.0).astype(BF16)
            d["tss"] = d["tss"] + _dot(d["tss"].astype(BF16), d["pbd"])
    zero_blk = jnp.zeros((sb, 4 * GDN_DV), BF16)
    for p in probs:
        d = st[p]
        d["tbd"] = jnp.where(same_blk, tile_rows(d["tss"]), 0.0).astype(BF16)
        d["x2"] = []
        d["x"] = []
    for n in range(nblk):
        for p in probs:
            d = st[p]
            z = d["sol"][blk_rows[n]]
            if n > 0:
                lrow = jnp.where(lane_sb < n * sb, d["lmat"][blk_rows[n]], 0.0).astype(BF16)
                z = z - fold2(_dot(lrow, jnp.concatenate(d["x2"] + [zero_blk] * (nblk - n), axis=0)))
            d["y"] = jnp.concatenate([zero_blk] * n + [split2(z)] + [zero_blk] * (nblk - n - 1), axis=0)
        for p in probs:
            d = st[p]
            x = fold2(_dot(d["tbd"][blk_rows[n]], d["y"]))
            d["x"].append(x)
            if n + 1 < nblk:
                d["x2"].append(split2(x))
    for p in probs:
        d = st[p]
        sol = jnp.concatenate(d["x"], axis=0) if nblk > 1 else d["x"][0]
        d["wq"] = jnp.concatenate([sol[:, GDN_DV:].astype(BF16), d["qd"]], axis=0)
        d["u"] = sol[:, :GDN_DV]

    for ic in range(nc):
        rows = slice(ic * c, (ic + 1) * c)
        seqs = [(i, h) for i in range(bb) for h in range(GDN_H)]
        s_old = {ih: s_scr[ih[0], ih[1]] for ih in seqs}
        ws = {ih: _dot(st[ih[0], ic, ih[1]]["wq"], s_old[ih].astype(BF16)) for ih in seqs}
        vb = {ih: (st[ih[0], ic, ih[1]]["u"] - ws[ih][:c]).astype(BF16) for ih in seqs}
        for ih in seqs:
            d = st[ih[0], ic, ih[1]]
            s_scr[ih[0], ih[1]] = s_old[ih] * d["e_last"] + _dot_tn(d["k_dec"], vb[ih])
        for ih in seqs:
            i, h = ih
            hs = slice(h * GDN_DK, (h + 1) * GDN_DK)
            o = ws[ih][c:] + _dot(st[i, ic, h]["qk"], vb[ih])
            o = o * lax.rsqrt(jnp.mean(o * o, -1, keepdims=True) + RMS_EPS)
            o_ref[i, rows, hs] = (o * nw * zg_ref[i, rows, hs]).astype(o_ref.dtype)
    s_o_ref[...] = s_scr[...]


def _gdn_scan(q, k, v, gb, zg, norm_w, s0_all, *, j, layer, c, bb, nc):
    bsz, t_len, _ = q.shape
    tc = nc * c
    tile = lambda n: pl.BlockSpec((bb, tc, n), lambda b, t: (b, t, 0))
    st = pl.BlockSpec((bb, GDN_H, GDN_DK, GDN_DV), lambda b, t: (b, 0, 0, 0))
    st_in = pl.BlockSpec((1, bb, GDN_H, GDN_DK, GDN_DV), lambda b, t: (layer, b, 0, 0, 0))
    return pl.pallas_call(
        functools.partial(_gdn_scan_kernel, bb=bb, nc=nc, c=c),
        grid=(bsz // bb, t_len // tc),
        in_specs=[tile(GDN_DIM), tile(GDN_DIM), tile(GDN_DIM), tile(GB_LANES), tile(GDN_DIM),
                  _layer_block((1, GDN_DV), j), st_in],
        out_specs=[tile(GDN_DIM), st],
        out_shape=[jax.ShapeDtypeStruct((bsz, t_len, GDN_DIM), BF16),
                   jax.ShapeDtypeStruct((bsz, GDN_H, GDN_DK, GDN_DV), F32)],
        scratch_shapes=[pltpu.VMEM((bb, GDN_H, GDN_DK, GDN_DV), F32)],
        compiler_params=_cparams(("arbitrary", "arbitrary")),
        name="gdn_scan",
    )(q, k, v, gb, zg, norm_w, s0_all)


def _out_ln_kernel(a1_ref, a2_ref, w_ref, x_ref, g_ref, b_ref, o_ref):
    n1 = a1_ref.shape[-1]
    mix = (_dot(a1_ref[...].astype(BF16), w_ref[:n1, :])
           + _dot(a2_ref[...].astype(BF16), w_ref[n1:, :]))
    o_ref[...] = _layer_norm(ALPHA * x_ref[...] + mix, g_ref[...], b_ref[...])


def _out_ln(a1, a2, w, x, ln_g, ln_b, *, j, i):
    rows = x.shape[0]
    tr = min(ROW_TILE, rows)
    n1, n2 = a1.shape[1], a2.shape[1]
    tile = lambda n: pl.BlockSpec((tr, n), lambda r: (r, 0))
    return pl.pallas_call(
        _out_ln_kernel,
        grid=(rows // tr,),
        in_specs=[tile(n1), tile(n2), _layer_block((n1 + n2, D_MODEL), j), tile(D_MODEL),
                  _layer_block((1, D_MODEL), i), _layer_block((1, D_MODEL), i)],
        out_specs=tile(D_MODEL),
        out_shape=jax.ShapeDtypeStruct((rows, D_MODEL), F32),
        compiler_params=_cparams(("arbitrary",)),
        name="out_ln",
    )(a1, a2, w, x, ln_g, ln_b)


def _mix_a_front_sample_kernel(x_ref, hist_ref, win_ref, wc_ref, y_o, qm_o, hist_o, *, t_len, bsz):
    h = _dot(x_ref[...].astype(BF16), win_ref[...])
    u = h[:, 2 * SC_DIM:3 * SC_DIM] * h[:, :SC_DIM]
    slabs = [hist_ref[:, j, :] for j in range(SC_W - 1)]
    slabs += [u[t * bsz:(t + 1) * bsz] for t in range(t_len)]
    wc = wc_ref[...]
    for t in range(t_len):
        conv = wc[0:1] * slabs[t]
        for j in range(1, SC_W):
            conv = conv + wc[j:j + 1] * slabs[t + j]
        y_o[t * bsz:(t + 1) * bsz, :] = (h[t * bsz:(t + 1) * bsz, SC_DIM:2 * SC_DIM] * conv).astype(BF16)
    for j in range(SC_W - 1):
        hist_o[:, j, :] = slabs[t_len + j]
    qm_o[...] = h[:, 3 * SC_DIM:]


def _whole(shape):
    nd = len(shape)
    return pl.BlockSpec(tuple(shape), lambda *_: (0,) * nd)


def _mix_a_front_sample(x2d, hist, w_in, w_conv, *, j, t_len, bsz):
    rows = x2d.shape[0]
    out_shape = [jax.ShapeDtypeStruct((rows, SC_DIM), BF16),
                 jax.ShapeDtypeStruct((rows, XDIM), F32),
                 jax.ShapeDtypeStruct((bsz, SC_W - 1, SC_DIM), F32)]
    return pl.pallas_call(
        functools.partial(_mix_a_front_sample_kernel, t_len=t_len, bsz=bsz),
        grid=(1,),
        in_specs=[_whole(x2d.shape), _layer_block(hist.shape[1:], j), _layer_block(w_in.shape[1:], j),
                  _layer_block(w_conv.shape[1:], j)],
        out_specs=[_whole(s.shape) for s in out_shape],
        out_shape=out_shape,
        compiler_params=_cparams(("arbitrary",)),
        name="mix_a_front_sample",
    )(x2d, hist, w_in, w_conv)


def _proj_b_front_sample_kernel(x_ref, hist_ref, win_ref, wc_ref, arow_ref, dtrow_ref,
                                q_o, k_o, v_o, gb_o, zg_o, qm_o, hist_o, *, t_len, bsz):
    nqkv = 3 * GDN_DIM
    h = _dot(x_ref[...].astype(BF16), win_ref[...])
    slabs = [hist_ref[j] for j in range(GDN_CONV_W - 1)]
    slabs += [h[t * bsz:(t + 1) * bsz, :nqkv] for t in range(t_len)]
    wc = wc_ref[...]
    for t in range(t_len):
        conv = wc[0:1] * slabs[t]
        for j in range(1, GDN_CONV_W):
            conv = conv + wc[j:j + 1] * slabs[t + j]
        _qkv_post(conv, q_o, k_o, v_o, (t, slice(None)))
    for j in range(GDN_CONV_W - 1):
        hist_o[j] = slabs[t_len + j]
    zg_o[...] = _silu(h[:, nqkv:nqkv + GDN_DIM])
    qm_o[...] = h[:, nqkv + GDN_DIM:nqkv + GDN_DIM + XDIM]
    gb_o[...] = _gdn_gates(h[:, nqkv + GDN_DIM + XDIM:], arow_ref[...], dtrow_ref[...])


def _proj_b_front_sample(x2d, hist, w_in, w_conv, a_row, dt_row, *, j, t_len, bsz):
    rows = x2d.shape[0]
    tmaj = jax.ShapeDtypeStruct((t_len, bsz, GDN_DIM), F32)
    out_shape = [tmaj, tmaj, tmaj,
                 jax.ShapeDtypeStruct((rows, GB_LANES), F32),
                 jax.ShapeDtypeStruct((rows, GDN_DIM), F32),
                 jax.ShapeDtypeStruct((rows, XDIM), F32),
                 jax.ShapeDtypeStruct((GDN_CONV_W - 1, bsz, 3 * GDN_DIM), F32)]
    return pl.pallas_call(
        functools.partial(_proj_b_front_sample_kernel, t_len=t_len, bsz=bsz),
        grid=(1,),
        in_specs=[_whole(x2d.shape), _layer_block(hist.shape[1:], j), _layer_block(w_in.shape[1:], j),
                  _layer_block(w_conv.shape[1:], j), _layer_block((1, GB_LANES), j),
                  _layer_block((1, GB_LANES), j)],
        out_specs=[_whole(s.shape) for s in out_shape],
        out_shape=out_shape,
        compiler_params=_cparams(("arbitrary",)),
        name="proj_b_front_sample",
    )(x2d, hist, w_in, w_conv, a_row, dt_row)


def _ffn_sample_kernel(x_ref, hist_ref, w_ref, c_ref, wdn_ref, g_ref, b_ref, o_ref, hist_o,
                       acc, gate, *, t_len, bsz, nj):
    s = pl.program_id(0)
    h = _dot(x_ref[...].astype(BF16), w_ref[...])
    slabs = [hist_ref[:, r, :] for r in range(FFN_W - 1)]
    slabs += [h[t * bsz:(t + 1) * bsz] for t in range(t_len)]
    wc = c_ref[...]
    outs = []
    for t in range(t_len):
        c = wc[0:1] * slabs[t]
        for w in range(1, FFN_W):
            c = c + wc[w:w + 1] * slabs[t + w]
        outs.append(c)
    for r in range(FFN_W - 1):
        hist_o[:, r, :] = slabs[t_len + r]
    conv = jnp.concatenate(outs, axis=0)

    @pl.when(s < nj)
    def _():
        gate[s] = _silu(conv)

    @pl.when(s >= nj)
    def _():
        d = _dot((gate[s - nj] * conv).astype(BF16), wdn_ref[...])

        @pl.when(s == nj)
        def _():
            acc[...] = d

        @pl.when(s > nj)
        def _():
            acc[...] += d

    @pl.when(s == 2 * nj - 1)
    def _():
        o_ref[...] = _layer_norm(ALPHA * x_ref[...] + acc[...], g_ref[...], b_ref[...])


def _ffn_sample(x2d, hist, w_up, w_conv, w_down, ln_g, ln_b, *, i, t_len, bsz):
    rows = x2d.shape[0]
    tf = FF_TILE_SAMPLE
    nj = D_FF // tf
    return pl.pallas_call(
        functools.partial(_ffn_sample_kernel, t_len=t_len, bsz=bsz, nj=nj),
        grid=(2 * nj,),
        in_specs=[_resident((rows, D_MODEL)),
                  pl.BlockSpec((None, bsz, FFN_W - 1, tf), lambda s: (i, 0, 0, s)),
                  pl.BlockSpec((None, D_MODEL, tf), lambda s: (i, 0, s)),
                  pl.BlockSpec((None, FFN_W, tf), lambda s: (i, 0, s)),
                  pl.BlockSpec((None, tf, D_MODEL), lambda s: (i, jnp.maximum(s - nj, 0), 0)),
                  _layer_block((1, D_MODEL), i), _layer_block((1, D_MODEL), i)],
        out_specs=[pl.BlockSpec((rows, D_MODEL), lambda s: (0, 0)),
                   pl.BlockSpec((bsz, FFN_W - 1, tf), lambda s: (0, 0, s))],
        out_shape=[jax.ShapeDtypeStruct((rows, D_MODEL), F32),
                   jax.ShapeDtypeStruct((bsz, FFN_W - 1, 2 * D_FF), F32)],
        scratch_shapes=[pltpu.VMEM((rows, D_MODEL), F32), pltpu.VMEM((nj, rows, tf), F32)],
        compiler_params=_cparams(("arbitrary",)),
        name="ffn_sample",
    )(x2d, hist, w_up, w_conv, w_down, ln_g, ln_b)


def _prep_weights(w_in_a, w_out_a, w_in_b, a_log, dt_bias, w_out_b, w_mem_kv, w_up, w_down):
    nqkvz = 4 * GDN_DIM
    w_b = jnp.concatenate(
        [w_in_b[..., :nqkvz], w_in_b[..., nqkvz + 2 * GDN_H:],
         jnp.pad(w_in_b[..., nqkvz:nqkvz + 2 * GDN_H], ((0, 0), (0, 0), (0, GB_LANES - 2 * GDN_H)))],
        axis=-1).astype(BF16)
    pad = GB_LANES - 2 * GDN_H
    a_row = jnp.pad(-jnp.exp(a_log.astype(F32)), ((0, 0), (GDN_H, pad)))[:, None, :]
    dt_row = jnp.pad(dt_bias.astype(F32), ((0, 0), (GDN_H, pad)))[:, None, :]
    return dict(w_in_a=w_in_a.astype(BF16), w_out_a=w_out_a.astype(BF16), w_in_b=w_b,
                w_out_b=w_out_b.astype(BF16), w_mem_kv=w_mem_kv.astype(BF16),
                w_up=w_up.astype(BF16), w_down=w_down.astype(BF16), a_row=a_row, dt_row=dt_row)


def _trunk_prompt(x, mem_k, mem_v, wts, conv_a, conv_b, gdn_norm_w, ln1_g, ln1_b, ln2_g, ln2_b,
                  w_conv_ffn):
    bsz, t_len, _ = x.shape
    z = lambda *s: jnp.zeros(s, F32)
    new_sc, new_gc, new_gs, new_ffn = [], [], [], []
    for i in range(DEPTH):
        j = i // 2
        if i % 2 == 0:
            x, hs = _mix_a_seq(x, z(bsz, SC_W - 1, SC_DIM), wts["w_in_a"], conv_a, wts["w_out_a"],
                               mem_k, mem_v, ln1_g, ln1_b, j=j, i=i)
            new_sc.append(hs)
        else:
            q, k, v, gb, zg, om, hg = _proj_b_seq(
                x, z(bsz, GDN_CONV_W - 1, 3 * GDN_DIM), wts["w_in_b"], conv_b,
                wts["a_row"], wts["dt_row"], mem_k, mem_v, j=j, i=i)
            o, sg = _gdn_scan(q, k, v, gb, zg, gdn_norm_w, z(1, bsz, GDN_H, GDN_DK, GDN_DV),
                              j=j, layer=0, c=GDN_CHUNK, bb=SCAN_SEQS, nc=SCAN_CHUNKS)
            rows = bsz * t_len
            x = _out_ln(o.reshape(rows, GDN_DIM), om.reshape(rows, XDIM), wts["w_out_b"],
                        x.reshape(rows, D_MODEL), ln1_g, ln1_b, j=j, i=i).reshape(bsz, t_len, D_MODEL)
            new_gc.append(hg)
            new_gs.append(sg)
        x, hf = _ffn_seq(x, z(bsz, FFN_W - 1, 2 * D_FF), wts["w_up"], w_conv_ffn, wts["w_down"],
                         ln2_g, ln2_b, i=i)
        new_ffn.append(hf)
    return x, jnp.stack(new_sc), jnp.stack(new_gc), jnp.stack(new_gs), jnp.stack(new_ffn)


def _trunk_sample(x, mem_k, mem_v, sc_hist, gdn_hist, gdn_s, ffn_hist, wts, conv_a, conv_b,
                  gdn_norm_w, ln1_g, ln1_b, ln2_g, ln2_b, w_conv_ffn):
    bsz, t_len, _ = x.shape
    rows = bsz * t_len
    t_pad = SUBLANES
    x2 = jnp.transpose(x, (1, 0, 2)).reshape(rows, D_MODEL)
    to_bm = lambda a: jnp.transpose(a, (1, 0, 2))
    new_sc, new_gc, new_gs, new_ffn = [], [], [], []
    mem_kt = jnp.transpose(mem_k, (0, 1, 3, 4, 2))
    mem_vt = jnp.transpose(mem_v, (0, 1, 3, 4, 2))
    gdn_hist = jnp.transpose(gdn_hist, (0, 2, 1, 3))
    for i in range(DEPTH):
        j = i // 2
        if i % 2 == 0:
            y, qm, hs = _mix_a_front_sample(x2, sc_hist, wts["w_in_a"], conv_a, j=j,
                                            t_len=t_len, bsz=bsz)
            om = _attn_sample(qm.reshape(t_len, bsz, XDIM), mem_kt, mem_vt, i).reshape(rows, XDIM)
            x2 = _out_ln(y, om, wts["w_out_a"], x2, ln1_g, ln1_b, j=j, i=i)
            new_sc.append(hs)
        else:
            q, k, v, gb, zg, qm, hg = _proj_b_front_sample(
                x2, gdn_hist, wts["w_in_b"], conv_b, wts["a_row"], wts["dt_row"], j=j,
                t_len=t_len, bsz=bsz)
            om = _attn_sample(qm.reshape(t_len, bsz, XDIM), mem_kt, mem_vt, i).reshape(rows, XDIM)
            padt = lambda a: jnp.pad(to_bm(a.reshape(t_len, bsz, -1)), ((0, 0), (0, t_pad - t_len), (0, 0)))
            o, sg = _gdn_scan(padt(q), padt(k), padt(v), padt(gb), padt(zg), gdn_norm_w, gdn_s,
                              j=j, layer=j, c=t_pad, bb=SCAN_SEQS_SAMPLE, nc=1)
            o = jnp.transpose(o[:, :t_len], (1, 0, 2)).reshape(rows, GDN_DIM)
            x2 = _out_ln(o, om, wts["w_out_b"], x2, ln1_g, ln1_b, j=j, i=i)
            new_gc.append(hg)
            new_gs.append(sg)
        x2, hf = _ffn_sample(x2, ffn_hist, wts["w_up"], w_conv_ffn, wts["w_down"],
                             ln2_g, ln2_b, i=i, t_len=t_len, bsz=bsz)
        new_ffn.append(hf)
    y = jnp.transpose(x2.reshape(t_len, bsz, D_MODEL), (1, 0, 2))
    gc = jnp.transpose(jnp.stack(new_gc), (0, 2, 1, 3))
    return y, jnp.stack(new_sc), gc, jnp.stack(new_gs), jnp.stack(new_ffn)


def kernel(x_prompt, x_sample, mem_prompt, cache_mem_k, cache_mem_v, state_shortconv, state_gdn_conv,
           state_gdn, state_ffn_conv, w_in_a, conv_a, w_out_a, w_in_b, conv_b, a_log, dt_bias,
           gdn_norm_w, w_out_b, w_mem_kv, ln1_g, ln1_b, ln2_g, ln2_b, w_up, w_conv_ffn, w_down):
    wts = _prep_weights(w_in_a, w_out_a, w_in_b, a_log, dt_bias, w_out_b, w_mem_kv, w_up, w_down)
    row3 = lambda a: a.reshape(a.shape[0], 1, a.shape[1])
    shared = (wts, conv_a, conv_b, row3(gdn_norm_w), row3(ln1_g), row3(ln1_b), row3(ln2_g),
              row3(ln2_b), w_conv_ffn)
    bsz = x_prompt.shape[0]
    k2, v2 = _mem_kv(mem_prompt.reshape(bsz * N_MEM, D_MODEL), wts["w_mem_kv"])
    mem_k_prompt = k2.reshape(DEPTH, bsz, N_MEM, XDIM)
    mem_v_prompt = v2.reshape(DEPTH, bsz, N_MEM, XDIM)
    y_prompt, sc_p, gc_p, gs_p, ffn_p = _trunk_prompt(x_prompt, mem_k_prompt, mem_v_prompt, *shared)
    y_sample, sc_s, gc_s, gs_s, ffn_s = _trunk_sample(
        x_sample, cache_mem_k, cache_mem_v, state_shortconv, state_gdn_conv, state_gdn,
        state_ffn_conv, *shared)
    shape5 = (DEPTH, bsz, N_MEM, XH, XD)
    return (y_prompt, y_sample, mem_k_prompt.reshape(shape5), mem_v_prompt.reshape(shape5),
            sc_p, gc_p, gs_p, ffn_p, sc_s, gc_s, gs_s, ffn_s)
```

```python
import functools

import jax
import jax.numpy as jnp
from jax import lax
from jax.experimental import pallas as pl
from jax.experimental.pallas import tpu as pltpu

F32 = jnp.float32
BF16 = jnp.bfloat16

DEPTH = 4
D_MODEL = 1024
SC_DIM = 768
SC_W = 3
GDN_H = 6
GDN_DK = 128
GDN_DV = 128
GDN_DIM = GDN_H * GDN_DK
GDN_CONV_W = 4
GDN_CHUNK = 64
N_MEM = 256
XH = 4
XD = 64
XDIM = XH * XD
D_FF = 2816
FFN_W = 3
ALPHA = (2.0 * DEPTH) ** 0.25
LN_EPS = 1e-5
RMS_EPS = 1e-6

V7X_VMEM_BYTES = 64 * 1024 * 1024
VMEM_LIMIT = V7X_VMEM_BYTES - 8 * 1024 * 1024
SUBLANES = 8
LANES = 128

GB_LANES = LANES

SEQ_TILE = 512
MIX_TILE = 512
MIX_SUBTILES = 2
SCAN_SEQS = 2
SCAN_CHUNKS = 4
SCAN_SEQS_SAMPLE = 8
SOLVE_BLOCK = 16
ROW_TILE = 512
FF_TILE = 2816
FF_TILE_SAMPLE = 1408
ATTN_BATCH_BLOCK = 16
ATTN_UNROLL = 4
W_B_COLS = 3 * GDN_DIM + GDN_DIM + XDIM + GB_LANES


def _cparams(sem):
    return pltpu.CompilerParams(dimension_semantics=sem, vmem_limit_bytes=VMEM_LIMIT)


def _resident(shape):
    nd = len(shape)
    return pl.BlockSpec(shape, lambda *_: (0,) * nd, pipeline_mode=pl.Buffered(1))


def _layer_block(shape, layer):
    nd = len(shape)
    return pl.BlockSpec((None,) + tuple(shape), lambda *_: (layer,) + (0,) * nd,
                        pipeline_mode=pl.Buffered(1))


def _silu(x):
    return x * (1.0 / (1.0 + jnp.exp(-x)))


def _sigmoid(x):
    return 1.0 / (1.0 + jnp.exp(-x))


def _softplus(x):
    return jnp.maximum(x, 0.0) + jnp.log(1.0 + jnp.exp(-jnp.abs(x)))


def _layer_norm(v, g, b):
    mu = jnp.mean(v, -1, keepdims=True)
    d = v - mu
    var = jnp.mean(d * d, -1, keepdims=True)
    return d * lax.rsqrt(var + LN_EPS) * g + b


def _dot(a, b):
    return jnp.dot(a, b, preferred_element_type=F32)


def _dot_nt(a, b):
    return lax.dot_general(a, b, (((1,), (1,)), ((), ())), preferred_element_type=F32)


def _dot_tn(a, b):
    return lax.dot_general(a, b, (((0,), (0,)), ((), ())), preferred_element_type=F32)


def _dot_hi(a, b):
    return jnp.dot(a, b, preferred_element_type=F32, precision=lax.Precision.HIGHEST)


def _dot_nt_hi(a, b):
    return lax.dot_general(a, b, (((1,), (1,)), ((), ())), preferred_element_type=F32,
                           precision=lax.Precision.HIGHEST)


def _kv_kernel(m_ref, w_ref, k_ref, v_ref):
    kv = _dot(m_ref[...].astype(BF16), w_ref[0])
    k_ref[0] = kv[:, :XDIM]
    v_ref[0] = kv[:, XDIM:]


def _mem_kv(mem2d, w_kv):
    rows = mem2d.shape[0]
    out = jax.ShapeDtypeStruct((DEPTH, rows, XDIM), F32)
    return pl.pallas_call(
        _kv_kernel,
        grid=(DEPTH,),
        in_specs=[_resident((rows, D_MODEL)),
                  pl.BlockSpec((1, D_MODEL, 2 * XDIM), lambda l: (l, 0, 0))],
        out_specs=[pl.BlockSpec((1, rows, XDIM), lambda l: (l, 0, 0))] * 2,
        out_shape=[out, out],
        compiler_params=_cparams(("arbitrary",)),
        name="mem_kv",
    )(mem2d, w_kv)


def _head_blockdiag(kv):
    lane_head = lax.broadcasted_iota(jnp.int32, kv.shape, 1) // XD
    return jnp.concatenate(
        [jnp.where(lane_head == h, kv, 0.0).astype(BF16) for h in range(XH)], axis=0)


def _softmax_rows(s):
    m = jnp.max(s, -1, keepdims=True)
    e = jnp.exp(s - m)
    return e / jnp.sum(e, -1, keepdims=True)


def _attn_sample_kernel(q_ref, k_ref, v_ref, o_ref, *, bb, t_len):
    def body(g, carry):
        elems = [g * ATTN_UNROLL + e for e in range(ATTN_UNROLL)]
        q8 = []
        for i in elems:
            rows = [q_ref[t, pl.ds(i, 1), :] for t in range(t_len)]
            rows.append(jnp.zeros((SUBLANES - t_len, XDIM), F32))
            q8.append(jnp.concatenate(rows, axis=0).astype(BF16))
        s = [[_dot(q8[e][:, h * XD:(h + 1) * XD], k_ref[0, i, h].astype(BF16)) * (XD ** -0.5)
              for h in range(XH)] for e, i in enumerate(elems)]
        p = [[_softmax_rows(sh).astype(BF16) for sh in se] for se in s]
        for e, i in enumerate(elems):
            o8 = jnp.concatenate([_dot_nt(p[e][h], v_ref[0, i, h].astype(BF16)) for h in range(XH)],
                                 axis=-1)
            for t in range(t_len):
                o_ref[t, pl.ds(i, 1), :] = o8[t:t + 1]
        return carry

    lax.fori_loop(0, bb // ATTN_UNROLL, body, 0)


def _attn_sample(qm_tm, mem_kt, mem_vt, layer):
    t_len, bsz, _ = qm_tm.shape
    bb = min(ATTN_BATCH_BLOCK, bsz)
    kv = pl.BlockSpec((1, bb, XH, XD, N_MEM), lambda i: (layer, i, 0, 0, 0))
    return pl.pallas_call(
        functools.partial(_attn_sample_kernel, bb=bb, t_len=t_len),
        grid=(bsz // bb,),
        in_specs=[pl.BlockSpec((t_len, bb, XDIM), lambda i: (0, i, 0)), kv, kv],
        out_specs=pl.BlockSpec((t_len, bb, XDIM), lambda i: (0, i, 0)),
        out_shape=jax.ShapeDtypeStruct((t_len, bsz, XDIM), F32),
        compiler_params=_cparams(("arbitrary",)),
        name="attn_sample",
    )(qm_tm, mem_kt, mem_vt)


def _mix_a_seq_kernel(x_ref, hist_ref, win_ref, wc_ref, wout_ref, k_ref, v_ref, g_ref, b_ref,
                      o_ref, hist_o_ref, ubuf, kbd, vbd, *, tm):
    t = pl.program_id(1)
    lo = SUBLANES - (SC_W - 1)

    @pl.when(t == 0)
    def _():
        ubuf[lo:SUBLANES, :] = hist_ref[0]
        kbd[...] = _head_blockdiag(k_ref[0])
        vbd[...] = _head_blockdiag(v_ref[0])

    ts = tm // MIX_SUBTILES
    subs = range(MIX_SUBTILES)
    rows = [slice(s * ts, (s + 1) * ts) for s in subs]
    hs = [_dot(x_ref[0, rows[s], :].astype(BF16), win_ref[...]) for s in subs]
    us = [hs[s][:, 2 * SC_DIM:3 * SC_DIM] * hs[s][:, :SC_DIM] for s in subs]
    for s in subs:
        ubuf[SUBLANES + s * ts:SUBLANES + (s + 1) * ts, :] = us[s]
    wc = wc_ref[...]
    ys = []
    for s in subs:
        conv = wc[SC_W - 1:SC_W] * us[s]
        for j in range(SC_W - 1):
            conv = conv + wc[j:j + 1] * ubuf[lo + j + s * ts:lo + j + (s + 1) * ts, :]
        ys.append((hs[s][:, SC_DIM:2 * SC_DIM] * conv).astype(BF16))
    last = ubuf[tm + lo:tm + SUBLANES, :]
    ubuf[lo:SUBLANES, :] = last
    hist_o_ref[0] = last

    sc = [_dot_nt(hs[s][:, 3 * SC_DIM:].astype(BF16), kbd[...]) * (XD ** -0.5) for s in subs]
    ps = [jnp.concatenate([_softmax_rows(sc[s][:, h * N_MEM:(h + 1) * N_MEM]).astype(BF16)
                           for h in range(XH)], axis=-1) for s in subs]
    oms = [_dot(ps[s], vbd[...]).astype(BF16) for s in subs]
    mix = [_dot(ys[s], wout_ref[:SC_DIM, :]) + _dot(oms[s], wout_ref[SC_DIM:, :]) for s in subs]
    for s in subs:
        o_ref[0, rows[s], :] = _layer_norm(ALPHA * x_ref[0, rows[s], :] + mix[s], g_ref[...], b_ref[...])


def _mix_a_seq(x, hist, w_in, w_conv, w_out, mem_k, mem_v, ln_g, ln_b, *, j, i):
    bsz, t_len, _ = x.shape
    tm = min(MIX_TILE, t_len)
    mem = pl.BlockSpec((None, 1, N_MEM, XDIM), lambda b, t: (i, b, 0, 0))
    return pl.pallas_call(
        functools.partial(_mix_a_seq_kernel, tm=tm),
        grid=(bsz, t_len // tm),
        in_specs=[pl.BlockSpec((1, tm, D_MODEL), lambda b, t: (b, t, 0)),
                  pl.BlockSpec((1, SC_W - 1, SC_DIM), lambda b, t: (b, 0, 0)),
                  _layer_block(w_in.shape[1:], j),
                  _layer_block((SC_W, SC_DIM), j),
                  _layer_block((SC_DIM + XDIM, D_MODEL), j),
                  mem, mem,
                  _layer_block((1, D_MODEL), i),
                  _layer_block((1, D_MODEL), i)],
        out_specs=[pl.BlockSpec((1, tm, D_MODEL), lambda b, t: (b, t, 0)),
                   pl.BlockSpec((1, SC_W - 1, SC_DIM), lambda b, t: (b, 0, 0))],
        out_shape=[jax.ShapeDtypeStruct((bsz, t_len, D_MODEL), F32),
                   jax.ShapeDtypeStruct((bsz, SC_W - 1, SC_DIM), F32)],
        scratch_shapes=[pltpu.VMEM((tm + SUBLANES, SC_DIM), F32),
                        pltpu.VMEM((XH * N_MEM, XDIM), BF16),
                        pltpu.VMEM((XH * N_MEM, XDIM), BF16)],
        compiler_params=_cparams(("arbitrary", "arbitrary")),
        name="mix_a_seq",
    )(x, hist, w_in, w_conv, w_out, mem_k, mem_v, ln_g, ln_b)


def _ffn_seq_kernel(x_ref, hist_ref, wup_ref, wc_ref, wdn_ref, g_ref, b_ref,
                    o_ref, hist_o_ref, gbuf, ubuf, *, tm, tf):
    t = pl.program_id(1)
    lo = SUBLANES - (FFN_W - 1)
    nj = D_FF // tf
    halves = ((gbuf, 0), (ubuf, D_FF))

    @pl.when(t == 0)
    def _():
        for buf, off in halves:
            for j in range(nj):
                buf[j, lo:SUBLANES, :] = hist_ref[0, :, off + j * tf:off + (j + 1) * tf]

    x = x_ref[0]
    xb = x.astype(BF16)

    def up_project(j):
        return [_dot(xb, wup_ref[:, off + j * tf:off + (j + 1) * tf]) for _, off in halves]

    def conv_act(j, hs):
        conv = []
        for (buf, off), h in zip(halves, hs):
            c0 = off + j * tf
            buf[j, SUBLANES:SUBLANES + tm, :] = h
            c = wc_ref[FFN_W - 1:FFN_W, c0:c0 + tf] * h
            for w in range(FFN_W - 1):
                c = c + wc_ref[w:w + 1, c0:c0 + tf] * buf[j, lo + w:lo + w + tm, :]
            last = buf[j, tm + lo:tm + SUBLANES, :]
            buf[j, lo:SUBLANES, :] = last
            hist_o_ref[0, :, c0:c0 + tf] = last
            conv.append(c)
        return (_silu(conv[0]) * conv[1]).astype(BF16)

    acc = None
    hs = up_project(0)
    for j in range(nj):
        hs_next = up_project(j + 1) if j + 1 < nj else None
        d = _dot(conv_act(j, hs), wdn_ref[j * tf:(j + 1) * tf, :])
        acc = d if acc is None else acc + d
        hs = hs_next
    o_ref[0] = _layer_norm(ALPHA * x + acc, g_ref[...], b_ref[...])


def _ffn_seq(x, hist, w_up, w_conv, w_down, ln_g, ln_b, *, i):
    bsz, t_len, _ = x.shape
    tm = min(SEQ_TILE, t_len)
    tf = FF_TILE
    return pl.pallas_call(
        functools.partial(_ffn_seq_kernel, tm=tm, tf=tf),
        grid=(bsz, t_len // tm),
        in_specs=[pl.BlockSpec((1, tm, D_MODEL), lambda b, t: (b, t, 0)),
                  pl.BlockSpec((1, FFN_W - 1, 2 * D_FF), lambda b, t: (b, 0, 0)),
                  _layer_block((D_MODEL, 2 * D_FF), i),
                  _layer_block((FFN_W, 2 * D_FF), i),
                  _layer_block((D_FF, D_MODEL), i),
                  _layer_block((1, D_MODEL), i),
                  _layer_block((1, D_MODEL), i)],
        out_specs=[pl.BlockSpec((1, tm, D_MODEL), lambda b, t: (b, t, 0)),
                   pl.BlockSpec((1, FFN_W - 1, 2 * D_FF), lambda b, t: (b, 0, 0))],
        out_shape=[jax.ShapeDtypeStruct((bsz, t_len, D_MODEL), F32),
                   jax.ShapeDtypeStruct((bsz, FFN_W - 1, 2 * D_FF), F32)],
        scratch_shapes=[pltpu.VMEM((D_FF // tf, tm + SUBLANES, tf), F32),
                        pltpu.VMEM((D_FF // tf, tm + SUBLANES, tf), F32)],
        compiler_params=_cparams(("arbitrary", "arbitrary")),
        name="ffn_seq",
    )(x, hist, w_up, w_conv, w_down, ln_g, ln_b)


def _gdn_gates(ba, a_row, dt_row):
    lane = lax.broadcasted_iota(jnp.int32, ba.shape, 1)
    return jnp.where(lane < GDN_H, _sigmoid(ba), a_row * _softplus(ba + dt_row))


def _qkv_post(c, q_ref, k_ref, v_ref, idx):
    c = _silu(c)
    for h in range(GDN_H):
        for ref, base in ((q_ref, 0), (k_ref, GDN_DIM)):
            a = c[:, base + h * GDN_DK:base + (h + 1) * GDN_DK]
            a = a * lax.rsqrt(jnp.sum(a * a, -1, keepdims=True) + RMS_EPS)
            ref[idx + (slice(h * GDN_DK, (h + 1) * GDN_DK),)] = a
    v_ref[idx + (slice(None),)] = c[:, 2 * GDN_DIM:]


def _proj_b_seq_kernel(x_ref, hist_ref, win_ref, wc_ref, arow_ref, dtrow_ref, k_ref, v_ref,
                       q_o, k_o, v_o, gb_o, zg_o, om_o, hist_o_ref, sbuf, kbd, vbd, *, tm):
    t = pl.program_id(1)
    lo = SUBLANES - (GDN_CONV_W - 1)
    nqkv = 3 * GDN_DIM

    @pl.when(t == 0)
    def _():
        sbuf[lo:SUBLANES, :] = hist_ref[0]
        kbd[...] = _head_blockdiag(k_ref[0])
        vbd[...] = _head_blockdiag(v_ref[0])

    ts = tm // MIX_SUBTILES
    subs = range(MIX_SUBTILES)
    rows = [slice(s * ts, (s + 1) * ts) for s in subs]
    hs = [_dot(x_ref[0, rows[s], :].astype(BF16), win_ref[...]) for s in subs]
    for s in subs:
        sbuf[SUBLANES + s * ts:SUBLANES + (s + 1) * ts, :] = hs[s][:, :nqkv]
    wc = wc_ref[...]
    qm0 = nqkv + GDN_DIM
    sc = []
    for s in subs:
        conv = wc[GDN_CONV_W - 1:GDN_CONV_W] * hs[s][:, :nqkv]
        for j in range(GDN_CONV_W - 1):
            conv = conv + wc[j:j + 1] * sbuf[lo + j + s * ts:lo + j + (s + 1) * ts, :]
        sc.append(_dot_nt(hs[s][:, qm0:qm0 + XDIM].astype(BF16), kbd[...]) * (XD ** -0.5))
        _qkv_post(conv, q_o, k_o, v_o, (0, rows[s]))
        zg_o[0, rows[s], :] = _silu(hs[s][:, nqkv:qm0])
        gb_o[0, rows[s], :] = _gdn_gates(hs[s][:, qm0 + XDIM:], arow_ref[...], dtrow_ref[...])
    last = sbuf[tm + lo:tm + SUBLANES, :]
    sbuf[lo:SUBLANES, :] = last
    hist_o_ref[0] = last
    ps = [jnp.concatenate([_softmax_rows(sc[s][:, h * N_MEM:(h + 1) * N_MEM]).astype(BF16)
                           for h in range(XH)], axis=-1) for s in subs]
    for s in subs:
        om_o[0, rows[s], :] = _dot(ps[s], vbd[...]).astype(BF16)


def _proj_b_seq(x, hist, w_in, w_conv, a_row, dt_row, mem_k, mem_v, *, j, i):
    bsz, t_len, _ = x.shape
    tm = min(MIX_TILE, t_len)
    tile = lambda n: pl.BlockSpec((1, tm, n), lambda b, t: (b, t, 0))
    f32o = lambda n: jax.ShapeDtypeStruct((bsz, t_len, n), F32)
    mem = pl.BlockSpec((None, 1, N_MEM, XDIM), lambda b, t: (i, b, 0, 0))
    return pl.pallas_call(
        functools.partial(_proj_b_seq_kernel, tm=tm),
        grid=(bsz, t_len // tm),
        in_specs=[tile(D_MODEL),
                  pl.BlockSpec((1, GDN_CONV_W - 1, 3 * GDN_DIM), lambda b, t: (b, 0, 0)),
                  _layer_block((D_MODEL, W_B_COLS), j),
                  _layer_block((GDN_CONV_W, 3 * GDN_DIM), j),
                  _layer_block((1, GB_LANES), j),
                  _layer_block((1, GB_LANES), j),
                  mem, mem],
        out_specs=[tile(GDN_DIM), tile(GDN_DIM), tile(GDN_DIM), tile(GB_LANES), tile(GDN_DIM),
                   tile(XDIM),
                   pl.BlockSpec((1, GDN_CONV_W - 1, 3 * GDN_DIM), lambda b, t: (b, 0, 0))],
        out_shape=[f32o(GDN_DIM), f32o(GDN_DIM), f32o(GDN_DIM), f32o(GB_LANES), f32o(GDN_DIM),
                   jax.ShapeDtypeStruct((bsz, t_len, XDIM), BF16),
                   jax.ShapeDtypeStruct((bsz, GDN_CONV_W - 1, 3 * GDN_DIM), F32)],
        scratch_shapes=[pltpu.VMEM((tm + SUBLANES, 3 * GDN_DIM), F32),
                        pltpu.VMEM((XH * N_MEM, XDIM), BF16),
                        pltpu.VMEM((XH * N_MEM, XDIM), BF16)],
        compiler_params=_cparams(("arbitrary", "arbitrary")),
        name="proj_b_seq",
    )(x, hist, w_in, w_conv, a_row, dt_row, mem_k, mem_v)


def _gdn_scan_kernel(q_ref, k_ref, v_ref, gb_ref, zg_ref, nw_ref, s0_ref, prev_ref, o_ref, s_o_ref,
                     s_scr, *, bb, nc, c, n_prev):
    t = pl.program_id(1)
    tc = nc * c

    @pl.when(t == 0)
    def _():
        s_scr[...] = s0_ref[0]
        for n in range(n_prev):
            s_o_ref[n] = prev_ref[n]

    ri = lax.broadcasted_iota(jnp.int32, (c, c), 0)
    ci = lax.broadcasted_iota(jnp.int32, (c, c), 1)
    tril = ri >= ci
    strict = ri > ci
    rt = lax.broadcasted_iota(jnp.int32, (tc, tc), 0)
    ct = lax.broadcasted_iota(jnp.int32, (tc, tc), 1)
    tril_chunks = ((rt >= ct) & ((rt // c) == (ct // c))).astype(F32)
    eye_l = (lax.broadcasted_iota(jnp.int32, (GB_LANES, GB_LANES), 0)
             == lax.broadcasted_iota(jnp.int32, (GB_LANES, GB_LANES), 1)).astype(F32)
    nw = nw_ref[...]

    gcs = []
    for i in range(bb):
        gc_all = _dot_hi(tril_chunks, gb_ref[i])
        gcs.append((gc_all, _dot_nt_hi(eye_l, gc_all)))

    probs = [(i, ic, h) for i in range(bb) for ic in range(nc) for h in range(GDN_H)]
    st = {}
    for p in probs:
        i, ic, h = p
        rows = slice(ic * c, (ic + 1) * c)
        hs = slice(h * GDN_DK, (h + 1) * GDN_DK)
        gc_all, gc_t = gcs[i]
        q = q_ref[i, rows, hs] * (GDN_DK ** -0.5)
        k = k_ref[i, rows, hs]
        gcol = gc_all[rows, GDN_H + h:GDN_H + h + 1]
        grow = gc_t[GDN_H + h:GDN_H + h + 1, rows]
        glast = gc_all[(ic + 1) * c - 1:(ic + 1) * c, GDN_H + h:GDN_H + h + 1]
        egc = jnp.exp(gcol)
        kb = k * gb_ref[i, rows, h:h + 1]
        st[p] = dict(
            decay=jnp.exp(jnp.where(tril, gcol - grow, -jnp.inf)),
            aq=_dot_nt(jnp.concatenate([kb, q], axis=0).astype(BF16), k.astype(BF16)),
            sol=jnp.concatenate([v_ref[i, rows, hs] * gb_ref[i, rows, h:h + 1], kb * egc], axis=-1),
            qd=(q * egc).astype(BF16),
            k_dec=(k * jnp.exp(glast - gcol)).astype(BF16),
            e_last=jnp.exp(glast))
    sb = min(SOLVE_BLOCK, c)
    nblk = c // sb
    same_blk = (ri // sb) == (ci // sb)
    lane_sb = lax.broadcasted_iota(jnp.int32, (sb, c), 1)
    eye_ss = ((lane_sb % sb) == lax.broadcasted_iota(jnp.int32, (sb, c), 0)).astype(F32)
    blk_rows = [slice(n * sb, (n + 1) * sb) for n in range(nblk)]
    tile_rows = lambda a: jnp.concatenate([a] * nblk, axis=0) if nblk > 1 else a

    def split2(x):
        hi = x.astype(BF16)
        return jnp.concatenate([hi, (x - hi.astype(F32)).astype(BF16)], axis=-1)

    def fold2(y):
        n = y.shape[-1] // 2
        return y[:, :n] + y[:, n:]

    for p in probs:
        d = st[p]
        d["lmat"] = jnp.where(strict, d["aq"][:c] * d["decay"], 0.0)
        d["qk"] = (d["aq"][c:] * d["decay"]).astype(BF16)
        d["pss"] = sum(jnp.where(lane_sb // sb == n, d["lmat"][blk_rows[n]], 0.0) for n in range(nblk))
        d["pbd"] = jnp.where(same_blk, d["lmat"], 0.0).astype(BF16)
        d["tss"] = eye_ss - d["pss"]
    for _ in range(max(sb.bit_length() - 2, 0)):
        for p in probs:
            st[p]["pss"] = _dot(st[p]["pss"].astype(BF16), st[p]["pbd"])
        for p in probs:
            d = st[p]
            d["pbd"] = jnp.where(same_blk, tile_rows(d["pss"]), 0.0).astype(BF16)
            d["tss"] = d["tss"] + _dot(d["tss"].astype(BF16), d["pbd"])
    zero_blk = jnp.zeros((sb, 4 * GDN_DV), BF16)
    for p in probs:
        d = st[p]
        d["tbd"] = jnp.where(same_blk, tile_rows(d["tss"]), 0.0).astype(BF16)
        d["x2"] = []
        d["x"] = []
    for n in range(nblk):
        for p in probs:
            d = st[p]
            z = d["sol"][blk_rows[n]]
            if n > 0:
                lrow = jnp.where(lane_sb < n * sb, d["lmat"][blk_rows[n]], 0.0).astype(BF16)
                z = z - fold2(_dot(lrow, jnp.concatenate(d["x2"] + [zero_blk] * (nblk - n), axis=0)))
            d["y"] = jnp.concatenate([zero_blk] * n + [split2(z)] + [zero_blk] * (nblk - n - 1), axis=0)
        for p in probs:
            d = st[p]
            x = fold2(_dot(d["tbd"][blk_rows[n]], d["y"]))
            d["x"].append(x)
            if n + 1 < nblk:
                d["x2"].append(split2(x))
    for p in probs:
        d = st[p]
        sol = jnp.concatenate(d["x"], axis=0) if nblk > 1 else d["x"][0]
        d["wq"] = jnp.concatenate([sol[:, GDN_DV:].astype(BF16), d["qd"]], axis=0)
        d["u"] = sol[:, :GDN_DV]

    for ic in range(nc):
        rows = slice(ic * c, (ic + 1) * c)
        seqs = [(i, h) for i in range(bb) for h in range(GDN_H)]
        s_old = {ih: s_scr[ih[0], ih[1]] for ih in seqs}
        ws = {ih: _dot(st[ih[0], ic, ih[1]]["wq"], s_old[ih].astype(BF16)) for ih in seqs}
        vb = {ih: (st[ih[0], ic, ih[1]]["u"] - ws[ih][:c]).astype(BF16) for ih in seqs}
        for ih in seqs:
            d = st[ih[0], ic, ih[1]]
            s_scr[ih[0], ih[1]] = s_old[ih] * d["e_last"] + _dot_tn(d["k_dec"], vb[ih])
        for ih in seqs:
            i, h = ih
            hs = slice(h * GDN_DK, (h + 1) * GDN_DK)
            o = ws[ih][c:] + _dot(st[i, ic, h]["qk"], vb[ih])
            o = o * lax.rsqrt(jnp.mean(o * o, -1, keepdims=True) + RMS_EPS)
            o_ref[i, rows, hs] = (o * nw * zg_ref[i, rows, hs]).astype(o_ref.dtype)
    s_o_ref[n_prev] = s_scr[...]


def _gdn_scan(q, k, v, gb, zg, norm_w, s0_all, prev, *, j, layer, c, bb, nc):
    bsz, t_len, _ = q.shape
    tc = nc * c
    n_prev = 0 if prev is None else prev.shape[0]
    tile = lambda n: pl.BlockSpec((bb, tc, n), lambda b, t: (b, t, 0))
    st = lambda n: pl.BlockSpec((n, bb, GDN_H, GDN_DK, GDN_DV), lambda b, t: (0, b, 0, 0, 0))
    st_in = pl.BlockSpec((1, bb, GDN_H, GDN_DK, GDN_DV), lambda b, t: (layer, b, 0, 0, 0))
    if prev is None:
        prev, prev_spec = s0_all, pl.BlockSpec(memory_space=pl.ANY)
    else:
        prev_spec = st(n_prev)
    return pl.pallas_call(
        functools.partial(_gdn_scan_kernel, bb=bb, nc=nc, c=c, n_prev=n_prev),
        grid=(bsz // bb, t_len // tc),
        in_specs=[tile(GDN_DIM), tile(GDN_DIM), tile(GDN_DIM), tile(GB_LANES), tile(GDN_DIM),
                  _layer_block((1, GDN_DV), j), st_in, prev_spec],
        out_specs=[tile(GDN_DIM), st(n_prev + 1)],
        out_shape=[jax.ShapeDtypeStruct((bsz, t_len, GDN_DIM), BF16),
                   jax.ShapeDtypeStruct((n_prev + 1, bsz, GDN_H, GDN_DK, GDN_DV), F32)],
        scratch_shapes=[pltpu.VMEM((bb, GDN_H, GDN_DK, GDN_DV), F32)],
        compiler_params=_cparams(("arbitrary", "arbitrary")),
        name="gdn_scan",
    )(q, k, v, gb, zg, norm_w, s0_all, prev)


def _out_ln_kernel(a1_ref, a2_ref, w_ref, x_ref, g_ref, b_ref, o_ref):
    n1 = a1_ref.shape[-1]
    mix = (_dot(a1_ref[...].astype(BF16), w_ref[:n1, :])
           + _dot(a2_ref[...].astype(BF16), w_ref[n1:, :]))
    o_ref[...] = _layer_norm(ALPHA * x_ref[...] + mix, g_ref[...], b_ref[...])


def _out_ln(a1, a2, w, x, ln_g, ln_b, *, j, i):
    rows = x.shape[0]
    tr = min(ROW_TILE, rows)
    n1, n2 = a1.shape[1], a2.shape[1]
    tile = lambda n: pl.BlockSpec((tr, n), lambda r: (r, 0))
    return pl.pallas_call(
        _out_ln_kernel,
        grid=(rows // tr,),
        in_specs=[tile(n1), tile(n2), _layer_block((n1 + n2, D_MODEL), j), tile(D_MODEL),
                  _layer_block((1, D_MODEL), i), _layer_block((1, D_MODEL), i)],
        out_specs=tile(D_MODEL),
        out_shape=jax.ShapeDtypeStruct((rows, D_MODEL), F32),
        compiler_params=_cparams(("arbitrary",)),
        name="out_ln",
    )(a1, a2, w, x, ln_g, ln_b)


def _mix_a_front_sample_kernel(x_ref, hist_ref, win_ref, wc_ref, y_o, qm_o, hist_o, *, t_len, bsz):
    h = _dot(x_ref[...].astype(BF16), win_ref[...])
    u = h[:, 2 * SC_DIM:3 * SC_DIM] * h[:, :SC_DIM]
    slabs = [hist_ref[:, j, :] for j in range(SC_W - 1)]
    slabs += [u[t * bsz:(t + 1) * bsz] for t in range(t_len)]
    wc = wc_ref[...]
    for t in range(t_len):
        conv = wc[0:1] * slabs[t]
        for j in range(1, SC_W):
            conv = conv + wc[j:j + 1] * slabs[t + j]
        y_o[t * bsz:(t + 1) * bsz, :] = (h[t * bsz:(t + 1) * bsz, SC_DIM:2 * SC_DIM] * conv).astype(BF16)
    for j in range(SC_W - 1):
        hist_o[:, j, :] = slabs[t_len + j]
    qm_o[...] = h[:, 3 * SC_DIM:]


def _whole(shape):
    nd = len(shape)
    return pl.BlockSpec(tuple(shape), lambda *_: (0,) * nd)


def _mix_a_front_sample(x2d, hist, w_in, w_conv, *, j, t_len, bsz):
    rows = x2d.shape[0]
    out_shape = [jax.ShapeDtypeStruct((rows, SC_DIM), BF16),
                 jax.ShapeDtypeStruct((rows, XDIM), F32),
                 jax.ShapeDtypeStruct((bsz, SC_W - 1, SC_DIM), F32)]
    return pl.pallas_call(
        functools.partial(_mix_a_front_sample_kernel, t_len=t_len, bsz=bsz),
        grid=(1,),
        in_specs=[_whole(x2d.shape), _layer_block(hist.shape[1:], j), _layer_block(w_in.shape[1:], j),
                  _layer_block(w_conv.shape[1:], j)],
        out_specs=[_whole(s.shape) for s in out_shape],
        out_shape=out_shape,
        compiler_params=_cparams(("arbitrary",)),
        name="mix_a_front_sample",
    )(x2d, hist, w_in, w_conv)


def _proj_b_front_sample_kernel(x_ref, hist_ref, win_ref, wc_ref, arow_ref, dtrow_ref,
                                q_o, k_o, v_o, gb_o, zg_o, qm_o, hist_o, *, t_len, bsz):
    nqkv = 3 * GDN_DIM
    h = _dot(x_ref[...].astype(BF16), win_ref[...])
    slabs = [hist_ref[j] for j in range(GDN_CONV_W - 1)]
    slabs += [h[t * bsz:(t + 1) * bsz, :nqkv] for t in range(t_len)]
    wc = wc_ref[...]
    for t in range(t_len):
        conv = wc[0:1] * slabs[t]
        for j in range(1, GDN_CONV_W):
            conv = conv + wc[j:j + 1] * slabs[t + j]
        _qkv_post(conv, q_o, k_o, v_o, (t, slice(None)))
    for j in range(GDN_CONV_W - 1):
        hist_o[j] = slabs[t_len + j]
    zg_o[...] = _silu(h[:, nqkv:nqkv + GDN_DIM])
    qm_o[...] = h[:, nqkv + GDN_DIM:nqkv + GDN_DIM + XDIM]
    gb_o[...] = _gdn_gates(h[:, nqkv + GDN_DIM + XDIM:], arow_ref[...], dtrow_ref[...])


def _proj_b_front_sample(x2d, hist, w_in, w_conv, a_row, dt_row, *, j, t_len, bsz):
    rows = x2d.shape[0]
    tmaj = jax.ShapeDtypeStruct((t_len, bsz, GDN_DIM), F32)
    out_shape = [tmaj, tmaj, tmaj,
                 jax.ShapeDtypeStruct((rows, GB_LANES), F32),
                 jax.ShapeDtypeStruct((rows, GDN_DIM), F32),
                 jax.ShapeDtypeStruct((rows, XDIM), F32),
                 jax.ShapeDtypeStruct((GDN_CONV_W - 1, bsz, 3 * GDN_DIM), F32)]
    return pl.pallas_call(
        functools.partial(_proj_b_front_sample_kernel, t_len=t_len, bsz=bsz),
        grid=(1,),
        in_specs=[_whole(x2d.shape), _layer_block(hist.shape[1:], j), _layer_block(w_in.shape[1:], j),
                  _layer_block(w_conv.shape[1:], j), _layer_block((1, GB_LANES), j),
                  _layer_block((1, GB_LANES), j)],
        out_specs=[_whole(s.shape) for s in out_shape],
        out_shape=out_shape,
        compiler_params=_cparams(("arbitrary",)),
        name="proj_b_front_sample",
    )(x2d, hist, w_in, w_conv, a_row, dt_row)


def _ffn_sample_kernel(x_ref, hist_ref, w_ref, c_ref, wdn_ref, g_ref, b_ref, o_ref, hist_o,
                       acc, gate, *, t_len, bsz, nj):
    s = pl.program_id(0)
    h = _dot(x_ref[...].astype(BF16), w_ref[...])
    slabs = [hist_ref[:, r, :] for r in range(FFN_W - 1)]
    slabs += [h[t * bsz:(t + 1) * bsz] for t in range(t_len)]
    wc = c_ref[...]
    outs = []
    for t in range(t_len):
        c = wc[0:1] * slabs[t]
        for w in range(1, FFN_W):
            c = c + wc[w:w + 1] * slabs[t + w]
        outs.append(c)
    for r in range(FFN_W - 1):
        hist_o[:, r, :] = slabs[t_len + r]
    conv = jnp.concatenate(outs, axis=0)

    @pl.when(s < nj)
    def _():
        gate[s] = _silu(conv)

    @pl.when(s >= nj)
    def _():
        d = _dot((gate[s - nj] * conv).astype(BF16), wdn_ref[...])

        @pl.when(s == nj)
        def _():
            acc[...] = d

        @pl.when(s > nj)
        def _():
            acc[...] += d

    @pl.when(s == 2 * nj - 1)
    def _():
        o_ref[...] = _layer_norm(ALPHA * x_ref[...] + acc[...], g_ref[...], b_ref[...])


def _ffn_sample(x2d, hist, w_up, w_conv, w_down, ln_g, ln_b, *, i, t_len, bsz):
    rows = x2d.shape[0]
    tf = FF_TILE_SAMPLE
    nj = D_FF // tf
    return pl.pallas_call(
        functools.partial(_ffn_sample_kernel, t_len=t_len, bsz=bsz, nj=nj),
        grid=(2 * nj,),
        in_specs=[_resident((rows, D_MODEL)),
                  pl.BlockSpec((None, bsz, FFN_W - 1, tf), lambda s: (i, 0, 0, s)),
                  pl.BlockSpec((None, D_MODEL, tf), lambda s: (i, 0, s)),
                  pl.BlockSpec((None, FFN_W, tf), lambda s: (i, 0, s)),
                  pl.BlockSpec((None, tf, D_MODEL), lambda s: (i, jnp.maximum(s - nj, 0), 0)),
                  _layer_block((1, D_MODEL), i), _layer_block((1, D_MODEL), i)],
        out_specs=[pl.BlockSpec((rows, D_MODEL), lambda s: (0, 0)),
                   pl.BlockSpec((bsz, FFN_W - 1, tf), lambda s: (0, 0, s))],
        out_shape=[jax.ShapeDtypeStruct((rows, D_MODEL), F32),
                   jax.ShapeDtypeStruct((bsz, FFN_W - 1, 2 * D_FF), F32)],
        scratch_shapes=[pltpu.VMEM((rows, D_MODEL), F32), pltpu.VMEM((nj, rows, tf), F32)],
        compiler_params=_cparams(("arbitrary",)),
        name="ffn_sample",
    )(x2d, hist, w_up, w_conv, w_down, ln_g, ln_b)


def _prep_weights(w_in_a, w_out_a, w_in_b, a_log, dt_bias, w_out_b, w_mem_kv, w_up, w_down):
    nqkvz = 4 * GDN_DIM
    w_b = jnp.concatenate(
        [w_in_b[..., :nqkvz], w_in_b[..., nqkvz + 2 * GDN_H:],
         jnp.pad(w_in_b[..., nqkvz:nqkvz + 2 * GDN_H], ((0, 0), (0, 0), (0, GB_LANES - 2 * GDN_H)))],
        axis=-1).astype(BF16)
    pad = GB_LANES - 2 * GDN_H
    a_row = jnp.pad(-jnp.exp(a_log.astype(F32)), ((0, 0), (GDN_H, pad)))[:, None, :]
    dt_row = jnp.pad(dt_bias.astype(F32), ((0, 0), (GDN_H, pad)))[:, None, :]
    return dict(w_in_a=w_in_a.astype(BF16), w_out_a=w_out_a.astype(BF16), w_in_b=w_b,
                w_out_b=w_out_b.astype(BF16), w_mem_kv=w_mem_kv.astype(BF16),
                w_up=w_up.astype(BF16), w_down=w_down.astype(BF16), a_row=a_row, dt_row=dt_row)


def _trunk_prompt(x, mem_k, mem_v, wts, conv_a, conv_b, gdn_norm_w, ln1_g, ln1_b, ln2_g, ln2_b,
                  w_conv_ffn):
    bsz, t_len, _ = x.shape
    z = lambda *s: jnp.zeros(s, F32)
    new_sc, new_gc, new_gs, new_ffn = [], [], None, []
    for i in range(DEPTH):
        j = i // 2
        if i % 2 == 0:
            x, hs = _mix_a_seq(x, z(bsz, SC_W - 1, SC_DIM), wts["w_in_a"], conv_a, wts["w_out_a"],
                               mem_k, mem_v, ln1_g, ln1_b, j=j, i=i)
            new_sc.append(hs)
        else:
            q, k, v, gb, zg, om, hg = _proj_b_seq(
                x, z(bsz, GDN_CONV_W - 1, 3 * GDN_DIM), wts["w_in_b"], conv_b,
                wts["a_row"], wts["dt_row"], mem_k, mem_v, j=j, i=i)
            o, new_gs = _gdn_scan(q, k, v, gb, zg, gdn_norm_w, z(1, bsz, GDN_H, GDN_DK, GDN_DV),
                                  new_gs, j=j, layer=0, c=GDN_CHUNK, bb=SCAN_SEQS, nc=SCAN_CHUNKS)
            rows = bsz * t_len
            x = _out_ln(o.reshape(rows, GDN_DIM), om.reshape(rows, XDIM), wts["w_out_b"],
                        x.reshape(rows, D_MODEL), ln1_g, ln1_b, j=j, i=i).reshape(bsz, t_len, D_MODEL)
            new_gc.append(hg)
        x, hf = _ffn_seq(x, z(bsz, FFN_W - 1, 2 * D_FF), wts["w_up"], w_conv_ffn, wts["w_down"],
                         ln2_g, ln2_b, i=i)
        new_ffn.append(hf)
    return x, jnp.stack(new_sc), jnp.stack(new_gc), new_gs, jnp.stack(new_ffn)


def _trunk_sample(x, mem_k, mem_v, sc_hist, gdn_hist, gdn_s, ffn_hist, wts, conv_a, conv_b,
                  gdn_norm_w, ln1_g, ln1_b, ln2_g, ln2_b, w_conv_ffn):
    bsz, t_len, _ = x.shape
    rows = bsz * t_len
    t_pad = SUBLANES
    x2 = jnp.transpose(x, (1, 0, 2)).reshape(rows, D_MODEL)
    to_bm = lambda a: jnp.transpose(a, (1, 0, 2))
    new_sc, new_gc, new_gs, new_ffn = [], [], None, []
    mem_kt = jnp.transpose(mem_k, (0, 1, 3, 4, 2))
    mem_vt = jnp.transpose(mem_v, (0, 1, 3, 4, 2))
    gdn_hist = jnp.transpose(gdn_hist, (0, 2, 1, 3))
    for i in range(DEPTH):
        j = i // 2
        if i % 2 == 0:
            y, qm, hs = _mix_a_front_sample(x2, sc_hist, wts["w_in_a"], conv_a, j=j,
                                            t_len=t_len, bsz=bsz)
            om = _attn_sample(qm.reshape(t_len, bsz, XDIM), mem_kt, mem_vt, i).reshape(rows, XDIM)
            x2 = _out_ln(y, om, wts["w_out_a"], x2, ln1_g, ln1_b, j=j, i=i)
            new_sc.append(hs)
        else:
            q, k, v, gb, zg, qm, hg = _proj_b_front_sample(
                x2, gdn_hist, wts["w_in_b"], conv_b, wts["a_row"], wts["dt_row"], j=j,
                t_len=t_len, bsz=bsz)
            om = _attn_sample(qm.reshape(t_len, bsz, XDIM), mem_kt, mem_vt, i).reshape(rows, XDIM)
            padt = lambda a: jnp.pad(to_bm(a.reshape(t_len, bsz, -1)), ((0, 0), (0, t_pad - t_len), (0, 0)))
            o, new_gs = _gdn_scan(padt(q), padt(k), padt(v), padt(gb), padt(zg), gdn_norm_w, gdn_s,
                                  new_gs, j=j, layer=j, c=t_pad, bb=SCAN_SEQS_SAMPLE, nc=1)
            o = jnp.transpose(o[:, :t_len], (1, 0, 2)).reshape(rows, GDN_DIM)
            x2 = _out_ln(o, om, wts["w_out_b"], x2, ln1_g, ln1_b, j=j, i=i)
            new_gc.append(hg)
        x2, hf = _ffn_sample(x2, ffn_hist, wts["w_up"], w_conv_ffn, wts["w_down"],
                             ln2_g, ln2_b, i=i, t_len=t_len, bsz=bsz)
        new_ffn.append(hf)
    y = jnp.transpose(x2.reshape(t_len, bsz, D_MODEL), (1, 0, 2))
    gc = jnp.transpose(jnp.stack(new_gc), (0, 2, 1, 3))
    return y, jnp.stack(new_sc), gc, new_gs, jnp.stack(new_ffn)


def kernel(x_prompt, x_sample, mem_prompt, cache_mem_k, cache_mem_v, state_shortconv, state_gdn_conv,
           state_gdn, state_ffn_conv, w_in_a, conv_a, w_out_a, w_in_b, conv_b, a_log, dt_bias,
           gdn_norm_w, w_out_b, w_mem_kv, ln1_g, ln1_b, ln2_g, ln2_b, w_up, w_conv_ffn, w_down):
    wts = _prep_weights(w_in_a, w_out_a, w_in_b, a_log, dt_bias, w_out_b, w_mem_kv, w_up, w_down)
    row3 = lambda a: a.reshape(a.shape[0], 1, a.shape[1])
    shared = (wts, conv_a, conv_b, row3(gdn_norm_w), row3(ln1_g), row3(ln1_b), row3(ln2_g),
              row3(ln2_b), w_conv_ffn)
    bsz = x_prompt.shape[0]
    k2, v2 = _mem_kv(mem_prompt.reshape(bsz * N_MEM, D_MODEL), wts["w_mem_kv"])
    mem_k_prompt = k2.reshape(DEPTH, bsz, N_MEM, XDIM)
    mem_v_prompt = v2.reshape(DEPTH, bsz, N_MEM, XDIM)
    y_prompt, sc_p, gc_p, gs_p, ffn_p = _trunk_prompt(x_prompt, mem_k_prompt, mem_v_prompt, *shared)
    y_sample, sc_s, gc_s, gs_s, ffn_s = _trunk_sample(
        x_sample, cache_mem_k, cache_mem_v, state_shortconv, state_gdn_conv, state_gdn,
        state_ffn_conv, *shared)
    shape5 = (DEPTH, bsz, N_MEM, XH, XD)
    return (y_prompt, y_sample, mem_k_prompt.reshape(shape5), mem_v_prompt.reshape(shape5),
            sc_p, gc_p, gs_p, ffn_p, sc_s, gc_s, gs_s, ffn_s)
```

```python
import functools

import jax
import jax.numpy as jnp
from jax import lax
from jax.experimental import pallas as pl
from jax.experimental.pallas import tpu as pltpu

F32 = jnp.float32
BF16 = jnp.bfloat16

DEPTH = 4
D_MODEL = 1024
SC_DIM = 768
SC_W = 3
GDN_H = 6
GDN_DK = 128
GDN_DV = 128
GDN_DIM = GDN_H * GDN_DK
GDN_CONV_W = 4
GDN_CHUNK = 64
N_MEM = 256
XH = 4
XD = 64
XDIM = XH * XD
D_FF = 2816
FFN_W = 3
ALPHA = (2.0 * DEPTH) ** 0.25
LN_EPS = 1e-5
RMS_EPS = 1e-6

V7X_VMEM_BYTES = 64 * 1024 * 1024
VMEM_LIMIT = V7X_VMEM_BYTES - 8 * 1024 * 1024
SUBLANES = 8
LANES = 128

GB_LANES = LANES

SEQ_TILE = 512
MIX_TILE = 512
MIX_SUBTILES = 2
SCAN_SEQS = 4
SCAN_CHUNKS = 2
SCAN_SEQS_SAMPLE = 8
SOLVE_BLOCK = 16
ROW_TILE = 1024
FF_TILE = 2816
FF_TILE_SAMPLE = 1408
ATTN_BATCH_BLOCK = 16
ATTN_UNROLL = 4
W_B_COLS = 3 * GDN_DIM + GDN_DIM + XDIM + GB_LANES


def _cparams(sem):
    return pltpu.CompilerParams(dimension_semantics=sem, vmem_limit_bytes=VMEM_LIMIT)


def _resident(shape):
    nd = len(shape)
    return pl.BlockSpec(shape, lambda *_: (0,) * nd, pipeline_mode=pl.Buffered(1))


def _layer_block(shape, layer):
    nd = len(shape)
    return pl.BlockSpec((None,) + tuple(shape), lambda *_: (layer,) + (0,) * nd,
                        pipeline_mode=pl.Buffered(1))


def _silu(x):
    return x * (1.0 / (1.0 + jnp.exp(-x)))


def _sigmoid(x):
    return 1.0 / (1.0 + jnp.exp(-x))


def _softplus(x):
    return jnp.maximum(x, 0.0) + jnp.log(1.0 + jnp.exp(-jnp.abs(x)))


def _layer_norm(v, g, b):
    mu = jnp.mean(v, -1, keepdims=True)
    d = v - mu
    var = jnp.mean(d * d, -1, keepdims=True)
    return d * lax.rsqrt(var + LN_EPS) * g + b


def _dot(a, b):
    return jnp.dot(a, b, preferred_element_type=F32)


def _dot_nt(a, b):
    return lax.dot_general(a, b, (((1,), (1,)), ((), ())), preferred_element_type=F32)


def _dot_tn(a, b):
    return lax.dot_general(a, b, (((0,), (0,)), ((), ())), preferred_element_type=F32)


def _kv_kernel(m_ref, w_ref, k_ref, v_ref):
    kv = _dot(m_ref[...].astype(BF16), w_ref[0])
    k_ref[0] = kv[:, :XDIM]
    v_ref[0] = kv[:, XDIM:]


def _mem_kv(mem2d, w_kv):
    rows = mem2d.shape[0]
    out = jax.ShapeDtypeStruct((DEPTH, rows, XDIM), F32)
    return pl.pallas_call(
        _kv_kernel,
        grid=(DEPTH,),
        in_specs=[_resident((rows, D_MODEL)),
                  pl.BlockSpec((1, D_MODEL, 2 * XDIM), lambda l: (l, 0, 0))],
        out_specs=[pl.BlockSpec((1, rows, XDIM), lambda l: (l, 0, 0))] * 2,
        out_shape=[out, out],
        compiler_params=_cparams(("arbitrary",)),
        name="mem_kv",
    )(mem2d, w_kv)


def _head_blockdiag(kv):
    lane_head = lax.broadcasted_iota(jnp.int32, kv.shape, 1) // XD
    return jnp.concatenate(
        [jnp.where(lane_head == h, kv, 0.0).astype(BF16) for h in range(XH)], axis=0)


def _softmax_rows(s):
    m = jnp.max(s, -1, keepdims=True)
    e = jnp.exp(s - m)
    return e / jnp.sum(e, -1, keepdims=True)


def _attn_sample_kernel(q_ref, k_ref, v_ref, o_ref, *, bb, t_len):
    def body(g, carry):
        elems = [g * ATTN_UNROLL + e for e in range(ATTN_UNROLL)]
        q8 = []
        for i in elems:
            rows = [q_ref[t, pl.ds(i, 1), :] for t in range(t_len)]
            rows.append(jnp.zeros((SUBLANES - t_len, XDIM), F32))
            q8.append(jnp.concatenate(rows, axis=0).astype(BF16))
        s = [[_dot(q8[e][:, h * XD:(h + 1) * XD], k_ref[0, i, h].astype(BF16)) * (XD ** -0.5)
              for h in range(XH)] for e, i in enumerate(elems)]
        p = [[_softmax_rows(sh).astype(BF16) for sh in se] for se in s]
        for e, i in enumerate(elems):
            o8 = jnp.concatenate([_dot_nt(p[e][h], v_ref[0, i, h].astype(BF16)) for h in range(XH)],
                                 axis=-1)
            for t in range(t_len):
                o_ref[t, pl.ds(i, 1), :] = o8[t:t + 1]
        return carry

    lax.fori_loop(0, bb // ATTN_UNROLL, body, 0)


def _attn_sample(qm_tm, mem_kt, mem_vt, layer):
    t_len, bsz, _ = qm_tm.shape
    bb = min(ATTN_BATCH_BLOCK, bsz)
    kv = pl.BlockSpec((1, bb, XH, XD, N_MEM), lambda i: (layer, i, 0, 0, 0))
    return pl.pallas_call(
        functools.partial(_attn_sample_kernel, bb=bb, t_len=t_len),
        grid=(bsz // bb,),
        in_specs=[pl.BlockSpec((t_len, bb, XDIM), lambda i: (0, i, 0)), kv, kv],
        out_specs=pl.BlockSpec((t_len, bb, XDIM), lambda i: (0, i, 0)),
        out_shape=jax.ShapeDtypeStruct((t_len, bsz, XDIM), F32),
        compiler_params=_cparams(("arbitrary",)),
        name="attn_sample",
    )(qm_tm, mem_kt, mem_vt)


def _mix_a_seq_kernel(x_ref, hist_ref, win_ref, wc_ref, wout_ref, k_ref, v_ref, g_ref, b_ref,
                      o_ref, hist_o_ref, ubuf, kbd, vbd, *, tm):
    t = pl.program_id(1)
    lo = SUBLANES - (SC_W - 1)

    @pl.when(t == 0)
    def _():
        ubuf[lo:SUBLANES, :] = hist_ref[0]
        kbd[...] = _head_blockdiag(k_ref[0])
        vbd[...] = _head_blockdiag(v_ref[0])

    ts = tm // MIX_SUBTILES
    subs = range(MIX_SUBTILES)
    rows = [slice(s * ts, (s + 1) * ts) for s in subs]
    hs = [_dot(x_ref[0, rows[s], :].astype(BF16), win_ref[...]) for s in subs]
    us = [hs[s][:, 2 * SC_DIM:3 * SC_DIM] * hs[s][:, :SC_DIM] for s in subs]
    for s in subs:
        ubuf[SUBLANES + s * ts:SUBLANES + (s + 1) * ts, :] = us[s]
    wc = wc_ref[...]
    ys = []
    for s in subs:
        conv = wc[SC_W - 1:SC_W] * us[s]
        for j in range(SC_W - 1):
            conv = conv + wc[j:j + 1] * ubuf[lo + j + s * ts:lo + j + (s + 1) * ts, :]
        ys.append((hs[s][:, SC_DIM:2 * SC_DIM] * conv).astype(BF16))
    last = ubuf[tm + lo:tm + SUBLANES, :]
    ubuf[lo:SUBLANES, :] = last
    hist_o_ref[0] = last

    sc = [_dot_nt(hs[s][:, 3 * SC_DIM:].astype(BF16), kbd[...]) * (XD ** -0.5) for s in subs]
    ps = [jnp.concatenate([_softmax_rows(sc[s][:, h * N_MEM:(h + 1) * N_MEM]).astype(BF16)
                           for h in range(XH)], axis=-1) for s in subs]
    oms = [_dot(ps[s], vbd[...]).astype(BF16) for s in subs]
    mix = [_dot(ys[s], wout_ref[:SC_DIM, :]) + _dot(oms[s], wout_ref[SC_DIM:, :]) for s in subs]
    for s in subs:
        o_ref[0, rows[s], :] = _layer_norm(ALPHA * x_ref[0, rows[s], :] + mix[s], g_ref[...], b_ref[...])


def _mix_a_seq(x, hist, w_in, w_conv, w_out, mem_k, mem_v, ln_g, ln_b, *, j, i):
    bsz, t_len, _ = x.shape
    tm = min(MIX_TILE, t_len)
    mem = pl.BlockSpec((None, 1, N_MEM, XDIM), lambda b, t: (i, b, 0, 0))
    return pl.pallas_call(
        functools.partial(_mix_a_seq_kernel, tm=tm),
        grid=(bsz, t_len // tm),
        in_specs=[pl.BlockSpec((1, tm, D_MODEL), lambda b, t: (b, t, 0)),
                  pl.BlockSpec((1, SC_W - 1, SC_DIM), lambda b, t: (b, 0, 0)),
                  _layer_block(w_in.shape[1:], j),
                  _layer_block((SC_W, SC_DIM), j),
                  _layer_block((SC_DIM + XDIM, D_MODEL), j),
                  mem, mem,
                  _layer_block((1, D_MODEL), i),
                  _layer_block((1, D_MODEL), i)],
        out_specs=[pl.BlockSpec((1, tm, D_MODEL), lambda b, t: (b, t, 0)),
                   pl.BlockSpec((1, SC_W - 1, SC_DIM), lambda b, t: (b, 0, 0))],
        out_shape=[jax.ShapeDtypeStruct((bsz, t_len, D_MODEL), F32),
                   jax.ShapeDtypeStruct((bsz, SC_W - 1, SC_DIM), F32)],
        scratch_shapes=[pltpu.VMEM((tm + SUBLANES, SC_DIM), F32),
                        pltpu.VMEM((XH * N_MEM, XDIM), BF16),
                        pltpu.VMEM((XH * N_MEM, XDIM), BF16)],
        compiler_params=_cparams(("arbitrary", "arbitrary")),
        name="mix_a_seq",
    )(x, hist, w_in, w_conv, w_out, mem_k, mem_v, ln_g, ln_b)


def _ffn_seq_kernel(x_ref, hist_ref, wup_ref, wc_ref, wdn_ref, g_ref, b_ref,
                    o_ref, hist_o_ref, gbuf, ubuf, *, tm, tf):
    t = pl.program_id(1)
    lo = SUBLANES - (FFN_W - 1)
    nj = D_FF // tf
    halves = ((gbuf, 0), (ubuf, D_FF))

    @pl.when(t == 0)
    def _():
        for buf, off in halves:
            for j in range(nj):
                buf[j, lo:SUBLANES, :] = hist_ref[0, :, off + j * tf:off + (j + 1) * tf]

    x = x_ref[0]
    xb = x.astype(BF16)

    def up_project(j):
        return [_dot(xb, wup_ref[:, off + j * tf:off + (j + 1) * tf]) for _, off in halves]

    def conv_act(j, hs):
        conv = []
        for (buf, off), h in zip(halves, hs):
            c0 = off + j * tf
            buf[j, SUBLANES:SUBLANES + tm, :] = h
            c = wc_ref[FFN_W - 1:FFN_W, c0:c0 + tf] * h
            for w in range(FFN_W - 1):
                c = c + wc_ref[w:w + 1, c0:c0 + tf] * buf[j, lo + w:lo + w + tm, :]
            last = buf[j, tm + lo:tm + SUBLANES, :]
            buf[j, lo:SUBLANES, :] = last
            hist_o_ref[0, :, c0:c0 + tf] = last
            conv.append(c)
        return (_silu(conv[0]) * conv[1]).astype(BF16)

    acc = None
    hs = up_project(0)
    for j in range(nj):
        hs_next = up_project(j + 1) if j + 1 < nj else None
        d = _dot(conv_act(j, hs), wdn_ref[j * tf:(j + 1) * tf, :])
        acc = d if acc is None else acc + d
        hs = hs_next
    o_ref[0] = _layer_norm(ALPHA * x + acc, g_ref[...], b_ref[...])


def _ffn_seq(x, hist, w_up, w_conv, w_down, ln_g, ln_b, *, i):
    bsz, t_len, _ = x.shape
    tm = min(SEQ_TILE, t_len)
    tf = FF_TILE
    return pl.pallas_call(
        functools.partial(_ffn_seq_kernel, tm=tm, tf=tf),
        grid=(bsz, t_len // tm),
        in_specs=[pl.BlockSpec((1, tm, D_MODEL), lambda b, t: (b, t, 0)),
                  pl.BlockSpec((1, FFN_W - 1, 2 * D_FF), lambda b, t: (b, 0, 0)),
                  _layer_block((D_MODEL, 2 * D_FF), i),
                  _layer_block((FFN_W, 2 * D_FF), i),
                  _layer_block((D_FF, D_MODEL), i),
                  _layer_block((1, D_MODEL), i),
                  _layer_block((1, D_MODEL), i)],
        out_specs=[pl.BlockSpec((1, tm, D_MODEL), lambda b, t: (b, t, 0)),
                   pl.BlockSpec((1, FFN_W - 1, 2 * D_FF), lambda b, t: (b, 0, 0))],
        out_shape=[jax.ShapeDtypeStruct((bsz, t_len, D_MODEL), F32),
                   jax.ShapeDtypeStruct((bsz, FFN_W - 1, 2 * D_FF), F32)],
        scratch_shapes=[pltpu.VMEM((D_FF // tf, tm + SUBLANES, tf), F32),
                        pltpu.VMEM((D_FF // tf, tm + SUBLANES, tf), F32)],
        compiler_params=_cparams(("arbitrary", "arbitrary")),
        name="ffn_seq",
    )(x, hist, w_up, w_conv, w_down, ln_g, ln_b)


def _gdn_gates(ba, a_row, dt_row):
    lane = lax.broadcasted_iota(jnp.int32, ba.shape, 1)
    return jnp.where(lane < GDN_H, _sigmoid(ba), a_row * _softplus(ba + dt_row))


def _qkv_post(c, q_ref, k_ref, v_ref, idx):
    c = _silu(c)
    for h in range(GDN_H):
        for ref, base in ((q_ref, 0), (k_ref, GDN_DIM)):
            a = c[:, base + h * GDN_DK:base + (h + 1) * GDN_DK]
            a = a * lax.rsqrt(jnp.sum(a * a, -1, keepdims=True) + RMS_EPS)
            ref[idx + (slice(h * GDN_DK, (h + 1) * GDN_DK),)] = a
    v_ref[idx + (slice(None),)] = c[:, 2 * GDN_DIM:]


def _proj_b_seq_kernel(x_ref, hist_ref, win_ref, wc_ref, arow_ref, dtrow_ref, k_ref, v_ref,
                       q_o, k_o, v_o, gb_o, zg_o, om_o, hist_o_ref, sbuf, kbd, vbd, *, tm):
    t = pl.program_id(1)
    lo = SUBLANES - (GDN_CONV_W - 1)
    nqkv = 3 * GDN_DIM

    @pl.when(t == 0)
    def _():
        sbuf[lo:SUBLANES, :] = hist_ref[0]
        kbd[...] = _head_blockdiag(k_ref[0])
        vbd[...] = _head_blockdiag(v_ref[0])

    ts = tm // MIX_SUBTILES
    subs = range(MIX_SUBTILES)
    rows = [slice(s * ts, (s + 1) * ts) for s in subs]
    hs = [_dot(x_ref[0, rows[s], :].astype(BF16), win_ref[...]) for s in subs]
    for s in subs:
        sbuf[SUBLANES + s * ts:SUBLANES + (s + 1) * ts, :] = hs[s][:, :nqkv]
    wc = wc_ref[...]
    qm0 = nqkv + GDN_DIM
    sc = []
    for s in subs:
        conv = wc[GDN_CONV_W - 1:GDN_CONV_W] * hs[s][:, :nqkv]
        for j in range(GDN_CONV_W - 1):
            conv = conv + wc[j:j + 1] * sbuf[lo + j + s * ts:lo + j + (s + 1) * ts, :]
        sc.append(_dot_nt(hs[s][:, qm0:qm0 + XDIM].astype(BF16), kbd[...]) * (XD ** -0.5))
        _qkv_post(conv, q_o, k_o, v_o, (0, rows[s]))
        zg_o[0, rows[s], :] = _silu(hs[s][:, nqkv:qm0])
        gb_o[0, rows[s], :] = _gdn_gates(hs[s][:, qm0 + XDIM:], arow_ref[...], dtrow_ref[...])
    last = sbuf[tm + lo:tm + SUBLANES, :]
    sbuf[lo:SUBLANES, :] = last
    hist_o_ref[0] = last
    ps = [jnp.concatenate([_softmax_rows(sc[s][:, h * N_MEM:(h + 1) * N_MEM]).astype(BF16)
                           for h in range(XH)], axis=-1) for s in subs]
    for s in subs:
        om_o[0, rows[s], :] = _dot(ps[s], vbd[...]).astype(BF16)


def _proj_b_seq(x, hist, w_in, w_conv, a_row, dt_row, mem_k, mem_v, *, j, i):
    bsz, t_len, _ = x.shape
    tm = min(MIX_TILE, t_len)
    tile = lambda n: pl.BlockSpec((1, tm, n), lambda b, t: (b, t, 0))
    f32o = lambda n: jax.ShapeDtypeStruct((bsz, t_len, n), F32)
    mem = pl.BlockSpec((None, 1, N_MEM, XDIM), lambda b, t: (i, b, 0, 0))
    return pl.pallas_call(
        functools.partial(_proj_b_seq_kernel, tm=tm),
        grid=(bsz, t_len // tm),
        in_specs=[tile(D_MODEL),
                  pl.BlockSpec((1, GDN_CONV_W - 1, 3 * GDN_DIM), lambda b, t: (b, 0, 0)),
                  _layer_block((D_MODEL, W_B_COLS), j),
                  _layer_block((GDN_CONV_W, 3 * GDN_DIM), j),
                  _layer_block((1, GB_LANES), j),
                  _layer_block((1, GB_LANES), j),
                  mem, mem],
        out_specs=[tile(GDN_DIM), tile(GDN_DIM), tile(GDN_DIM), tile(GB_LANES), tile(GDN_DIM),
                   tile(XDIM),
                   pl.BlockSpec((1, GDN_CONV_W - 1, 3 * GDN_DIM), lambda b, t: (b, 0, 0))],
        out_shape=[f32o(GDN_DIM), f32o(GDN_DIM), f32o(GDN_DIM), f32o(GB_LANES), f32o(GDN_DIM),
                   jax.ShapeDtypeStruct((bsz, t_len, XDIM), BF16),
                   jax.ShapeDtypeStruct((bsz, GDN_CONV_W - 1, 3 * GDN_DIM), F32)],
        scratch_shapes=[pltpu.VMEM((tm + SUBLANES, 3 * GDN_DIM), F32),
                        pltpu.VMEM((XH * N_MEM, XDIM), BF16),
                        pltpu.VMEM((XH * N_MEM, XDIM), BF16)],
        compiler_params=_cparams(("arbitrary", "arbitrary")),
        name="proj_b_seq",
    )(x, hist, w_in, w_conv, a_row, dt_row, mem_k, mem_v)


def _gdn_scan_kernel(q_ref, k_ref, v_ref, gb_ref, zg_ref, nw_ref, s0_ref, prev_ref, o_ref, s_o_ref,
                     s_scr, *, bb, nc, c, n_prev):
    t = pl.program_id(1)
    tc = nc * c

    @pl.when(t == 0)
    def _():
        s_scr[...] = s0_ref[0]
        for n in range(n_prev):
            s_o_ref[n] = prev_ref[n]

    ri = lax.broadcasted_iota(jnp.int32, (c, c), 0)
    ci = lax.broadcasted_iota(jnp.int32, (c, c), 1)
    tril = ri >= ci
    strict = ri > ci
    rt = lax.broadcasted_iota(jnp.int32, (tc, tc), 0)
    ct = lax.broadcasted_iota(jnp.int32, (tc, tc), 1)
    tril_chunks = ((rt >= ct) & ((rt // c) == (ct // c))).astype(BF16)
    head_sel = (lax.broadcasted_iota(jnp.int32, (SUBLANES, GB_LANES), 1)
                == lax.broadcasted_iota(jnp.int32, (SUBLANES, GB_LANES), 0) + GDN_H).astype(BF16)
    nw = nw_ref[...]

    def split3(x):
        a = x.astype(BF16)
        r = x - a.astype(F32)
        b = r.astype(BF16)
        return a, b, (r - b.astype(F32)).astype(BF16)

    gcs = []
    for i in range(bb):
        gc_all = sum(_dot(tril_chunks, p) for p in split3(gb_ref[i]))
        gc_t = sum(_dot_nt(head_sel, p) for p in split3(gc_all))
        gcs.append((gc_all, gc_t))

    probs = [(i, ic, h) for i in range(bb) for ic in range(nc) for h in range(GDN_H)]
    st = {}
    for p in probs:
        i, ic, h = p
        rows = slice(ic * c, (ic + 1) * c)
        hs = slice(h * GDN_DK, (h + 1) * GDN_DK)
        gc_all, gc_t = gcs[i]
        q = q_ref[i, rows, hs] * (GDN_DK ** -0.5)
        k = k_ref[i, rows, hs]
        gcol = gc_all[rows, GDN_H + h:GDN_H + h + 1]
        grow = gc_t[h:h + 1, rows]
        glast = gc_all[(ic + 1) * c - 1:(ic + 1) * c, GDN_H + h:GDN_H + h + 1]
        egc = jnp.exp(gcol)
        kb = k * gb_ref[i, rows, h:h + 1]
        st[p] = dict(
            decay=jnp.exp(jnp.where(tril, gcol - grow, -jnp.inf)),
            aq=_dot_nt(jnp.concatenate([kb, q], axis=0).astype(BF16), k.astype(BF16)),
            sol=jnp.concatenate([v_ref[i, rows, hs] * gb_ref[i, rows, h:h + 1], kb * egc], axis=-1),
            qd=(q * egc).astype(BF16),
            k_dec=(k * jnp.exp(glast - gcol)).astype(BF16),
            e_last=jnp.exp(glast))
    sb = min(SOLVE_BLOCK, c)
    nblk = c // sb
    same_blk = (ri // sb) == (ci // sb)
    lane_sb = lax.broadcasted_iota(jnp.int32, (sb, c), 1)
    eye_ss = ((lane_sb % sb) == lax.broadcasted_iota(jnp.int32, (sb, c), 0)).astype(F32)
    blk_rows = [slice(n * sb, (n + 1) * sb) for n in range(nblk)]
    tile_rows = lambda a: jnp.concatenate([a] * nblk, axis=0) if nblk > 1 else a

    def split2(x):
        hi = x.astype(BF16)
        return jnp.concatenate([hi, (x - hi.astype(F32)).astype(BF16)], axis=-1)

    def fold2(y):
        n = y.shape[-1] // 2
        return y[:, :n] + y[:, n:]

    for p in probs:
        d = st[p]
        d["lmat"] = jnp.where(strict, d["aq"][:c] * d["decay"], 0.0)
        d["qk"] = (d["aq"][c:] * d["decay"]).astype(BF16)
        d["pss"] = sum(jnp.where(lane_sb // sb == n, d["lmat"][blk_rows[n]], 0.0) for n in range(nblk))
        d["pbd"] = jnp.where(same_blk, d["lmat"], 0.0).astype(BF16)
        d["tss"] = eye_ss - d["pss"]
    for _ in range(max(sb.bit_length() - 2, 0)):
        for p in probs:
            st[p]["pss"] = _dot(st[p]["pss"].astype(BF16), st[p]["pbd"])
        for p in probs:
            d = st[p]
            d["pbd"] = jnp.where(same_blk, tile_rows(d["pss"]), 0.0).astype(BF16)
            d["tss"] = d["tss"] + _dot(d["tss"].astype(BF16), d["pbd"])
    zero_blk = jnp.zeros((sb, 4 * GDN_DV), BF16)
    for p in probs:
        d = st[p]
        d["tbd"] = jnp.where(same_blk, tile_rows(d["tss"]), 0.0).astype(BF16)
        d["x2"] = []
        d["x"] = []
    for n in range(nblk):
        for p in probs:
            d = st[p]
            z = d["sol"][blk_rows[n]]
            if n > 0:
                lrow = jnp.where(lane_sb < n * sb, d["lmat"][blk_rows[n]], 0.0).astype(BF16)
                z = z - fold2(_dot(lrow, jnp.concatenate(d["x2"] + [zero_blk] * (nblk - n), axis=0)))
            d["y"] = jnp.concatenate([zero_blk] * n + [split2(z)] + [zero_blk] * (nblk - n - 1), axis=0)
        for p in probs:
            d = st[p]
            x = fold2(_dot(d["tbd"][blk_rows[n]], d["y"]))
            d["x"].append(x)
            if n + 1 < nblk:
                d["x2"].append(split2(x))
    for p in probs:
        d = st[p]
        sol = jnp.concatenate(d["x"], axis=0) if nblk > 1 else d["x"][0]
        d["wq"] = jnp.concatenate([sol[:, GDN_DV:].astype(BF16), d["qd"]], axis=0)
        d["u"] = sol[:, :GDN_DV]

    for ic in range(nc):
        rows = slice(ic * c, (ic + 1) * c)
        seqs = [(i, h) for i in range(bb) for h in range(GDN_H)]
        s_old = {ih: s_scr[ih[0], ih[1]] for ih in seqs}
        ws = {ih: _dot(st[ih[0], ic, ih[1]]["wq"], s_old[ih].astype(BF16)) for ih in seqs}
        vb = {ih: (st[ih[0], ic, ih[1]]["u"] - ws[ih][:c]).astype(BF16) for ih in seqs}
        for ih in seqs:
            d = st[ih[0], ic, ih[1]]
            s_scr[ih[0], ih[1]] = s_old[ih] * d["e_last"] + _dot_tn(d["k_dec"], vb[ih])
        for ih in seqs:
            i, h = ih
            hs = slice(h * GDN_DK, (h + 1) * GDN_DK)
            o = ws[ih][c:] + _dot(st[i, ic, h]["qk"], vb[ih])
            o = o * lax.rsqrt(jnp.mean(o * o, -1, keepdims=True) + RMS_EPS)
            o_ref[i, rows, hs] = (o * nw * zg_ref[i, rows, hs]).astype(o_ref.dtype)
    s_o_ref[n_prev] = s_scr[...]


def _gdn_scan(q, k, v, gb, zg, norm_w, s0_all, prev, *, j, layer, c, bb, nc):
    bsz, t_len, _ = q.shape
    tc = nc * c
    n_prev = 0 if prev is None else prev.shape[0]
    tile = lambda n: pl.BlockSpec((bb, tc, n), lambda b, t: (b, t, 0))
    st = lambda n: pl.BlockSpec((n, bb, GDN_H, GDN_DK, GDN_DV), lambda b, t: (0, b, 0, 0, 0))
    st_in = pl.BlockSpec((1, bb, GDN_H, GDN_DK, GDN_DV), lambda b, t: (layer, b, 0, 0, 0))
    if prev is None:
        prev, prev_spec = s0_all, pl.BlockSpec(memory_space=pl.ANY)
    else:
        prev_spec = st(n_prev)
    return pl.pallas_call(
        functools.partial(_gdn_scan_kernel, bb=bb, nc=nc, c=c, n_prev=n_prev),
        grid=(bsz // bb, t_len // tc),
        in_specs=[tile(GDN_DIM), tile(GDN_DIM), tile(GDN_DIM), tile(GB_LANES), tile(GDN_DIM),
                  _layer_block((1, GDN_DV), j), st_in, prev_spec],
        out_specs=[tile(GDN_DIM), st(n_prev + 1)],
        out_shape=[jax.ShapeDtypeStruct((bsz, t_len, GDN_DIM), BF16),
                   jax.ShapeDtypeStruct((n_prev + 1, bsz, GDN_H, GDN_DK, GDN_DV), F32)],
        scratch_shapes=[pltpu.VMEM((bb, GDN_H, GDN_DK, GDN_DV), F32)],
        compiler_params=_cparams(("arbitrary", "arbitrary")),
        name="gdn_scan",
    )(q, k, v, gb, zg, norm_w, s0_all, prev)


def _out_ln_kernel(a1_ref, a2_ref, w_ref, x_ref, g_ref, b_ref, o_ref):
    n1 = a1_ref.shape[-1]
    mix = (_dot(a1_ref[...].astype(BF16), w_ref[:n1, :])
           + _dot(a2_ref[...].astype(BF16), w_ref[n1:, :]))
    o_ref[...] = _layer_norm(ALPHA * x_ref[...] + mix, g_ref[...], b_ref[...])


def _out_ln(a1, a2, w, x, ln_g, ln_b, *, j, i):
    rows = x.shape[0]
    tr = min(ROW_TILE, rows)
    n1, n2 = a1.shape[1], a2.shape[1]
    tile = lambda n: pl.BlockSpec((tr, n), lambda r: (r, 0))
    return pl.pallas_call(
        _out_ln_kernel,
        grid=(rows // tr,),
        in_specs=[tile(n1), tile(n2), _layer_block((n1 + n2, D_MODEL), j), tile(D_MODEL),
                  _layer_block((1, D_MODEL), i), _layer_block((1, D_MODEL), i)],
        out_specs=tile(D_MODEL),
        out_shape=jax.ShapeDtypeStruct((rows, D_MODEL), F32),
        compiler_params=_cparams(("arbitrary",)),
        name="out_ln",
    )(a1, a2, w, x, ln_g, ln_b)


def _mix_a_front_sample_kernel(x_ref, hist_ref, win_ref, wc_ref, y_o, qm_o, hist_o, *, t_len, bsz):
    h = _dot(x_ref[...].astype(BF16), win_ref[...])
    u = h[:, 2 * SC_DIM:3 * SC_DIM] * h[:, :SC_DIM]
    slabs = [hist_ref[:, j, :] for j in range(SC_W - 1)]
    slabs += [u[t * bsz:(t + 1) * bsz] for t in range(t_len)]
    wc = wc_ref[...]
    for t in range(t_len):
        conv = wc[0:1] * slabs[t]
        for j in range(1, SC_W):
            conv = conv + wc[j:j + 1] * slabs[t + j]
        y_o[t * bsz:(t + 1) * bsz, :] = (h[t * bsz:(t + 1) * bsz, SC_DIM:2 * SC_DIM] * conv).astype(BF16)
    for j in range(SC_W - 1):
        hist_o[:, j, :] = slabs[t_len + j]
    qm_o[...] = h[:, 3 * SC_DIM:]


def _whole(shape):
    nd = len(shape)
    return pl.BlockSpec(tuple(shape), lambda *_: (0,) * nd)


def _mix_a_front_sample(x2d, hist, w_in, w_conv, *, j, t_len, bsz):
    rows = x2d.shape[0]
    out_shape = [jax.ShapeDtypeStruct((rows, SC_DIM), BF16),
                 jax.ShapeDtypeStruct((rows, XDIM), F32),
                 jax.ShapeDtypeStruct((bsz, SC_W - 1, SC_DIM), F32)]
    return pl.pallas_call(
        functools.partial(_mix_a_front_sample_kernel, t_len=t_len, bsz=bsz),
        grid=(1,),
        in_specs=[_whole(x2d.shape), _layer_block(hist.shape[1:], j), _layer_block(w_in.shape[1:], j),
                  _layer_block(w_conv.shape[1:], j)],
        out_specs=[_whole(s.shape) for s in out_shape],
        out_shape=out_shape,
        compiler_params=_cparams(("arbitrary",)),
        name="mix_a_front_sample",
    )(x2d, hist, w_in, w_conv)


def _proj_b_front_sample_kernel(x_ref, hist_ref, win_ref, wc_ref, arow_ref, dtrow_ref,
                                q_o, k_o, v_o, gb_o, zg_o, qm_o, hist_o, *, t_len, bsz):
    nqkv = 3 * GDN_DIM
    h = _dot(x_ref[...].astype(BF16), win_ref[...])
    slabs = [hist_ref[j] for j in range(GDN_CONV_W - 1)]
    slabs += [h[t * bsz:(t + 1) * bsz, :nqkv] for t in range(t_len)]
    wc = wc_ref[...]
    for t in range(t_len):
        conv = wc[0:1] * slabs[t]
        for j in range(1, GDN_CONV_W):
            conv = conv + wc[j:j + 1] * slabs[t + j]
        _qkv_post(conv, q_o, k_o, v_o, (t, slice(None)))
    for j in range(GDN_CONV_W - 1):
        hist_o[j] = slabs[t_len + j]
    zg_o[...] = _silu(h[:, nqkv:nqkv + GDN_DIM])
    qm_o[...] = h[:, nqkv + GDN_DIM:nqkv + GDN_DIM + XDIM]
    gb_o[...] = _gdn_gates(h[:, nqkv + GDN_DIM + XDIM:], arow_ref[...], dtrow_ref[...])


def _proj_b_front_sample(x2d, hist, w_in, w_conv, a_row, dt_row, *, j, t_len, bsz):
    rows = x2d.shape[0]
    tmaj = jax.ShapeDtypeStruct((t_len, bsz, GDN_DIM), F32)
    out_shape = [tmaj, tmaj, tmaj,
                 jax.ShapeDtypeStruct((rows, GB_LANES), F32),
                 jax.ShapeDtypeStruct((rows, GDN_DIM), F32),
                 jax.ShapeDtypeStruct((rows, XDIM), F32),
                 jax.ShapeDtypeStruct((GDN_CONV_W - 1, bsz, 3 * GDN_DIM), F32)]
    return pl.pallas_call(
        functools.partial(_proj_b_front_sample_kernel, t_len=t_len, bsz=bsz),
        grid=(1,),
        in_specs=[_whole(x2d.shape), _layer_block(hist.shape[1:], j), _layer_block(w_in.shape[1:], j),
                  _layer_block(w_conv.shape[1:], j), _layer_block((1, GB_LANES), j),
                  _layer_block((1, GB_LANES), j)],
        out_specs=[_whole(s.shape) for s in out_shape],
        out_shape=out_shape,
        compiler_params=_cparams(("arbitrary",)),
        name="proj_b_front_sample",
    )(x2d, hist, w_in, w_conv, a_row, dt_row)


def _ffn_sample_kernel(x_ref, hist_ref, w_ref, c_ref, wdn_ref, g_ref, b_ref, o_ref, hist_o,
                       acc, gate, *, t_len, bsz, nj):
    s = pl.program_id(0)
    h = _dot(x_ref[...].astype(BF16), w_ref[...])
    slabs = [hist_ref[:, r, :] for r in range(FFN_W - 1)]
    slabs += [h[t * bsz:(t + 1) * bsz] for t in range(t_len)]
    wc = c_ref[...]
    outs = []
    for t in range(t_len):
        c = wc[0:1] * slabs[t]
        for w in range(1, FFN_W):
            c = c + wc[w:w + 1] * slabs[t + w]
        outs.append(c)
    for r in range(FFN_W - 1):
        hist_o[:, r, :] = slabs[t_len + r]
    conv = jnp.concatenate(outs, axis=0)

    @pl.when(s < nj)
    def _():
        gate[s] = _silu(conv)

    @pl.when(s >= nj)
    def _():
        d = _dot((gate[s - nj] * conv).astype(BF16), wdn_ref[...])

        @pl.when(s == nj)
        def _():
            acc[...] = d

        @pl.when(s > nj)
        def _():
            acc[...] += d

    @pl.when(s == 2 * nj - 1)
    def _():
        o_ref[...] = _layer_norm(ALPHA * x_ref[...] + acc[...], g_ref[...], b_ref[...])


def _ffn_sample(x2d, hist, w_up, w_conv, w_down, ln_g, ln_b, *, i, t_len, bsz):
    rows = x2d.shape[0]
    tf = FF_TILE_SAMPLE
    nj = D_FF // tf
    return pl.pallas_call(
        functools.partial(_ffn_sample_kernel, t_len=t_len, bsz=bsz, nj=nj),
        grid=(2 * nj,),
        in_specs=[_resident((rows, D_MODEL)),
                  pl.BlockSpec((None, bsz, FFN_W - 1, tf), lambda s: (i, 0, 0, s)),
                  pl.BlockSpec((None, D_MODEL, tf), lambda s: (i, 0, s)),
                  pl.BlockSpec((None, FFN_W, tf), lambda s: (i, 0, s)),
                  pl.BlockSpec((None, tf, D_MODEL), lambda s: (i, jnp.maximum(s - nj, 0), 0)),
                  _layer_block((1, D_MODEL), i), _layer_block((1, D_MODEL), i)],
        out_specs=[pl.BlockSpec((rows, D_MODEL), lambda s: (0, 0)),
                   pl.BlockSpec((bsz, FFN_W - 1, tf), lambda s: (0, 0, s))],
        out_shape=[jax.ShapeDtypeStruct((rows, D_MODEL), F32),
                   jax.ShapeDtypeStruct((bsz, FFN_W - 1, 2 * D_FF), F32)],
        scratch_shapes=[pltpu.VMEM((rows, D_MODEL), F32), pltpu.VMEM((nj, rows, tf), F32)],
        compiler_params=_cparams(("arbitrary",)),
        name="ffn_sample",
    )(x2d, hist, w_up, w_conv, w_down, ln_g, ln_b)


def _prep_weights(w_in_a, w_out_a, w_in_b, a_log, dt_bias, w_out_b, w_mem_kv, w_up, w_down):
    nqkvz = 4 * GDN_DIM
    w_b = jnp.concatenate(
        [w_in_b[..., :nqkvz], w_in_b[..., nqkvz + 2 * GDN_H:],
         jnp.pad(w_in_b[..., nqkvz:nqkvz + 2 * GDN_H], ((0, 0), (0, 0), (0, GB_LANES - 2 * GDN_H)))],
        axis=-1).astype(BF16)
    pad = GB_LANES - 2 * GDN_H
    a_row = jnp.pad(-jnp.exp(a_log.astype(F32)), ((0, 0), (GDN_H, pad)))[:, None, :]
    dt_row = jnp.pad(dt_bias.astype(F32), ((0, 0), (GDN_H, pad)))[:, None, :]
    return dict(w_in_a=w_in_a.astype(BF16), w_out_a=w_out_a.astype(BF16), w_in_b=w_b,
                w_out_b=w_out_b.astype(BF16), w_mem_kv=w_mem_kv.astype(BF16),
                w_up=w_up.astype(BF16), w_down=w_down.astype(BF16), a_row=a_row, dt_row=dt_row)


def _trunk_prompt(x, mem_k, mem_v, wts, conv_a, conv_b, gdn_norm_w, ln1_g, ln1_b, ln2_g, ln2_b,
                  w_conv_ffn):
    bsz, t_len, _ = x.shape
    z = lambda *s: jnp.zeros(s, F32)
    new_sc, new_gc, new_gs, new_ffn = [], [], None, []
    for i in range(DEPTH):
        j = i // 2
        if i % 2 == 0:
            x, hs = _mix_a_seq(x, z(bsz, SC_W - 1, SC_DIM), wts["w_in_a"], conv_a, wts["w_out_a"],
                               mem_k, mem_v, ln1_g, ln1_b, j=j, i=i)
            new_sc.append(hs)
        else:
            q, k, v, gb, zg, om, hg = _proj_b_seq(
                x, z(bsz, GDN_CONV_W - 1, 3 * GDN_DIM), wts["w_in_b"], conv_b,
                wts["a_row"], wts["dt_row"], mem_k, mem_v, j=j, i=i)
            o, new_gs = _gdn_scan(q, k, v, gb, zg, gdn_norm_w, z(1, bsz, GDN_H, GDN_DK, GDN_DV),
                                  new_gs, j=j, layer=0, c=GDN_CHUNK, bb=SCAN_SEQS, nc=SCAN_CHUNKS)
            rows = bsz * t_len
            x = _out_ln(o.reshape(rows, GDN_DIM), om.reshape(rows, XDIM), wts["w_out_b"],
                        x.reshape(rows, D_MODEL), ln1_g, ln1_b, j=j, i=i).reshape(bsz, t_len, D_MODEL)
            new_gc.append(hg)
        x, hf = _ffn_seq(x, z(bsz, FFN_W - 1, 2 * D_FF), wts["w_up"], w_conv_ffn, wts["w_down"],
                         ln2_g, ln2_b, i=i)
        new_ffn.append(hf)
    return x, jnp.stack(new_sc), jnp.stack(new_gc), new_gs, jnp.stack(new_ffn)


def _trunk_sample(x, mem_k, mem_v, sc_hist, gdn_hist, gdn_s, ffn_hist, wts, conv_a, conv_b,
                  gdn_norm_w, ln1_g, ln1_b, ln2_g, ln2_b, w_conv_ffn):
    bsz, t_len, _ = x.shape
    rows = bsz * t_len
    t_pad = SUBLANES
    x2 = jnp.transpose(x, (1, 0, 2)).reshape(rows, D_MODEL)
    to_bm = lambda a: jnp.transpose(a, (1, 0, 2))
    new_sc, new_gc, new_gs, new_ffn = [], [], None, []
    mem_kt = jnp.transpose(mem_k, (0, 1, 3, 4, 2))
    mem_vt = jnp.transpose(mem_v, (0, 1, 3, 4, 2))
    gdn_hist = jnp.transpose(gdn_hist, (0, 2, 1, 3))
    for i in range(DEPTH):
        j = i // 2
        if i % 2 == 0:
            y, qm, hs = _mix_a_front_sample(x2, sc_hist, wts["w_in_a"], conv_a, j=j,
                                            t_len=t_len, bsz=bsz)
            om = _attn_sample(qm.reshape(t_len, bsz, XDIM), mem_kt, mem_vt, i).reshape(rows, XDIM)
            x2 = _out_ln(y, om, wts["w_out_a"], x2, ln1_g, ln1_b, j=j, i=i)
            new_sc.append(hs)
        else:
            q, k, v, gb, zg, qm, hg = _proj_b_front_sample(
                x2, gdn_hist, wts["w_in_b"], conv_b, wts["a_row"], wts["dt_row"], j=j,
                t_len=t_len, bsz=bsz)
            om = _attn_sample(qm.reshape(t_len, bsz, XDIM), mem_kt, mem_vt, i).reshape(rows, XDIM)
            padt = lambda a: jnp.pad(to_bm(a.reshape(t_len, bsz, -1)), ((0, 0), (0, t_pad - t_len), (0, 0)))
            o, new_gs = _gdn_scan(padt(q), padt(k), padt(v), padt(gb), padt(zg), gdn_norm_w, gdn_s,
                                  new_gs, j=j, layer=j, c=t_pad, bb=SCAN_SEQS_SAMPLE, nc=1)
            o = jnp.transpose(o[:, :t_len], (1, 0, 2)).reshape(rows, GDN_DIM)
            x2 = _out_ln(o, om, wts["w_out_b"], x2, ln1_g, ln1_b, j=j, i=i)
            new_gc.append(hg)
        x2, hf = _ffn_sample(x2, ffn_hist, wts["w_up"], w_conv_ffn, wts["w_down"],
                             ln2_g, ln2_b, i=i, t_len=t_len, bsz=bsz)
        new_ffn.append(hf)
    y = jnp.transpose(x2.reshape(t_len, bsz, D_MODEL), (1, 0, 2))
    gc = jnp.transpose(jnp.stack(new_gc), (0, 2, 1, 3))
    return y, jnp.stack(new_sc), gc, new_gs, jnp.stack(new_ffn)


def kernel(x_prompt, x_sample, mem_prompt, cache_mem_k, cache_mem_v, state_shortconv, state_gdn_conv,
           state_gdn, state_ffn_conv, w_in_a, conv_a, w_out_a, w_in_b, conv_b, a_log, dt_bias,
           gdn_norm_w, w_out_b, w_mem_kv, ln1_g, ln1_b, ln2_g, ln2_b, w_up, w_conv_ffn, w_down):
    wts = _prep_weights(w_in_a, w_out_a, w_in_b, a_log, dt_bias, w_out_b, w_mem_kv, w_up, w_down)
    row3 = lambda a: a.reshape(a.shape[0], 1, a.shape[1])
    shared = (wts, conv_a, conv_b, row3(gdn_norm_w), row3(ln1_g), row3(ln1_b), row3(ln2_g),
              row3(ln2_b), w_conv_ffn)
    bsz = x_prompt.shape[0]
    k2, v2 = _mem_kv(mem_prompt.reshape(bsz * N_MEM, D_MODEL), wts["w_mem_kv"])
    mem_k_prompt = k2.reshape(DEPTH, bsz, N_MEM, XDIM)
    mem_v_prompt = v2.reshape(DEPTH, bsz, N_MEM, XDIM)
    y_prompt, sc_p, gc_p, gs_p, ffn_p = _trunk_prompt(x_prompt, mem_k_prompt, mem_v_prompt, *shared)
    y_sample, sc_s, gc_s, gs_s, ffn_s = _trunk_sample(
        x_sample, cache_mem_k, cache_mem_v, state_shortconv, state_gdn_conv, state_gdn,
        state_ffn_conv, *shared)
    shape5 = (DEPTH, bsz, N_MEM, XH, XD)
    return (y_prompt, y_sample, mem_k_prompt.reshape(shape5), mem_v_prompt.reshape(shape5),
            sc_p, gc_p, gs_p, ffn_p, sc_s, gc_s, gs_s, ffn_s)
```

```python
import functools

import jax
import jax.numpy as jnp
from jax import lax
from jax.experimental import pallas as pl
from jax.experimental.pallas import tpu as pltpu

F32 = jnp.float32
BF16 = jnp.bfloat16

DEPTH = 4
D_MODEL = 1024
SC_DIM = 768
SC_W = 3
GDN_H = 6
GDN_DK = 128
GDN_DV = 128
GDN_DIM = GDN_H * GDN_DK
GDN_CONV_W = 4
GDN_CHUNK = 64
N_MEM = 256
XH = 4
XD = 64
XDIM = XH * XD
D_FF = 2816
FFN_W = 3
ALPHA = (2.0 * DEPTH) ** 0.25
LN_EPS = 1e-5
RMS_EPS = 1e-6

V7X_VMEM_BYTES = 64 * 1024 * 1024
VMEM_LIMIT = V7X_VMEM_BYTES - 8 * 1024 * 1024
SUBLANES = 8
LANES = 128

GB_LANES = LANES

SEQ_TILE = 512
MIX_TILE = 512
MIX_A_TILE = 1024
MIX_SUBTILES = 2
SCAN_SEQS = 4
SCAN_CHUNKS = 2
SCAN_SEQS_SAMPLE = 8
SOLVE_BLOCK = 16
ROW_TILE = 1024
FF_TILE = 2816
FF_TILE_SAMPLE = 1408
ATTN_BATCH_BLOCK = 16
ATTN_UNROLL = 4
W_B_COLS = 3 * GDN_DIM + GDN_DIM + XDIM + GB_LANES


def _cparams(sem):
    return pltpu.CompilerParams(dimension_semantics=sem, vmem_limit_bytes=VMEM_LIMIT)


def _resident(shape):
    nd = len(shape)
    return pl.BlockSpec(shape, lambda *_: (0,) * nd, pipeline_mode=pl.Buffered(1))


def _layer_block(shape, layer):
    nd = len(shape)
    return pl.BlockSpec((None,) + tuple(shape), lambda *_: (layer,) + (0,) * nd,
                        pipeline_mode=pl.Buffered(1))


def _silu(x):
    return x * (1.0 / (1.0 + jnp.exp(-x)))


def _sigmoid(x):
    return 1.0 / (1.0 + jnp.exp(-x))


def _softplus(x):
    return jnp.maximum(x, 0.0) + jnp.log(1.0 + jnp.exp(-jnp.abs(x)))


def _layer_norm(v, g, b):
    mu = jnp.mean(v, -1, keepdims=True)
    d = v - mu
    var = jnp.mean(d * d, -1, keepdims=True)
    return d * lax.rsqrt(var + LN_EPS) * g + b


def _dot(a, b):
    return jnp.dot(a, b, preferred_element_type=F32)


def _dot_nt(a, b):
    return lax.dot_general(a, b, (((1,), (1,)), ((), ())), preferred_element_type=F32)


def _dot_tn(a, b):
    return lax.dot_general(a, b, (((0,), (0,)), ((), ())), preferred_element_type=F32)


def _kv_kernel(m_ref, w_ref, k_ref, v_ref):
    kv = _dot(m_ref[...].astype(BF16), w_ref[0])
    k_ref[0] = kv[:, :XDIM]
    v_ref[0] = kv[:, XDIM:]


def _mem_kv(mem2d, w_kv):
    rows = mem2d.shape[0]
    out = jax.ShapeDtypeStruct((DEPTH, rows, XDIM), F32)
    return pl.pallas_call(
        _kv_kernel,
        grid=(DEPTH,),
        in_specs=[_resident((rows, D_MODEL)),
                  pl.BlockSpec((1, D_MODEL, 2 * XDIM), lambda l: (l, 0, 0))],
        out_specs=[pl.BlockSpec((1, rows, XDIM), lambda l: (l, 0, 0))] * 2,
        out_shape=[out, out],
        compiler_params=_cparams(("arbitrary",)),
        name="mem_kv",
    )(mem2d, w_kv)


def _head_blockdiag(kv):
    lane_head = lax.broadcasted_iota(jnp.int32, kv.shape, 1) // XD
    return jnp.concatenate(
        [jnp.where(lane_head == h, kv, 0.0).astype(BF16) for h in range(XH)], axis=0)


def _softmax_rows(s):
    m = jnp.max(s, -1, keepdims=True)
    e = jnp.exp(s - m)
    return e / jnp.sum(e, -1, keepdims=True)


def _attn_sample_kernel(q_ref, k_ref, v_ref, o_ref, *, bb, t_len):
    def body(g, carry):
        elems = [g * ATTN_UNROLL + e for e in range(ATTN_UNROLL)]
        q8 = []
        for i in elems:
            rows = [q_ref[t, pl.ds(i, 1), :] for t in range(t_len)]
            rows.append(jnp.zeros((SUBLANES - t_len, XDIM), F32))
            q8.append(jnp.concatenate(rows, axis=0).astype(BF16))
        s = [[_dot(q8[e][:, h * XD:(h + 1) * XD], k_ref[0, i, h].astype(BF16)) * (XD ** -0.5)
              for h in range(XH)] for e, i in enumerate(elems)]
        p = [[_softmax_rows(sh).astype(BF16) for sh in se] for se in s]
        for e, i in enumerate(elems):
            o8 = jnp.concatenate([_dot_nt(p[e][h], v_ref[0, i, h].astype(BF16)) for h in range(XH)],
                                 axis=-1)
            for t in range(t_len):
                o_ref[t, pl.ds(i, 1), :] = o8[t:t + 1]
        return carry

    lax.fori_loop(0, bb // ATTN_UNROLL, body, 0)


def _attn_sample(qm_tm, mem_kt, mem_vt, layer):
    t_len, bsz, _ = qm_tm.shape
    bb = min(ATTN_BATCH_BLOCK, bsz)
    kv = pl.BlockSpec((1, bb, XH, XD, N_MEM), lambda i: (layer, i, 0, 0, 0))
    return pl.pallas_call(
        functools.partial(_attn_sample_kernel, bb=bb, t_len=t_len),
        grid=(bsz // bb,),
        in_specs=[pl.BlockSpec((t_len, bb, XDIM), lambda i: (0, i, 0)), kv, kv],
        out_specs=pl.BlockSpec((t_len, bb, XDIM), lambda i: (0, i, 0)),
        out_shape=jax.ShapeDtypeStruct((t_len, bsz, XDIM), F32),
        compiler_params=_cparams(("arbitrary",)),
        name="attn_sample",
    )(qm_tm, mem_kt, mem_vt)


def _mix_a_seq_kernel(x_ref, hist_ref, win_ref, wc_ref, wout_ref, k_ref, v_ref, g_ref, b_ref,
                      o_ref, hist_o_ref, ubuf, kbd, vbd, *, tm):
    t = pl.program_id(1)
    lo = SUBLANES - (SC_W - 1)

    @pl.when(t == 0)
    def _():
        ubuf[lo:SUBLANES, :] = hist_ref[0]
        kbd[...] = _head_blockdiag(k_ref[0])
        vbd[...] = _head_blockdiag(v_ref[0])

    ts = tm // MIX_SUBTILES
    subs = range(MIX_SUBTILES)
    rows = [slice(s * ts, (s + 1) * ts) for s in subs]
    hs = [_dot(x_ref[0, rows[s], :].astype(BF16), win_ref[...]) for s in subs]
    us = [hs[s][:, 2 * SC_DIM:3 * SC_DIM] * hs[s][:, :SC_DIM] for s in subs]
    for s in subs:
        ubuf[SUBLANES + s * ts:SUBLANES + (s + 1) * ts, :] = us[s]
    wc = wc_ref[...]
    ys = []
    for s in subs:
        conv = wc[SC_W - 1:SC_W] * us[s]
        for j in range(SC_W - 1):
            conv = conv + wc[j:j + 1] * ubuf[lo + j + s * ts:lo + j + (s + 1) * ts, :]
        ys.append((hs[s][:, SC_DIM:2 * SC_DIM] * conv).astype(BF16))
    last = ubuf[tm + lo:tm + SUBLANES, :]
    ubuf[lo:SUBLANES, :] = last
    hist_o_ref[0] = last

    sc = [_dot_nt(hs[s][:, 3 * SC_DIM:].astype(BF16), kbd[...]) * (XD ** -0.5) for s in subs]
    ps = [jnp.concatenate([_softmax_rows(sc[s][:, h * N_MEM:(h + 1) * N_MEM]).astype(BF16)
                           for h in range(XH)], axis=-1) for s in subs]
    oms = [_dot(ps[s], vbd[...]).astype(BF16) for s in subs]
    mix = [_dot(ys[s], wout_ref[:SC_DIM, :]) + _dot(oms[s], wout_ref[SC_DIM:, :]) for s in subs]
    for s in subs:
        o_ref[0, rows[s], :] = _layer_norm(ALPHA * x_ref[0, rows[s], :] + mix[s], g_ref[...], b_ref[...])


def _mix_a_seq(x, hist, w_in, w_conv, w_out, mem_k, mem_v, ln_g, ln_b, *, j, i):
    bsz, t_len, _ = x.shape
    tm = min(MIX_A_TILE, t_len)
    mem = pl.BlockSpec((None, 1, N_MEM, XDIM), lambda b, t: (i, b, 0, 0))
    return pl.pallas_call(
        functools.partial(_mix_a_seq_kernel, tm=tm),
        grid=(bsz, t_len // tm),
        in_specs=[pl.BlockSpec((1, tm, D_MODEL), lambda b, t: (b, t, 0)),
                  pl.BlockSpec((1, SC_W - 1, SC_DIM), lambda b, t: (b, 0, 0)),
                  _layer_block(w_in.shape[1:], j),
                  _layer_block((SC_W, SC_DIM), j),
                  _layer_block((SC_DIM + XDIM, D_MODEL), j),
                  mem, mem,
                  _layer_block((1, D_MODEL), i),
                  _layer_block((1, D_MODEL), i)],
        out_specs=[pl.BlockSpec((1, tm, D_MODEL), lambda b, t: (b, t, 0)),
                   pl.BlockSpec((1, SC_W - 1, SC_DIM), lambda b, t: (b, 0, 0))],
        out_shape=[jax.ShapeDtypeStruct((bsz, t_len, D_MODEL), F32),
                   jax.ShapeDtypeStruct((bsz, SC_W - 1, SC_DIM), F32)],
        scratch_shapes=[pltpu.VMEM((tm + SUBLANES, SC_DIM), F32),
                        pltpu.VMEM((XH * N_MEM, XDIM), BF16),
                        pltpu.VMEM((XH * N_MEM, XDIM), BF16)],
        compiler_params=_cparams(("arbitrary", "arbitrary")),
        name="mix_a_seq",
    )(x, hist, w_in, w_conv, w_out, mem_k, mem_v, ln_g, ln_b)


def _ffn_seq_kernel(x_ref, hist_ref, wup_ref, wc_ref, wdn_ref, g_ref, b_ref,
                    o_ref, hist_o_ref, gbuf, ubuf, *, tm, tf):
    t = pl.program_id(1)
    lo = SUBLANES - (FFN_W - 1)
    nj = D_FF // tf
    halves = ((gbuf, 0), (ubuf, D_FF))

    @pl.when(t == 0)
    def _():
        for buf, off in halves:
            for j in range(nj):
                buf[j, lo:SUBLANES, :] = hist_ref[0, :, off + j * tf:off + (j + 1) * tf]

    x = x_ref[0]
    xb = x.astype(BF16)

    def up_project(j):
        return [_dot(xb, wup_ref[:, off + j * tf:off + (j + 1) * tf]) for _, off in halves]

    def conv_act(j, hs):
        conv = []
        for (buf, off), h in zip(halves, hs):
            c0 = off + j * tf
            buf[j, SUBLANES:SUBLANES + tm, :] = h
            c = wc_ref[FFN_W - 1:FFN_W, c0:c0 + tf] * h
            for w in range(FFN_W - 1):
                c = c + wc_ref[w:w + 1, c0:c0 + tf] * buf[j, lo + w:lo + w + tm, :]
            last = buf[j, tm + lo:tm + SUBLANES, :]
            buf[j, lo:SUBLANES, :] = last
            hist_o_ref[0, :, c0:c0 + tf] = last
            conv.append(c)
        return (_silu(conv[0]) * conv[1]).astype(BF16)

    acc = None
    hs = up_project(0)
    for j in range(nj):
        hs_next = up_project(j + 1) if j + 1 < nj else None
        d = _dot(conv_act(j, hs), wdn_ref[j * tf:(j + 1) * tf, :])
        acc = d if acc is None else acc + d
        hs = hs_next
    o_ref[0] = _layer_norm(ALPHA * x + acc, g_ref[...], b_ref[...])


def _ffn_seq(x, hist, w_up, w_conv, w_down, ln_g, ln_b, *, i):
    bsz, t_len, _ = x.shape
    tm = min(SEQ_TILE, t_len)
    tf = FF_TILE
    return pl.pallas_call(
        functools.partial(_ffn_seq_kernel, tm=tm, tf=tf),
        grid=(bsz, t_len // tm),
        in_specs=[pl.BlockSpec((1, tm, D_MODEL), lambda b, t: (b, t, 0)),
                  pl.BlockSpec((1, FFN_W - 1, 2 * D_FF), lambda b, t: (b, 0, 0)),
                  _layer_block((D_MODEL, 2 * D_FF), i),
                  _layer_block((FFN_W, 2 * D_FF), i),
                  _layer_block((D_FF, D_MODEL), i),
                  _layer_block((1, D_MODEL), i),
                  _layer_block((1, D_MODEL), i)],
        out_specs=[pl.BlockSpec((1, tm, D_MODEL), lambda b, t: (b, t, 0)),
                   pl.BlockSpec((1, FFN_W - 1, 2 * D_FF), lambda b, t: (b, 0, 0))],
        out_shape=[jax.ShapeDtypeStruct((bsz, t_len, D_MODEL), F32),
                   jax.ShapeDtypeStruct((bsz, FFN_W - 1, 2 * D_FF), F32)],
        scratch_shapes=[pltpu.VMEM((D_FF // tf, tm + SUBLANES, tf), F32),
                        pltpu.VMEM((D_FF // tf, tm + SUBLANES, tf), F32)],
        compiler_params=_cparams(("arbitrary", "arbitrary")),
        name="ffn_seq",
    )(x, hist, w_up, w_conv, w_down, ln_g, ln_b)


def _gdn_gates(ba, a_row, dt_row):
    lane = lax.broadcasted_iota(jnp.int32, ba.shape, 1)
    return jnp.where(lane < GDN_H, _sigmoid(ba), a_row * _softplus(ba + dt_row))


def _qkv_post(c, q_ref, k_ref, v_ref, idx):
    c = _silu(c)
    for h in range(GDN_H):
        for ref, base in ((q_ref, 0), (k_ref, GDN_DIM)):
            a = c[:, base + h * GDN_DK:base + (h + 1) * GDN_DK]
            a = a * lax.rsqrt(jnp.sum(a * a, -1, keepdims=True) + RMS_EPS)
            ref[idx + (slice(h * GDN_DK, (h + 1) * GDN_DK),)] = a
    v_ref[idx + (slice(None),)] = c[:, 2 * GDN_DIM:]


def _proj_b_seq_kernel(x_ref, hist_ref, win_ref, wc_ref, arow_ref, dtrow_ref, k_ref, v_ref,
                       q_o, k_o, v_o, gb_o, zg_o, om_o, hist_o_ref, sbuf, kbd, vbd, *, tm):
    t = pl.program_id(1)
    lo = SUBLANES - (GDN_CONV_W - 1)
    nqkv = 3 * GDN_DIM

    @pl.when(t == 0)
    def _():
        sbuf[lo:SUBLANES, :] = hist_ref[0]
        kbd[...] = _head_blockdiag(k_ref[0])
        vbd[...] = _head_blockdiag(v_ref[0])

    ts = tm // MIX_SUBTILES
    subs = range(MIX_SUBTILES)
    rows = [slice(s * ts, (s + 1) * ts) for s in subs]
    hs = [_dot(x_ref[0, rows[s], :].astype(BF16), win_ref[...]) for s in subs]
    for s in subs:
        sbuf[SUBLANES + s * ts:SUBLANES + (s + 1) * ts, :] = hs[s][:, :nqkv]
    wc = wc_ref[...]
    qm0 = nqkv + GDN_DIM
    sc = []
    for s in subs:
        conv = wc[GDN_CONV_W - 1:GDN_CONV_W] * hs[s][:, :nqkv]
        for j in range(GDN_CONV_W - 1):
            conv = conv + wc[j:j + 1] * sbuf[lo + j + s * ts:lo + j + (s + 1) * ts, :]
        sc.append(_dot_nt(hs[s][:, qm0:qm0 + XDIM].astype(BF16), kbd[...]) * (XD ** -0.5))
        _qkv_post(conv, q_o, k_o, v_o, (0, rows[s]))
        zg_o[0, rows[s], :] = _silu(hs[s][:, nqkv:qm0])
        gb_o[0, rows[s], :] = _gdn_gates(hs[s][:, qm0 + XDIM:], arow_ref[...], dtrow_ref[...])
    last = sbuf[tm + lo:tm + SUBLANES, :]
    sbuf[lo:SUBLANES, :] = last
    hist_o_ref[0] = last
    ps = [jnp.concatenate([_softmax_rows(sc[s][:, h * N_MEM:(h + 1) * N_MEM]).astype(BF16)
                           for h in range(XH)], axis=-1) for s in subs]
    for s in subs:
        om_o[0, rows[s], :] = _dot(ps[s], vbd[...]).astype(BF16)


def _proj_b_seq(x, hist, w_in, w_conv, a_row, dt_row, mem_k, mem_v, *, j, i):
    bsz, t_len, _ = x.shape
    tm = min(MIX_TILE, t_len)
    tile = lambda n: pl.BlockSpec((1, tm, n), lambda b, t: (b, t, 0))
    f32o = lambda n: jax.ShapeDtypeStruct((bsz, t_len, n), F32)
    mem = pl.BlockSpec((None, 1, N_MEM, XDIM), lambda b, t: (i, b, 0, 0))
    return pl.pallas_call(
        functools.partial(_proj_b_seq_kernel, tm=tm),
        grid=(bsz, t_len // tm),
        in_specs=[tile(D_MODEL),
                  pl.BlockSpec((1, GDN_CONV_W - 1, 3 * GDN_DIM), lambda b, t: (b, 0, 0)),
                  _layer_block((D_MODEL, W_B_COLS), j),
                  _layer_block((GDN_CONV_W, 3 * GDN_DIM), j),
                  _layer_block((1, GB_LANES), j),
                  _layer_block((1, GB_LANES), j),
                  mem, mem],
        out_specs=[tile(GDN_DIM), tile(GDN_DIM), tile(GDN_DIM), tile(GB_LANES), tile(GDN_DIM),
                   tile(XDIM),
                   pl.BlockSpec((1, GDN_CONV_W - 1, 3 * GDN_DIM), lambda b, t: (b, 0, 0))],
        out_shape=[f32o(GDN_DIM), f32o(GDN_DIM), f32o(GDN_DIM), f32o(GB_LANES), f32o(GDN_DIM),
                   jax.ShapeDtypeStruct((bsz, t_len, XDIM), BF16),
                   jax.ShapeDtypeStruct((bsz, GDN_CONV_W - 1, 3 * GDN_DIM), F32)],
        scratch_shapes=[pltpu.VMEM((tm + SUBLANES, 3 * GDN_DIM), F32),
                        pltpu.VMEM((XH * N_MEM, XDIM), BF16),
                        pltpu.VMEM((XH * N_MEM, XDIM), BF16)],
        compiler_params=_cparams(("arbitrary", "arbitrary")),
        name="proj_b_seq",
    )(x, hist, w_in, w_conv, a_row, dt_row, mem_k, mem_v)


def _gdn_scan_kernel(q_ref, k_ref, v_ref, gb_ref, zg_ref, nw_ref, s0_ref, prev_ref, o_ref, s_o_ref,
                     s_scr, *, bb, nc, c, n_prev):
    t = pl.program_id(1)
    tc = nc * c

    @pl.when(t == 0)
    def _():
        s_scr[...] = s0_ref[0]
        for n in range(n_prev):
            s_o_ref[n] = prev_ref[n]

    ri = lax.broadcasted_iota(jnp.int32, (c, c), 0)
    ci = lax.broadcasted_iota(jnp.int32, (c, c), 1)
    tril = ri >= ci
    strict = ri > ci
    rt = lax.broadcasted_iota(jnp.int32, (tc, tc), 0)
    ct = lax.broadcasted_iota(jnp.int32, (tc, tc), 1)
    tril_chunks = ((rt >= ct) & ((rt // c) == (ct // c))).astype(BF16)
    head_sel = (lax.broadcasted_iota(jnp.int32, (SUBLANES, GB_LANES), 1)
                == lax.broadcasted_iota(jnp.int32, (SUBLANES, GB_LANES), 0) + GDN_H).astype(BF16)
    nw = nw_ref[...]

    def split3(x):
        a = x.astype(BF16)
        r = x - a.astype(F32)
        b = r.astype(BF16)
        return a, b, (r - b.astype(F32)).astype(BF16)

    gcs = []
    for i in range(bb):
        gc_all = sum(_dot(tril_chunks, p) for p in split3(gb_ref[i]))
        gc_t = sum(_dot_nt(head_sel, p) for p in split3(gc_all))
        gcs.append((gc_all, gc_t))

    probs = [(i, ic, h) for i in range(bb) for ic in range(nc) for h in range(GDN_H)]
    st = {}
    for p in probs:
        i, ic, h = p
        rows = slice(ic * c, (ic + 1) * c)
        hs = slice(h * GDN_DK, (h + 1) * GDN_DK)
        gc_all, gc_t = gcs[i]
        q = q_ref[i, rows, hs] * (GDN_DK ** -0.5)
        k = k_ref[i, rows, hs]
        gcol = gc_all[rows, GDN_H + h:GDN_H + h + 1]
        grow = gc_t[h:h + 1, rows]
        glast = gc_all[(ic + 1) * c - 1:(ic + 1) * c, GDN_H + h:GDN_H + h + 1]
        egc = jnp.exp(gcol)
        kb = k * gb_ref[i, rows, h:h + 1]
        st[p] = dict(
            decay=jnp.exp(jnp.where(tril, gcol - grow, -jnp.inf)),
            aq=_dot_nt(jnp.concatenate([kb, q], axis=0).astype(BF16), k.astype(BF16)),
            sol=jnp.concatenate([v_ref[i, rows, hs] * gb_ref[i, rows, h:h + 1], kb * egc], axis=-1),
            qd=(q * egc).astype(BF16),
            k_dec=(k * jnp.exp(glast - gcol)).astype(BF16),
            e_last=jnp.exp(glast))
    sb = min(SOLVE_BLOCK, c)
    nblk = c // sb
    same_blk = (ri // sb) == (ci // sb)
    lane_sb = lax.broadcasted_iota(jnp.int32, (sb, c), 1)
    eye_ss = ((lane_sb % sb) == lax.broadcasted_iota(jnp.int32, (sb, c), 0)).astype(F32)
    blk_rows = [slice(n * sb, (n + 1) * sb) for n in range(nblk)]
    tile_rows = lambda a: jnp.concatenate([a] * nblk, axis=0) if nblk > 1 else a

    def split2(x):
        hi = x.astype(BF16)
        return jnp.concatenate([hi, (x - hi.astype(F32)).astype(BF16)], axis=-1)

    def fold2(y):
        n = y.shape[-1] // 2
        return y[:, :n] + y[:, n:]

    for p in probs:
        d = st[p]
        d["lmat"] = jnp.where(strict, d["aq"][:c] * d["decay"], 0.0)
        d["qk"] = (d["aq"][c:] * d["decay"]).astype(BF16)
        d["pss"] = sum(jnp.where(lane_sb // sb == n, d["lmat"][blk_rows[n]], 0.0) for n in range(nblk))
        d["pbd"] = jnp.where(same_blk, d["lmat"], 0.0).astype(BF16)
        d["tss"] = eye_ss - d["pss"]
    for _ in range(max(sb.bit_length() - 2, 0)):
        for p in probs:
            st[p]["pss"] = _dot(st[p]["pss"].astype(BF16), st[p]["pbd"])
        for p in probs:
            d = st[p]
            d["pbd"] = jnp.where(same_blk, tile_rows(d["pss"]), 0.0).astype(BF16)
            d["tss"] = d["tss"] + _dot(d["tss"].astype(BF16), d["pbd"])
    zero_blk = jnp.zeros((sb, 4 * GDN_DV), BF16)
    for p in probs:
        d = st[p]
        d["tbd"] = jnp.where(same_blk, tile_rows(d["tss"]), 0.0).astype(BF16)
        d["x2"] = []
        d["x"] = []
    for n in range(nblk):
        for p in probs:
            d = st[p]
            z = d["sol"][blk_rows[n]]
            if n > 0:
                lrow = jnp.where(lane_sb < n * sb, d["lmat"][blk_rows[n]], 0.0).astype(BF16)
                z = z - fold2(_dot(lrow, jnp.concatenate(d["x2"] + [zero_blk] * (nblk - n), axis=0)))
            d["y"] = jnp.concatenate([zero_blk] * n + [split2(z)] + [zero_blk] * (nblk - n - 1), axis=0)
        for p in probs:
            d = st[p]
            x = fold2(_dot(d["tbd"][blk_rows[n]], d["y"]))
            d["x"].append(x)
            if n + 1 < nblk:
                d["x2"].append(split2(x))
    for p in probs:
        d = st[p]
        sol = jnp.concatenate(d["x"], axis=0) if nblk > 1 else d["x"][0]
        d["wq"] = jnp.concatenate([sol[:, GDN_DV:].astype(BF16), d["qd"]], axis=0)
        d["u"] = sol[:, :GDN_DV]

    for ic in range(nc):
        rows = slice(ic * c, (ic + 1) * c)
        seqs = [(i, h) for i in range(bb) for h in range(GDN_H)]
        s_old = {ih: s_scr[ih[0], ih[1]] for ih in seqs}
        ws = {ih: _dot(st[ih[0], ic, ih[1]]["wq"], s_old[ih].astype(BF16)) for ih in seqs}
        vb = {ih: (st[ih[0], ic, ih[1]]["u"] - ws[ih][:c]).astype(BF16) for ih in seqs}
        for ih in seqs:
            d = st[ih[0], ic, ih[1]]
            s_scr[ih[0], ih[1]] = s_old[ih] * d["e_last"] + _dot_tn(d["k_dec"], vb[ih])
        for ih in seqs:
            i, h = ih
            hs = slice(h * GDN_DK, (h + 1) * GDN_DK)
            o = ws[ih][c:] + _dot(st[i, ic, h]["qk"], vb[ih])
            o = o * lax.rsqrt(jnp.mean(o * o, -1, keepdims=True) + RMS_EPS)
            o_ref[i, rows, hs] = (o * nw * zg_ref[i, rows, hs]).astype(o_ref.dtype)
    s_o_ref[n_prev] = s_scr[...]


def _gdn_scan(q, k, v, gb, zg, norm_w, s0_all, prev, *, j, layer, c, bb, nc):
    bsz, t_len, _ = q.shape
    tc = nc * c
    n_prev = 0 if prev is None else prev.shape[0]
    tile = lambda n: pl.BlockSpec((bb, tc, n), lambda b, t: (b, t, 0))
    st = lambda n: pl.BlockSpec((n, bb, GDN_H, GDN_DK, GDN_DV), lambda b, t: (0, b, 0, 0, 0))
    st_in = pl.BlockSpec((1, bb, GDN_H, GDN_DK, GDN_DV), lambda b, t: (layer, b, 0, 0, 0))
    if prev is None:
        prev, prev_spec = s0_all, pl.BlockSpec(memory_space=pl.ANY)
    else:
        prev_spec = st(n_prev)
    return pl.pallas_call(
        functools.partial(_gdn_scan_kernel, bb=bb, nc=nc, c=c, n_prev=n_prev),
        grid=(bsz // bb, t_len // tc),
        in_specs=[tile(GDN_DIM), tile(GDN_DIM), tile(GDN_DIM), tile(GB_LANES), tile(GDN_DIM),
                  _layer_block((1, GDN_DV), j), st_in, prev_spec],
        out_specs=[tile(GDN_DIM), st(n_prev + 1)],
        out_shape=[jax.ShapeDtypeStruct((bsz, t_len, GDN_DIM), BF16),
                   jax.ShapeDtypeStruct((n_prev + 1, bsz, GDN_H, GDN_DK, GDN_DV), F32)],
        scratch_shapes=[pltpu.VMEM((bb, GDN_H, GDN_DK, GDN_DV), F32)],
        compiler_params=_cparams(("arbitrary", "arbitrary")),
        name="gdn_scan",
    )(q, k, v, gb, zg, norm_w, s0_all, prev)


def _out_ln_kernel(a1_ref, a2_ref, w_ref, x_ref, g_ref, b_ref, o_ref):
    n1 = a1_ref.shape[-1]
    mix = (_dot(a1_ref[...].astype(BF16), w_ref[:n1, :])
           + _dot(a2_ref[...].astype(BF16), w_ref[n1:, :]))
    o_ref[...] = _layer_norm(ALPHA * x_ref[...] + mix, g_ref[...], b_ref[...])


def _out_ln(a1, a2, w, x, ln_g, ln_b, *, j, i):
    rows = x.shape[0]
    tr = min(ROW_TILE, rows)
    n1, n2 = a1.shape[1], a2.shape[1]
    tile = lambda n: pl.BlockSpec((tr, n), lambda r: (r, 0))
    return pl.pallas_call(
        _out_ln_kernel,
        grid=(rows // tr,),
        in_specs=[tile(n1), tile(n2), _layer_block((n1 + n2, D_MODEL), j), tile(D_MODEL),
                  _layer_block((1, D_MODEL), i), _layer_block((1, D_MODEL), i)],
        out_specs=tile(D_MODEL),
        out_shape=jax.ShapeDtypeStruct((rows, D_MODEL), F32),
        compiler_params=_cparams(("arbitrary",)),
        name="out_ln",
    )(a1, a2, w, x, ln_g, ln_b)


def _mix_a_front_sample_kernel(x_ref, hist_ref, win_ref, wc_ref, y_o, qm_o, hist_o, *, t_len, bsz):
    h = _dot(x_ref[...].astype(BF16), win_ref[...])
    u = h[:, 2 * SC_DIM:3 * SC_DIM] * h[:, :SC_DIM]
    slabs = [hist_ref[:, j, :] for j in range(SC_W - 1)]
    slabs += [u[t * bsz:(t + 1) * bsz] for t in range(t_len)]
    wc = wc_ref[...]
    for t in range(t_len):
        conv = wc[0:1] * slabs[t]
        for j in range(1, SC_W):
            conv = conv + wc[j:j + 1] * slabs[t + j]
        y_o[t * bsz:(t + 1) * bsz, :] = (h[t * bsz:(t + 1) * bsz, SC_DIM:2 * SC_DIM] * conv).astype(BF16)
    for j in range(SC_W - 1):
        hist_o[:, j, :] = slabs[t_len + j]
    qm_o[...] = h[:, 3 * SC_DIM:]


def _whole(shape):
    nd = len(shape)
    return pl.BlockSpec(tuple(shape), lambda *_: (0,) * nd)


def _mix_a_front_sample(x2d, hist, w_in, w_conv, *, j, t_len, bsz):
    rows = x2d.shape[0]
    out_shape = [jax.ShapeDtypeStruct((rows, SC_DIM), BF16),
                 jax.ShapeDtypeStruct((rows, XDIM), F32),
                 jax.ShapeDtypeStruct((bsz, SC_W - 1, SC_DIM), F32)]
    return pl.pallas_call(
        functools.partial(_mix_a_front_sample_kernel, t_len=t_len, bsz=bsz),
        grid=(1,),
        in_specs=[_whole(x2d.shape), _layer_block(hist.shape[1:], j), _layer_block(w_in.shape[1:], j),
                  _layer_block(w_conv.shape[1:], j)],
        out_specs=[_whole(s.shape) for s in out_shape],
        out_shape=out_shape,
        compiler_params=_cparams(("arbitrary",)),
        name="mix_a_front_sample",
    )(x2d, hist, w_in, w_conv)


def _proj_b_front_sample_kernel(x_ref, hist_ref, win_ref, wc_ref, arow_ref, dtrow_ref,
                                q_o, k_o, v_o, gb_o, zg_o, qm_o, hist_o, *, t_len, t_pad, bsz):
    nqkv = 3 * GDN_DIM
    h = _dot(x_ref[...].astype(BF16), win_ref[...])
    slabs = [hist_ref[j] for j in range(GDN_CONV_W - 1)]
    slabs += [h[t * bsz:(t + 1) * bsz, :nqkv] for t in range(t_len)]
    wc = wc_ref[...]
    for t in range(t_len):
        conv = wc[0:1] * slabs[t]
        for j in range(1, GDN_CONV_W):
            conv = conv + wc[j:j + 1] * slabs[t + j]
        _qkv_post(conv, q_o, k_o, v_o, (slice(None), t))
    for j in range(GDN_CONV_W - 1):
        hist_o[j] = slabs[t_len + j]
    zg = _silu(h[:, nqkv:nqkv + GDN_DIM])
    gb = _gdn_gates(h[:, nqkv + GDN_DIM + XDIM:], arow_ref[...], dtrow_ref[...])
    for t in range(t_len):
        zg_o[:, t, :] = zg[t * bsz:(t + 1) * bsz]
        gb_o[:, t, :] = gb[t * bsz:(t + 1) * bsz]
    for ref in (q_o, k_o, v_o, gb_o, zg_o):
        for t in range(t_len, t_pad):
            ref[:, t, :] = jnp.zeros((bsz, ref.shape[-1]), F32)
    qm_o[...] = h[:, nqkv + GDN_DIM:nqkv + GDN_DIM + XDIM]


def _proj_b_front_sample(x2d, hist, w_in, w_conv, a_row, dt_row, *, j, t_len, t_pad, bsz):
    rows = x2d.shape[0]
    bmaj = lambda n: jax.ShapeDtypeStruct((bsz, t_pad, n), F32)
    out_shape = [bmaj(GDN_DIM), bmaj(GDN_DIM), bmaj(GDN_DIM), bmaj(GB_LANES), bmaj(GDN_DIM),
                 jax.ShapeDtypeStruct((rows, XDIM), F32),
                 jax.ShapeDtypeStruct((GDN_CONV_W - 1, bsz, 3 * GDN_DIM), F32)]
    return pl.pallas_call(
        functools.partial(_proj_b_front_sample_kernel, t_len=t_len, t_pad=t_pad, bsz=bsz),
        grid=(1,),
        in_specs=[_whole(x2d.shape), _layer_block(hist.shape[1:], j), _layer_block(w_in.shape[1:], j),
                  _layer_block(w_conv.shape[1:], j), _layer_block((1, GB_LANES), j),
                  _layer_block((1, GB_LANES), j)],
        out_specs=[_whole(s.shape) for s in out_shape],
        out_shape=out_shape,
        compiler_params=_cparams(("arbitrary",)),
        name="proj_b_front_sample",
    )(x2d, hist, w_in, w_conv, a_row, dt_row)


def _ffn_sample_kernel(x_ref, hist_ref, w_ref, c_ref, wdn_ref, g_ref, b_ref, o_ref, hist_o,
                       acc, gate, *, t_len, bsz, nj):
    s = pl.program_id(0)
    h = _dot(x_ref[...].astype(BF16), w_ref[...])
    slabs = [hist_ref[:, r, :] for r in range(FFN_W - 1)]
    slabs += [h[t * bsz:(t + 1) * bsz] for t in range(t_len)]
    wc = c_ref[...]
    outs = []
    for t in range(t_len):
        c = wc[0:1] * slabs[t]
        for w in range(1, FFN_W):
            c = c + wc[w:w + 1] * slabs[t + w]
        outs.append(c)
    for r in range(FFN_W - 1):
        hist_o[:, r, :] = slabs[t_len + r]
    conv = jnp.concatenate(outs, axis=0)

    @pl.when(s < nj)
    def _():
        gate[s] = _silu(conv)

    @pl.when(s >= nj)
    def _():
        d = _dot((gate[s - nj] * conv).astype(BF16), wdn_ref[...])

        @pl.when(s == nj)
        def _():
            acc[...] = d

        @pl.when(s > nj)
        def _():
            acc[...] += d

    @pl.when(s == 2 * nj - 1)
    def _():
        o_ref[...] = _layer_norm(ALPHA * x_ref[...] + acc[...], g_ref[...], b_ref[...])


def _ffn_sample(x2d, hist, w_up, w_conv, w_down, ln_g, ln_b, *, i, t_len, bsz):
    rows = x2d.shape[0]
    tf = FF_TILE_SAMPLE
    nj = D_FF // tf
    return pl.pallas_call(
        functools.partial(_ffn_sample_kernel, t_len=t_len, bsz=bsz, nj=nj),
        grid=(2 * nj,),
        in_specs=[_resident((rows, D_MODEL)),
                  pl.BlockSpec((None, bsz, FFN_W - 1, tf), lambda s: (i, 0, 0, s)),
                  pl.BlockSpec((None, D_MODEL, tf), lambda s: (i, 0, s)),
                  pl.BlockSpec((None, FFN_W, tf), lambda s: (i, 0, s)),
                  pl.BlockSpec((None, tf, D_MODEL), lambda s: (i, jnp.maximum(s - nj, 0), 0)),
                  _layer_block((1, D_MODEL), i), _layer_block((1, D_MODEL), i)],
        out_specs=[pl.BlockSpec((rows, D_MODEL), lambda s: (0, 0)),
                   pl.BlockSpec((bsz, FFN_W - 1, tf), lambda s: (0, 0, s))],
        out_shape=[jax.ShapeDtypeStruct((rows, D_MODEL), F32),
                   jax.ShapeDtypeStruct((bsz, FFN_W - 1, 2 * D_FF), F32)],
        scratch_shapes=[pltpu.VMEM((rows, D_MODEL), F32), pltpu.VMEM((nj, rows, tf), F32)],
        compiler_params=_cparams(("arbitrary",)),
        name="ffn_sample",
    )(x2d, hist, w_up, w_conv, w_down, ln_g, ln_b)


def _prep_weights(w_in_a, w_out_a, w_in_b, a_log, dt_bias, w_out_b, w_mem_kv, w_up, w_down):
    nqkvz = 4 * GDN_DIM
    w_b = jnp.concatenate(
        [w_in_b[..., :nqkvz], w_in_b[..., nqkvz + 2 * GDN_H:],
         jnp.pad(w_in_b[..., nqkvz:nqkvz + 2 * GDN_H], ((0, 0), (0, 0), (0, GB_LANES - 2 * GDN_H)))],
        axis=-1).astype(BF16)
    pad = GB_LANES - 2 * GDN_H
    a_row = jnp.pad(-jnp.exp(a_log.astype(F32)), ((0, 0), (GDN_H, pad)))[:, None, :]
    dt_row = jnp.pad(dt_bias.astype(F32), ((0, 0), (GDN_H, pad)))[:, None, :]
    return dict(w_in_a=w_in_a.astype(BF16), w_out_a=w_out_a.astype(BF16), w_in_b=w_b,
                w_out_b=w_out_b.astype(BF16), w_mem_kv=w_mem_kv.astype(BF16),
                w_up=w_up.astype(BF16), w_down=w_down.astype(BF16), a_row=a_row, dt_row=dt_row)


def _trunk_prompt(x, mem_k, mem_v, wts, conv_a, conv_b, gdn_norm_w, ln1_g, ln1_b, ln2_g, ln2_b,
                  w_conv_ffn):
    bsz, t_len, _ = x.shape
    z = lambda *s: jnp.zeros(s, F32)
    new_sc, new_gc, new_gs, new_ffn = [], [], None, []
    for i in range(DEPTH):
        j = i // 2
        if i % 2 == 0:
            x, hs = _mix_a_seq(x, z(bsz, SC_W - 1, SC_DIM), wts["w_in_a"], conv_a, wts["w_out_a"],
                               mem_k, mem_v, ln1_g, ln1_b, j=j, i=i)
            new_sc.append(hs)
        else:
            q, k, v, gb, zg, om, hg = _proj_b_seq(
                x, z(bsz, GDN_CONV_W - 1, 3 * GDN_DIM), wts["w_in_b"], conv_b,
                wts["a_row"], wts["dt_row"], mem_k, mem_v, j=j, i=i)
            o, new_gs = _gdn_scan(q, k, v, gb, zg, gdn_norm_w, z(1, bsz, GDN_H, GDN_DK, GDN_DV),
                                  new_gs, j=j, layer=0, c=GDN_CHUNK, bb=SCAN_SEQS, nc=SCAN_CHUNKS)
            rows = bsz * t_len
            x = _out_ln(o.reshape(rows, GDN_DIM), om.reshape(rows, XDIM), wts["w_out_b"],
                        x.reshape(rows, D_MODEL), ln1_g, ln1_b, j=j, i=i).reshape(bsz, t_len, D_MODEL)
            new_gc.append(hg)
        x, hf = _ffn_seq(x, z(bsz, FFN_W - 1, 2 * D_FF), wts["w_up"], w_conv_ffn, wts["w_down"],
                         ln2_g, ln2_b, i=i)
        new_ffn.append(hf)
    return x, jnp.stack(new_sc), jnp.stack(new_gc), new_gs, jnp.stack(new_ffn)


def _trunk_sample(x, mem_k, mem_v, sc_hist, gdn_hist, gdn_s, ffn_hist, wts, conv_a, conv_b,
                  gdn_norm_w, ln1_g, ln1_b, ln2_g, ln2_b, w_conv_ffn):
    bsz, t_len, _ = x.shape
    rows = bsz * t_len
    t_pad = SUBLANES
    x2 = jnp.transpose(x, (1, 0, 2)).reshape(rows, D_MODEL)
    new_sc, new_gc, new_gs, new_ffn = [], [], None, []
    mem_kt = jnp.transpose(mem_k, (0, 1, 3, 4, 2))
    mem_vt = jnp.transpose(mem_v, (0, 1, 3, 4, 2))
    gdn_hist = jnp.transpose(gdn_hist, (0, 2, 1, 3))
    for i in range(DEPTH):
        j = i // 2
        if i % 2 == 0:
            y, qm, hs = _mix_a_front_sample(x2, sc_hist, wts["w_in_a"], conv_a, j=j,
                                            t_len=t_len, bsz=bsz)
            om = _attn_sample(qm.reshape(t_len, bsz, XDIM), mem_kt, mem_vt, i).reshape(rows, XDIM)
            x2 = _out_ln(y, om, wts["w_out_a"], x2, ln1_g, ln1_b, j=j, i=i)
            new_sc.append(hs)
        else:
            q, k, v, gb, zg, qm, hg = _proj_b_front_sample(
                x2, gdn_hist, wts["w_in_b"], conv_b, wts["a_row"], wts["dt_row"], j=j,
                t_len=t_len, t_pad=t_pad, bsz=bsz)
            om = _attn_sample(qm.reshape(t_len, bsz, XDIM), mem_kt, mem_vt, i).reshape(rows, XDIM)
            o, new_gs = _gdn_scan(q, k, v, gb, zg, gdn_norm_w, gdn_s,
                                  new_gs, j=j, layer=j, c=t_pad, bb=SCAN_SEQS_SAMPLE, nc=1)
            o = jnp.transpose(o[:, :t_len], (1, 0, 2)).reshape(rows, GDN_DIM)
            x2 = _out_ln(o, om, wts["w_out_b"], x2, ln1_g, ln1_b, j=j, i=i)
            new_gc.append(hg)
        x2, hf = _ffn_sample(x2, ffn_hist, wts["w_up"], w_conv_ffn, wts["w_down"],
                             ln2_g, ln2_b, i=i, t_len=t_len, bsz=bsz)
        new_ffn.append(hf)
    y = jnp.transpose(x2.reshape(t_len, bsz, D_MODEL), (1, 0, 2))
    gc = jnp.transpose(jnp.stack(new_gc), (0, 2, 1, 3))
    return y, jnp.stack(new_sc), gc, new_gs, jnp.stack(new_ffn)


def kernel(x_prompt, x_sample, mem_prompt, cache_mem_k, cache_mem_v, state_shortconv, state_gdn_conv,
           state_gdn, state_ffn_conv, w_in_a, conv_a, w_out_a, w_in_b, conv_b, a_log, dt_bias,
           gdn_norm_w, w_out_b, w_mem_kv, ln1_g, ln1_b, ln2_g, ln2_b, w_up, w_conv_ffn, w_down):
    wts = _prep_weights(w_in_a, w_out_a, w_in_b, a_log, dt_bias, w_out_b, w_mem_kv, w_up, w_down)
    row3 = lambda a: a.reshape(a.shape[0], 1, a.shape[1])
    shared = (wts, conv_a, conv_b, row3(gdn_norm_w), row3(ln1_g), row3(ln1_b), row3(ln2_g),
              row3(ln2_b), w_conv_ffn)
    bsz = x_prompt.shape[0]
    k2, v2 = _mem_kv(mem_prompt.reshape(bsz * N_MEM, D_MODEL), wts["w_mem_kv"])
    mem_k_prompt = k2.reshape(DEPTH, bsz, N_MEM, XDIM)
    mem_v_prompt = v2.reshape(DEPTH, bsz, N_MEM, XDIM)
    y_prompt, sc_p, gc_p, gs_p, ffn_p = _trunk_prompt(x_prompt, mem_k_prompt, mem_v_prompt, *shared)
    y_sample, sc_s, gc_s, gs_s, ffn_s = _trunk_sample(
        x_sample, cache_mem_k, cache_mem_v, state_shortconv, state_gdn_conv, state_gdn,
        state_ffn_conv, *shared)
    shape5 = (DEPTH, bsz, N_MEM, XH, XD)
    return (y_prompt, y_sample, mem_k_prompt.reshape(shape5), mem_v_prompt.reshape(shape5),
            sc_p, gc_p, gs_p, ffn_p, sc_s, gc_s, gs_s, ffn_s)
```

```python
import functools

import jax
import jax.numpy as jnp
from jax import lax
from jax.experimental import pallas as pl
from jax.experimental.pallas import tpu as pltpu

F32 = jnp.float32
BF16 = jnp.bfloat16

DEPTH = 4
D_MODEL = 1024
SC_DIM = 768
SC_W = 3
GDN_H = 6
GDN_DK = 128
GDN_DV = 128
GDN_DIM = GDN_H * GDN_DK
GDN_CONV_W = 4
GDN_CHUNK = 64
N_MEM = 256
XH = 4
XD = 64
XDIM = XH * XD
D_FF = 2816
FFN_W = 3
ALPHA = (2.0 * DEPTH) ** 0.25
LN_EPS = 1e-5
RMS_EPS = 1e-6

V7X_VMEM_BYTES = 64 * 1024 * 1024
VMEM_LIMIT = V7X_VMEM_BYTES - 8 * 1024 * 1024
SUBLANES = 8
LANES = 128

GB_LANES = LANES

SEQ_TILE = 512
MIX_TILE = 512
MIX_A_TILE = 1024
MIX_SUBTILES = 2
SCAN_SEQS = 4
SCAN_CHUNKS = 2
SCAN_SEQS_SAMPLE = 8
SOLVE_BLOCK = 16
ROW_TILE = 1024
FF_TILE = 2816
FF_TILE_SAMPLE = 1408
ATTN_BATCH_BLOCK = 32
ATTN_UNROLL = 8
W_B_COLS = 3 * GDN_DIM + GDN_DIM + XDIM + GB_LANES


def _cparams(sem):
    return pltpu.CompilerParams(dimension_semantics=sem, vmem_limit_bytes=VMEM_LIMIT)


def _resident(shape):
    nd = len(shape)
    return pl.BlockSpec(shape, lambda *_: (0,) * nd, pipeline_mode=pl.Buffered(1))


def _layer_block(shape, layer):
    nd = len(shape)
    return pl.BlockSpec((None,) + tuple(shape), lambda *_: (layer,) + (0,) * nd,
                        pipeline_mode=pl.Buffered(1))


def _silu(x):
    return x * (1.0 / (1.0 + jnp.exp(-x)))


def _sigmoid(x):
    return 1.0 / (1.0 + jnp.exp(-x))


def _softplus(x):
    return jnp.maximum(x, 0.0) + jnp.log(1.0 + jnp.exp(-jnp.abs(x)))


def _layer_norm(v, g, b):
    mu = jnp.mean(v, -1, keepdims=True)
    d = v - mu
    var = jnp.mean(d * d, -1, keepdims=True)
    return d * lax.rsqrt(var + LN_EPS) * g + b


def _dot(a, b):
    return jnp.dot(a, b, preferred_element_type=F32)


def _dot_nt(a, b):
    return lax.dot_general(a, b, (((1,), (1,)), ((), ())), preferred_element_type=F32)


def _dot_tn(a, b):
    return lax.dot_general(a, b, (((0,), (0,)), ((), ())), preferred_element_type=F32)


def _kv_kernel(m_ref, w_ref, k_ref, v_ref, *, bsz):
    kv = _dot(m_ref[...].astype(BF16), w_ref[0])
    for b in range(bsz):
        kv_t = kv[b * N_MEM:(b + 1) * N_MEM, :].T
        k_ref[0, b] = kv_t[:XDIM]
        v_ref[0, b] = kv_t[XDIM:]


def _mem_kv(mem2d, w_kv, *, bsz):
    rows = mem2d.shape[0]
    out = jax.ShapeDtypeStruct((DEPTH, bsz, XDIM, N_MEM), F32)
    return pl.pallas_call(
        functools.partial(_kv_kernel, bsz=bsz),
        grid=(DEPTH,),
        in_specs=[_resident((rows, D_MODEL)),
                  pl.BlockSpec((1, D_MODEL, 2 * XDIM), lambda l: (l, 0, 0))],
        out_specs=[pl.BlockSpec((1, bsz, XDIM, N_MEM), lambda l: (l, 0, 0, 0))] * 2,
        out_shape=[out, out],
        compiler_params=_cparams(("arbitrary",)),
        name="mem_kv",
    )(mem2d, w_kv)


def _head_blockdiag(kv_t):
    row_head = lax.broadcasted_iota(jnp.int32, kv_t.shape, 0) // XD
    return jnp.concatenate(
        [jnp.where(row_head == h, kv_t, 0.0).astype(BF16) for h in range(XH)], axis=1)


def _softmax_rows(s):
    m = jnp.max(s, -1, keepdims=True)
    e = jnp.exp(s - m)
    return e / jnp.sum(e, -1, keepdims=True)


def _attn_sample_kernel(q_ref, k_ref, v_ref, o_ref, *, bb, t_len):
    def body(g, carry):
        elems = [g * ATTN_UNROLL + e for e in range(ATTN_UNROLL)]
        q8 = []
        for i in elems:
            rows = [q_ref[t, pl.ds(i, 1), :] for t in range(t_len)]
            rows.append(jnp.zeros((SUBLANES - t_len, XDIM), F32))
            q8.append(jnp.concatenate(rows, axis=0).astype(BF16))
        s = [[_dot(q8[e][:, h * XD:(h + 1) * XD], k_ref[0, i, h].astype(BF16)) * (XD ** -0.5)
              for h in range(XH)] for e, i in enumerate(elems)]
        p = [[_softmax_rows(sh).astype(BF16) for sh in se] for se in s]
        for e, i in enumerate(elems):
            o8 = jnp.concatenate([_dot_nt(p[e][h], v_ref[0, i, h].astype(BF16)) for h in range(XH)],
                                 axis=-1)
            for t in range(t_len):
                o_ref[t, pl.ds(i, 1), :] = o8[t:t + 1]
        return carry

    lax.fori_loop(0, bb // ATTN_UNROLL, body, 0)


def _attn_sample(qm_tm, mem_kt, mem_vt, layer):
    t_len, bsz, _ = qm_tm.shape
    bb = min(ATTN_BATCH_BLOCK, bsz)
    kv = pl.BlockSpec((1, bb, XH, XD, N_MEM), lambda i: (layer, i, 0, 0, 0))
    return pl.pallas_call(
        functools.partial(_attn_sample_kernel, bb=bb, t_len=t_len),
        grid=(bsz // bb,),
        in_specs=[pl.BlockSpec((t_len, bb, XDIM), lambda i: (0, i, 0)), kv, kv],
        out_specs=pl.BlockSpec((t_len, bb, XDIM), lambda i: (0, i, 0)),
        out_shape=jax.ShapeDtypeStruct((t_len, bsz, XDIM), F32),
        compiler_params=_cparams(("arbitrary",)),
        name="attn_sample",
    )(qm_tm, mem_kt, mem_vt)


def _mix_a_seq_kernel(x_ref, hist_ref, win_ref, wc_ref, wout_ref, k_ref, v_ref, g_ref, b_ref,
                      o_ref, hist_o_ref, ubuf, kbd, vbd, *, tm):
    t = pl.program_id(1)
    lo = SUBLANES - (SC_W - 1)

    @pl.when(t == 0)
    def _():
        ubuf[lo:SUBLANES, :] = hist_ref[0]
        kbd[...] = _head_blockdiag(k_ref[0])
        vbd[...] = _head_blockdiag(v_ref[0])

    ts = tm // MIX_SUBTILES
    subs = range(MIX_SUBTILES)
    rows = [slice(s * ts, (s + 1) * ts) for s in subs]
    hs = [_dot(x_ref[0, rows[s], :].astype(BF16), win_ref[...]) for s in subs]
    us = [hs[s][:, 2 * SC_DIM:3 * SC_DIM] * hs[s][:, :SC_DIM] for s in subs]
    for s in subs:
        ubuf[SUBLANES + s * ts:SUBLANES + (s + 1) * ts, :] = us[s]
    wc = wc_ref[...]
    ys = []
    for s in subs:
        conv = wc[SC_W - 1:SC_W] * us[s]
        for j in range(SC_W - 1):
            conv = conv + wc[j:j + 1] * ubuf[lo + j + s * ts:lo + j + (s + 1) * ts, :]
        ys.append((hs[s][:, SC_DIM:2 * SC_DIM] * conv).astype(BF16))
    last = ubuf[tm + lo:tm + SUBLANES, :]
    ubuf[lo:SUBLANES, :] = last
    hist_o_ref[0] = last

    sc = [_dot(hs[s][:, 3 * SC_DIM:].astype(BF16), kbd[...]) * (XD ** -0.5) for s in subs]
    ps = [jnp.concatenate([_softmax_rows(sc[s][:, h * N_MEM:(h + 1) * N_MEM]).astype(BF16)
                           for h in range(XH)], axis=-1) for s in subs]
    oms = [_dot_nt(ps[s], vbd[...]).astype(BF16) for s in subs]
    mix = [_dot(ys[s], wout_ref[:SC_DIM, :]) + _dot(oms[s], wout_ref[SC_DIM:, :]) for s in subs]
    for s in subs:
        o_ref[0, rows[s], :] = _layer_norm(ALPHA * x_ref[0, rows[s], :] + mix[s], g_ref[...], b_ref[...])


def _mix_a_seq(x, hist, w_in, w_conv, w_out, mem_k, mem_v, ln_g, ln_b, *, j, i):
    bsz, t_len, _ = x.shape
    tm = min(MIX_A_TILE, t_len)
    mem = pl.BlockSpec((None, 1, XDIM, N_MEM), lambda b, t: (i, b, 0, 0))
    return pl.pallas_call(
        functools.partial(_mix_a_seq_kernel, tm=tm),
        grid=(bsz, t_len // tm),
        in_specs=[pl.BlockSpec((1, tm, D_MODEL), lambda b, t: (b, t, 0)),
                  pl.BlockSpec((1, SC_W - 1, SC_DIM), lambda b, t: (b, 0, 0)),
                  _layer_block(w_in.shape[1:], j),
                  _layer_block((SC_W, SC_DIM), j),
                  _layer_block((SC_DIM + XDIM, D_MODEL), j),
                  mem, mem,
                  _layer_block((1, D_MODEL), i),
                  _layer_block((1, D_MODEL), i)],
        out_specs=[pl.BlockSpec((1, tm, D_MODEL), lambda b, t: (b, t, 0)),
                   pl.BlockSpec((1, SC_W - 1, SC_DIM), lambda b, t: (b, 0, 0))],
        out_shape=[jax.ShapeDtypeStruct((bsz, t_len, D_MODEL), F32),
                   jax.ShapeDtypeStruct((bsz, SC_W - 1, SC_DIM), F32)],
        scratch_shapes=[pltpu.VMEM((tm + SUBLANES, SC_DIM), F32),
                        pltpu.VMEM((XDIM, XH * N_MEM), BF16),
                        pltpu.VMEM((XDIM, XH * N_MEM), BF16)],
        compiler_params=_cparams(("arbitrary", "arbitrary")),
        name="mix_a_seq",
    )(x, hist, w_in, w_conv, w_out, mem_k, mem_v, ln_g, ln_b)


def _ffn_seq_kernel(x_ref, hist_ref, wup_ref, wc_ref, wdn_ref, g_ref, b_ref,
                    o_ref, hist_o_ref, gbuf, ubuf, *, tm, tf):
    t = pl.program_id(1)
    lo = SUBLANES - (FFN_W - 1)
    nj = D_FF // tf
    halves = ((gbuf, 0), (ubuf, D_FF))

    @pl.when(t == 0)
    def _():
        for buf, off in halves:
            for j in range(nj):
                buf[j, lo:SUBLANES, :] = hist_ref[0, :, off + j * tf:off + (j + 1) * tf]

    x = x_ref[0]
    xb = x.astype(BF16)

    def up_project(j):
        return [_dot(xb, wup_ref[:, off + j * tf:off + (j + 1) * tf]) for _, off in halves]

    def conv_act(j, hs):
        conv = []
        for (buf, off), h in zip(halves, hs):
            c0 = off + j * tf
            buf[j, SUBLANES:SUBLANES + tm, :] = h
            c = wc_ref[FFN_W - 1:FFN_W, c0:c0 + tf] * h
            for w in range(FFN_W - 1):
                c = c + wc_ref[w:w + 1, c0:c0 + tf] * buf[j, lo + w:lo + w + tm, :]
            last = buf[j, tm + lo:tm + SUBLANES, :]
            buf[j, lo:SUBLANES, :] = last
            hist_o_ref[0, :, c0:c0 + tf] = last
            conv.append(c)
        return (_silu(conv[0]) * conv[1]).astype(BF16)

    acc = None
    hs = up_project(0)
    for j in range(nj):
        hs_next = up_project(j + 1) if j + 1 < nj else None
        d = _dot(conv_act(j, hs), wdn_ref[j * tf:(j + 1) * tf, :])
        acc = d if acc is None else acc + d
        hs = hs_next
    o_ref[0] = _layer_norm(ALPHA * x + acc, g_ref[...], b_ref[...])


def _ffn_seq(x, hist, w_up, w_conv, w_down, ln_g, ln_b, *, i):
    bsz, t_len, _ = x.shape
    tm = min(SEQ_TILE, t_len)
    tf = FF_TILE
    return pl.pallas_call(
        functools.partial(_ffn_seq_kernel, tm=tm, tf=tf),
        grid=(bsz, t_len // tm),
        in_specs=[pl.BlockSpec((1, tm, D_MODEL), lambda b, t: (b, t, 0)),
                  pl.BlockSpec((1, FFN_W - 1, 2 * D_FF), lambda b, t: (b, 0, 0)),
                  _layer_block((D_MODEL, 2 * D_FF), i),
                  _layer_block((FFN_W, 2 * D_FF), i),
                  _layer_block((D_FF, D_MODEL), i),
                  _layer_block((1, D_MODEL), i),
                  _layer_block((1, D_MODEL), i)],
        out_specs=[pl.BlockSpec((1, tm, D_MODEL), lambda b, t: (b, t, 0)),
                   pl.BlockSpec((1, FFN_W - 1, 2 * D_FF), lambda b, t: (b, 0, 0))],
        out_shape=[jax.ShapeDtypeStruct((bsz, t_len, D_MODEL), F32),
                   jax.ShapeDtypeStruct((bsz, FFN_W - 1, 2 * D_FF), F32)],
        scratch_shapes=[pltpu.VMEM((D_FF // tf, tm + SUBLANES, tf), F32),
                        pltpu.VMEM((D_FF // tf, tm + SUBLANES, tf), F32)],
        compiler_params=_cparams(("arbitrary", "arbitrary")),
        name="ffn_seq",
    )(x, hist, w_up, w_conv, w_down, ln_g, ln_b)


def _gdn_gates(ba, a_row, dt_row):
    lane = lax.broadcasted_iota(jnp.int32, ba.shape, 1)
    return jnp.where(lane < GDN_H, _sigmoid(ba), a_row * _softplus(ba + dt_row))


def _qkv_post(c, q_ref, k_ref, v_ref, idx):
    c = _silu(c)
    for h in range(GDN_H):
        for ref, base in ((q_ref, 0), (k_ref, GDN_DIM)):
            a = c[:, base + h * GDN_DK:base + (h + 1) * GDN_DK]
            a = a * lax.rsqrt(jnp.sum(a * a, -1, keepdims=True) + RMS_EPS)
            ref[idx + (slice(h * GDN_DK, (h + 1) * GDN_DK),)] = a
    v_ref[idx + (slice(None),)] = c[:, 2 * GDN_DIM:]


def _proj_b_seq_kernel(x_ref, hist_ref, win_ref, wc_ref, arow_ref, dtrow_ref, k_ref, v_ref,
                       q_o, k_o, v_o, gb_o, zg_o, om_o, hist_o_ref, sbuf, kbd, vbd, *, tm):
    t = pl.program_id(1)
    lo = SUBLANES - (GDN_CONV_W - 1)
    nqkv = 3 * GDN_DIM

    @pl.when(t == 0)
    def _():
        sbuf[lo:SUBLANES, :] = hist_ref[0]
        kbd[...] = _head_blockdiag(k_ref[0])
        vbd[...] = _head_blockdiag(v_ref[0])

    ts = tm // MIX_SUBTILES
    subs = range(MIX_SUBTILES)
    rows = [slice(s * ts, (s + 1) * ts) for s in subs]
    hs = [_dot(x_ref[0, rows[s], :].astype(BF16), win_ref[...]) for s in subs]
    for s in subs:
        sbuf[SUBLANES + s * ts:SUBLANES + (s + 1) * ts, :] = hs[s][:, :nqkv]
    wc = wc_ref[...]
    qm0 = nqkv + GDN_DIM
    sc = []
    for s in subs:
        conv = wc[GDN_CONV_W - 1:GDN_CONV_W] * hs[s][:, :nqkv]
        for j in range(GDN_CONV_W - 1):
            conv = conv + wc[j:j + 1] * sbuf[lo + j + s * ts:lo + j + (s + 1) * ts, :]
        sc.append(_dot(hs[s][:, qm0:qm0 + XDIM].astype(BF16), kbd[...]) * (XD ** -0.5))
        _qkv_post(conv, q_o, k_o, v_o, (0, rows[s]))
        zg_o[0, rows[s], :] = _silu(hs[s][:, nqkv:qm0])
        gb_o[0, rows[s], :] = _gdn_gates(hs[s][:, qm0 + XDIM:], arow_ref[...], dtrow_ref[...])
    last = sbuf[tm + lo:tm + SUBLANES, :]
    sbuf[lo:SUBLANES, :] = last
    hist_o_ref[0] = last
    ps = [jnp.concatenate([_softmax_rows(sc[s][:, h * N_MEM:(h + 1) * N_MEM]).astype(BF16)
                           for h in range(XH)], axis=-1) for s in subs]
    for s in subs:
        om_o[0, rows[s], :] = _dot_nt(ps[s], vbd[...]).astype(BF16)


def _proj_b_seq(x, hist, w_in, w_conv, a_row, dt_row, mem_k, mem_v, *, j, i):
    bsz, t_len, _ = x.shape
    tm = min(MIX_TILE, t_len)
    tile = lambda n: pl.BlockSpec((1, tm, n), lambda b, t: (b, t, 0))
    f32o = lambda n: jax.ShapeDtypeStruct((bsz, t_len, n), F32)
    mem = pl.BlockSpec((None, 1, XDIM, N_MEM), lambda b, t: (i, b, 0, 0))
    return pl.pallas_call(
        functools.partial(_proj_b_seq_kernel, tm=tm),
        grid=(bsz, t_len // tm),
        in_specs=[tile(D_MODEL),
                  pl.BlockSpec((1, GDN_CONV_W - 1, 3 * GDN_DIM), lambda b, t: (b, 0, 0)),
                  _layer_block((D_MODEL, W_B_COLS), j),
                  _layer_block((GDN_CONV_W, 3 * GDN_DIM), j),
                  _layer_block((1, GB_LANES), j),
                  _layer_block((1, GB_LANES), j),
                  mem, mem],
        out_specs=[tile(GDN_DIM), tile(GDN_DIM), tile(GDN_DIM), tile(GB_LANES), tile(GDN_DIM),
                   tile(XDIM),
                   pl.BlockSpec((1, GDN_CONV_W - 1, 3 * GDN_DIM), lambda b, t: (b, 0, 0))],
        out_shape=[f32o(GDN_DIM), f32o(GDN_DIM), f32o(GDN_DIM), f32o(GB_LANES), f32o(GDN_DIM),
                   jax.ShapeDtypeStruct((bsz, t_len, XDIM), BF16),
                   jax.ShapeDtypeStruct((bsz, GDN_CONV_W - 1, 3 * GDN_DIM), F32)],
        scratch_shapes=[pltpu.VMEM((tm + SUBLANES, 3 * GDN_DIM), F32),
                        pltpu.VMEM((XDIM, XH * N_MEM), BF16),
                        pltpu.VMEM((XDIM, XH * N_MEM), BF16)],
        compiler_params=_cparams(("arbitrary", "arbitrary")),
        name="proj_b_seq",
    )(x, hist, w_in, w_conv, a_row, dt_row, mem_k, mem_v)


def _gdn_scan_kernel(q_ref, k_ref, v_ref, gb_ref, zg_ref, nw_ref, s0_ref, prev_ref, o_ref, s_o_ref,
                     s_scr, *, bb, nc, c, n_prev):
    t = pl.program_id(1)
    tc = nc * c

    @pl.when(t == 0)
    def _():
        s_scr[...] = s0_ref[0]
        for n in range(n_prev):
            s_o_ref[n] = prev_ref[n]

    ri = lax.broadcasted_iota(jnp.int32, (c, c), 0)
    ci = lax.broadcasted_iota(jnp.int32, (c, c), 1)
    tril = ri >= ci
    strict = ri > ci
    rt = lax.broadcasted_iota(jnp.int32, (tc, tc), 0)
    ct = lax.broadcasted_iota(jnp.int32, (tc, tc), 1)
    tril_chunks = ((rt >= ct) & ((rt // c) == (ct // c))).astype(BF16)
    head_sel = (lax.broadcasted_iota(jnp.int32, (SUBLANES, GB_LANES), 1)
                == lax.broadcasted_iota(jnp.int32, (SUBLANES, GB_LANES), 0) + GDN_H).astype(BF16)
    nw = nw_ref[...]

    def split3(x):
        a = x.astype(BF16)
        r = x - a.astype(F32)
        b = r.astype(BF16)
        return a, b, (r - b.astype(F32)).astype(BF16)

    gcs = []
    for i in range(bb):
        gc_all = sum(_dot(tril_chunks, p) for p in split3(gb_ref[i]))
        gc_t = sum(_dot_nt(head_sel, p) for p in split3(gc_all))
        gcs.append((gc_all, gc_t))

    probs = [(i, ic, h) for i in range(bb) for ic in range(nc) for h in range(GDN_H)]
    st = {}
    for p in probs:
        i, ic, h = p
        rows = slice(ic * c, (ic + 1) * c)
        hs = slice(h * GDN_DK, (h + 1) * GDN_DK)
        gc_all, gc_t = gcs[i]
        q = q_ref[i, rows, hs] * (GDN_DK ** -0.5)
        k = k_ref[i, rows, hs]
        gcol = gc_all[rows, GDN_H + h:GDN_H + h + 1]
        grow = gc_t[h:h + 1, rows]
        glast = gc_all[(ic + 1) * c - 1:(ic + 1) * c, GDN_H + h:GDN_H + h + 1]
        egc = jnp.exp(gcol)
        kb = k * gb_ref[i, rows, h:h + 1]
        st[p] = dict(
            decay=jnp.exp(jnp.where(tril, gcol - grow, -jnp.inf)),
            aq=_dot_nt(jnp.concatenate([kb, q], axis=0).astype(BF16), k.astype(BF16)),
            sol=jnp.concatenate([v_ref[i, rows, hs] * gb_ref[i, rows, h:h + 1], kb * egc], axis=-1),
            qd=(q * egc).astype(BF16),
            k_dec=(k * jnp.exp(glast - gcol)).astype(BF16),
            e_last=jnp.exp(glast))
    sb = min(SOLVE_BLOCK, c)
    nblk = c // sb
    same_blk = (ri // sb) == (ci // sb)
    lane_sb = lax.broadcasted_iota(jnp.int32, (sb, c), 1)
    eye_ss = ((lane_sb % sb) == lax.broadcasted_iota(jnp.int32, (sb, c), 0)).astype(F32)
    blk_rows = [slice(n * sb, (n + 1) * sb) for n in range(nblk)]
    tile_rows = lambda a: jnp.concatenate([a] * nblk, axis=0) if nblk > 1 else a

    def split2(x):
        hi = x.astype(BF16)
        return jnp.concatenate([hi, (x - hi.astype(F32)).astype(BF16)], axis=-1)

    def fold2(y):
        n = y.shape[-1] // 2
        return y[:, :n] + y[:, n:]

    for p in probs:
        d = st[p]
        d["lmat"] = jnp.where(strict, d["aq"][:c] * d["decay"], 0.0)
        d["qk"] = (d["aq"][c:] * d["decay"]).astype(BF16)
        d["pss"] = sum(jnp.where(lane_sb // sb == n, d["lmat"][blk_rows[n]], 0.0) for n in range(nblk))
        d["pbd"] = jnp.where(same_blk, d["lmat"], 0.0).astype(BF16)
        d["tss"] = eye_ss - d["pss"]
    for _ in range(max(sb.bit_length() - 2, 0)):
        for p in probs:
            st[p]["pss"] = _dot(st[p]["pss"].astype(BF16), st[p]["pbd"])
        for p in probs:
            d = st[p]
            d["pbd"] = jnp.where(same_blk, tile_rows(d["pss"]), 0.0).astype(BF16)
            d["tss"] = d["tss"] + _dot(d["tss"].astype(BF16), d["pbd"])
    zero_blk = jnp.zeros((sb, 4 * GDN_DV), BF16)
    for p in probs:
        d = st[p]
        d["tbd"] = jnp.where(same_blk, tile_rows(d["tss"]), 0.0).astype(BF16)
        d["x2"] = []
        d["x"] = []
    for n in range(nblk):
        for p in probs:
            d = st[p]
            z = d["sol"][blk_rows[n]]
            if n > 0:
                lrow = jnp.where(lane_sb < n * sb, d["lmat"][blk_rows[n]], 0.0).astype(BF16)
                z = z - fold2(_dot(lrow, jnp.concatenate(d["x2"] + [zero_blk] * (nblk - n), axis=0)))
            d["y"] = jnp.concatenate([zero_blk] * n + [split2(z)] + [zero_blk] * (nblk - n - 1), axis=0)
        for p in probs:
            d = st[p]
            x = fold2(_dot(d["tbd"][blk_rows[n]], d["y"]))
            d["x"].append(x)
            if n + 1 < nblk:
                d["x2"].append(split2(x))
    for p in probs:
        d = st[p]
        sol = jnp.concatenate(d["x"], axis=0) if nblk > 1 else d["x"][0]
        d["wq"] = jnp.concatenate([sol[:, GDN_DV:].astype(BF16), d["qd"]], axis=0)
        d["u"] = sol[:, :GDN_DV]

    for ic in range(nc):
        rows = slice(ic * c, (ic + 1) * c)
        seqs = [(i, h) for i in range(bb) for h in range(GDN_H)]
        s_old = {ih: s_scr[ih[0], ih[1]] for ih in seqs}
        ws = {ih: _dot(st[ih[0], ic, ih[1]]["wq"], s_old[ih].astype(BF16)) for ih in seqs}
        vb = {ih: (st[ih[0], ic, ih[1]]["u"] - ws[ih][:c]).astype(BF16) for ih in seqs}
        for ih in seqs:
            d = st[ih[0], ic, ih[1]]
            s_scr[ih[0], ih[1]] = s_old[ih] * d["e_last"] + _dot_tn(d["k_dec"], vb[ih])
        for ih in seqs:
            i, h = ih
            hs = slice(h * GDN_DK, (h + 1) * GDN_DK)
            o = ws[ih][c:] + _dot(st[i, ic, h]["qk"], vb[ih])
            o = o * lax.rsqrt(jnp.mean(o * o, -1, keepdims=True) + RMS_EPS)
            o_ref[i, rows, hs] = (o * nw * zg_ref[i, rows, hs]).astype(o_ref.dtype)
    s_o_ref[n_prev] = s_scr[...]


def _gdn_scan(q, k, v, gb, zg, norm_w, s0_all, prev, *, j, layer, c, bb, nc):
    bsz, t_len, _ = q.shape
    tc = nc * c
    n_prev = 0 if prev is None else prev.shape[0]
    tile = lambda n: pl.BlockSpec((bb, tc, n), lambda b, t: (b, t, 0))
    st = lambda n: pl.BlockSpec((n, bb, GDN_H, GDN_DK, GDN_DV), lambda b, t: (0, b, 0, 0, 0))
    st_in = pl.BlockSpec((1, bb, GDN_H, GDN_DK, GDN_DV), lambda b, t: (layer, b, 0, 0, 0))
    if prev is None:
        prev, prev_spec = s0_all, pl.BlockSpec(memory_space=pl.ANY)
    else:
        prev_spec = st(n_prev)
    return pl.pallas_call(
        functools.partial(_gdn_scan_kernel, bb=bb, nc=nc, c=c, n_prev=n_prev),
        grid=(bsz // bb, t_len // tc),
        in_specs=[tile(GDN_DIM), tile(GDN_DIM), tile(GDN_DIM), tile(GB_LANES), tile(GDN_DIM),
                  _layer_block((1, GDN_DV), j), st_in, prev_spec],
        out_specs=[tile(GDN_DIM), st(n_prev + 1)],
        out_shape=[jax.ShapeDtypeStruct((bsz, t_len, GDN_DIM), BF16),
                   jax.ShapeDtypeStruct((n_prev + 1, bsz, GDN_H, GDN_DK, GDN_DV), F32)],
        scratch_shapes=[pltpu.VMEM((bb, GDN_H, GDN_DK, GDN_DV), F32)],
        compiler_params=_cparams(("arbitrary", "arbitrary")),
        name="gdn_scan",
    )(q, k, v, gb, zg, norm_w, s0_all, prev)


def _out_ln_kernel(a1_ref, a2_ref, w_ref, x_ref, g_ref, b_ref, o_ref):
    n1 = a1_ref.shape[-1]
    mix = (_dot(a1_ref[...].astype(BF16), w_ref[:n1, :])
           + _dot(a2_ref[...].astype(BF16), w_ref[n1:, :]))
    o_ref[...] = _layer_norm(ALPHA * x_ref[...] + mix, g_ref[...], b_ref[...])


def _out_ln(a1, a2, w, x, ln_g, ln_b, *, j, i):
    rows = x.shape[0]
    tr = min(ROW_TILE, rows)
    n1, n2 = a1.shape[1], a2.shape[1]
    tile = lambda n: pl.BlockSpec((tr, n), lambda r: (r, 0))
    return pl.pallas_call(
        _out_ln_kernel,
        grid=(rows // tr,),
        in_specs=[tile(n1), tile(n2), _layer_block((n1 + n2, D_MODEL), j), tile(D_MODEL),
                  _layer_block((1, D_MODEL), i), _layer_block((1, D_MODEL), i)],
        out_specs=tile(D_MODEL),
        out_shape=jax.ShapeDtypeStruct((rows, D_MODEL), F32),
        compiler_params=_cparams(("arbitrary",)),
        name="out_ln",
    )(a1, a2, w, x, ln_g, ln_b)


def _mix_a_front_sample_kernel(x_ref, hist_ref, win_ref, wc_ref, y_o, qm_o, hist_o, *, t_len, bsz):
    h = _dot(x_ref[...].astype(BF16), win_ref[...])
    u = h[:, 2 * SC_DIM:3 * SC_DIM] * h[:, :SC_DIM]
    slabs = [hist_ref[:, j, :] for j in range(SC_W - 1)]
    slabs += [u[t * bsz:(t + 1) * bsz] for t in range(t_len)]
    wc = wc_ref[...]
    for t in range(t_len):
        conv = wc[0:1] * slabs[t]
        for j in range(1, SC_W):
            conv = conv + wc[j:j + 1] * slabs[t + j]
        y_o[t * bsz:(t + 1) * bsz, :] = (h[t * bsz:(t + 1) * bsz, SC_DIM:2 * SC_DIM] * conv).astype(BF16)
    for j in range(SC_W - 1):
        hist_o[:, j, :] = slabs[t_len + j]
    qm_o[...] = h[:, 3 * SC_DIM:]


def _whole(shape):
    nd = len(shape)
    return pl.BlockSpec(tuple(shape), lambda *_: (0,) * nd)


def _mix_a_front_sample(x2d, hist, w_in, w_conv, *, j, t_len, bsz):
    rows = x2d.shape[0]
    out_shape = [jax.ShapeDtypeStruct((rows, SC_DIM), BF16),
                 jax.ShapeDtypeStruct((rows, XDIM), F32),
                 jax.ShapeDtypeStruct((bsz, SC_W - 1, SC_DIM), F32)]
    return pl.pallas_call(
        functools.partial(_mix_a_front_sample_kernel, t_len=t_len, bsz=bsz),
        grid=(1,),
        in_specs=[_whole(x2d.shape), _layer_block(hist.shape[1:], j), _layer_block(w_in.shape[1:], j),
                  _layer_block(w_conv.shape[1:], j)],
        out_specs=[_whole(s.shape) for s in out_shape],
        out_shape=out_shape,
        compiler_params=_cparams(("arbitrary",)),
        name="mix_a_front_sample",
    )(x2d, hist, w_in, w_conv)


def _proj_b_front_sample_kernel(x_ref, hist_ref, win_ref, wc_ref, arow_ref, dtrow_ref,
                                q_o, k_o, v_o, gb_o, zg_o, qm_o, hist_o, *, t_len, t_pad, bsz):
    nqkv = 3 * GDN_DIM
    h = _dot(x_ref[...].astype(BF16), win_ref[...])
    slabs = [hist_ref[j] for j in range(GDN_CONV_W - 1)]
    slabs += [h[t * bsz:(t + 1) * bsz, :nqkv] for t in range(t_len)]
    wc = wc_ref[...]
    for t in range(t_len):
        conv = wc[0:1] * slabs[t]
        for j in range(1, GDN_CONV_W):
            conv = conv + wc[j:j + 1] * slabs[t + j]
        _qkv_post(conv, q_o, k_o, v_o, (slice(None), t))
    for j in range(GDN_CONV_W - 1):
        hist_o[j] = slabs[t_len + j]
    zg = _silu(h[:, nqkv:nqkv + GDN_DIM])
    gb = _gdn_gates(h[:, nqkv + GDN_DIM + XDIM:], arow_ref[...], dtrow_ref[...])
    for t in range(t_len):
        zg_o[:, t, :] = zg[t * bsz:(t + 1) * bsz]
        gb_o[:, t, :] = gb[t * bsz:(t + 1) * bsz]
    for ref in (q_o, k_o, v_o, gb_o, zg_o):
        for t in range(t_len, t_pad):
            ref[:, t, :] = jnp.zeros((bsz, ref.shape[-1]), F32)
    qm_o[...] = h[:, nqkv + GDN_DIM:nqkv + GDN_DIM + XDIM]


def _proj_b_front_sample(x2d, hist, w_in, w_conv, a_row, dt_row, *, j, t_len, t_pad, bsz):
    rows = x2d.shape[0]
    bmaj = lambda n: jax.ShapeDtypeStruct((bsz, t_pad, n), F32)
    out_shape = [bmaj(GDN_DIM), bmaj(GDN_DIM), bmaj(GDN_DIM), bmaj(GB_LANES), bmaj(GDN_DIM),
                 jax.ShapeDtypeStruct((rows, XDIM), F32),
                 jax.ShapeDtypeStruct((GDN_CONV_W - 1, bsz, 3 * GDN_DIM), F32)]
    return pl.pallas_call(
        functools.partial(_proj_b_front_sample_kernel, t_len=t_len, t_pad=t_pad, bsz=bsz),
        grid=(1,),
        in_specs=[_whole(x2d.shape), _layer_block(hist.shape[1:], j), _layer_block(w_in.shape[1:], j),
                  _layer_block(w_conv.shape[1:], j), _layer_block((1, GB_LANES), j),
                  _layer_block((1, GB_LANES), j)],
        out_specs=[_whole(s.shape) for s in out_shape],
        out_shape=out_shape,
        compiler_params=_cparams(("arbitrary",)),
        name="proj_b_front_sample",
    )(x2d, hist, w_in, w_conv, a_row, dt_row)


def _ffn_sample_kernel(x_ref, hist_ref, w_ref, c_ref, wdn_ref, g_ref, b_ref, o_ref, hist_o,
                       acc, gate, *, t_len, bsz, nj):
    s = pl.program_id(0)
    h = _dot(x_ref[...].astype(BF16), w_ref[...])
    slabs = [hist_ref[:, r, :] for r in range(FFN_W - 1)]
    slabs += [h[t * bsz:(t + 1) * bsz] for t in range(t_len)]
    wc = c_ref[...]
    outs = []
    for t in range(t_len):
        c = wc[0:1] * slabs[t]
        for w in range(1, FFN_W):
            c = c + wc[w:w + 1] * slabs[t + w]
        outs.append(c)
    for r in range(FFN_W - 1):
        hist_o[:, r, :] = slabs[t_len + r]
    conv = jnp.concatenate(outs, axis=0)

    @pl.when(s < nj)
    def _():
        gate[s] = _silu(conv)

    @pl.when(s >= nj)
    def _():
        d = _dot((gate[s - nj] * conv).astype(BF16), wdn_ref[...])

        @pl.when(s == nj)
        def _():
            acc[...] = d

        @pl.when(s > nj)
        def _():
            acc[...] += d

    @pl.when(s == 2 * nj - 1)
    def _():
        o_ref[...] = _layer_norm(ALPHA * x_ref[...] + acc[...], g_ref[...], b_ref[...])


def _ffn_sample(x2d, hist, w_up, w_conv, w_down, ln_g, ln_b, *, i, t_len, bsz):
    rows = x2d.shape[0]
    tf = FF_TILE_SAMPLE
    nj = D_FF // tf
    return pl.pallas_call(
        functools.partial(_ffn_sample_kernel, t_len=t_len, bsz=bsz, nj=nj),
        grid=(2 * nj,),
        in_specs=[_resident((rows, D_MODEL)),
                  pl.BlockSpec((None, bsz, FFN_W - 1, tf), lambda s: (i, 0, 0, s)),
                  pl.BlockSpec((None, D_MODEL, tf), lambda s: (i, 0, s)),
                  pl.BlockSpec((None, FFN_W, tf), lambda s: (i, 0, s)),
                  pl.BlockSpec((None, tf, D_MODEL), lambda s: (i, jnp.maximum(s - nj, 0), 0)),
                  _layer_block((1, D_MODEL), i), _layer_block((1, D_MODEL), i)],
        out_specs=[pl.BlockSpec((rows, D_MODEL), lambda s: (0, 0)),
                   pl.BlockSpec((bsz, FFN_W - 1, tf), lambda s: (0, 0, s))],
        out_shape=[jax.ShapeDtypeStruct((rows, D_MODEL), F32),
                   jax.ShapeDtypeStruct((bsz, FFN_W - 1, 2 * D_FF), F32)],
        scratch_shapes=[pltpu.VMEM((rows, D_MODEL), F32), pltpu.VMEM((nj, rows, tf), F32)],
        compiler_params=_cparams(("arbitrary",)),
        name="ffn_sample",
    )(x2d, hist, w_up, w_conv, w_down, ln_g, ln_b)


def _prep_weights(w_in_a, w_out_a, w_in_b, a_log, dt_bias, w_out_b, w_mem_kv, w_up, w_down):
    nqkvz = 4 * GDN_DIM
    w_b = jnp.concatenate(
        [w_in_b[..., :nqkvz], w_in_b[..., nqkvz + 2 * GDN_H:],
         jnp.pad(w_in_b[..., nqkvz:nqkvz + 2 * GDN_H], ((0, 0), (0, 0), (0, GB_LANES - 2 * GDN_H)))],
        axis=-1).astype(BF16)
    pad = GB_LANES - 2 * GDN_H
    a_row = jnp.pad(-jnp.exp(a_log.astype(F32)), ((0, 0), (GDN_H, pad)))[:, None, :]
    dt_row = jnp.pad(dt_bias.astype(F32), ((0, 0), (GDN_H, pad)))[:, None, :]
    return dict(w_in_a=w_in_a.astype(BF16), w_out_a=w_out_a.astype(BF16), w_in_b=w_b,
                w_out_b=w_out_b.astype(BF16), w_mem_kv=w_mem_kv.astype(BF16),
                w_up=w_up.astype(BF16), w_down=w_down.astype(BF16), a_row=a_row, dt_row=dt_row)


def _trunk_prompt(x, mem_k, mem_v, wts, conv_a, conv_b, gdn_norm_w, ln1_g, ln1_b, ln2_g, ln2_b,
                  w_conv_ffn):
    bsz, t_len, _ = x.shape
    z = lambda *s: jnp.zeros(s, F32)
    new_sc, new_gc, new_gs, new_ffn = [], [], None, []
    for i in range(DEPTH):
        j = i // 2
        if i % 2 == 0:
            x, hs = _mix_a_seq(x, z(bsz, SC_W - 1, SC_DIM), wts["w_in_a"], conv_a, wts["w_out_a"],
                               mem_k, mem_v, ln1_g, ln1_b, j=j, i=i)
            new_sc.append(hs)
        else:
            q, k, v, gb, zg, om, hg = _proj_b_seq(
                x, z(bsz, GDN_CONV_W - 1, 3 * GDN_DIM), wts["w_in_b"], conv_b,
                wts["a_row"], wts["dt_row"], mem_k, mem_v, j=j, i=i)
            o, new_gs = _gdn_scan(q, k, v, gb, zg, gdn_norm_w, z(1, bsz, GDN_H, GDN_DK, GDN_DV),
                                  new_gs, j=j, layer=0, c=GDN_CHUNK, bb=SCAN_SEQS, nc=SCAN_CHUNKS)
            rows = bsz * t_len
            x = _out_ln(o.reshape(rows, GDN_DIM), om.reshape(rows, XDIM), wts["w_out_b"],
                        x.reshape(rows, D_MODEL), ln1_g, ln1_b, j=j, i=i).reshape(bsz, t_len, D_MODEL)
            new_gc.append(hg)
        x, hf = _ffn_seq(x, z(bsz, FFN_W - 1, 2 * D_FF), wts["w_up"], w_conv_ffn, wts["w_down"],
                         ln2_g, ln2_b, i=i)
        new_ffn.append(hf)
    return x, jnp.stack(new_sc), jnp.stack(new_gc), new_gs, jnp.stack(new_ffn)


def _trunk_sample(x, mem_k, mem_v, sc_hist, gdn_hist, gdn_s, ffn_hist, wts, conv_a, conv_b,
                  gdn_norm_w, ln1_g, ln1_b, ln2_g, ln2_b, w_conv_ffn):
    bsz, t_len, _ = x.shape
    rows = bsz * t_len
    t_pad = SUBLANES
    x2 = jnp.transpose(x, (1, 0, 2)).reshape(rows, D_MODEL)
    new_sc, new_gc, new_gs, new_ffn = [], [], None, []
    mem_kt = jnp.transpose(mem_k, (0, 1, 3, 4, 2))
    mem_vt = jnp.transpose(mem_v, (0, 1, 3, 4, 2))
    gdn_hist = jnp.transpose(gdn_hist, (0, 2, 1, 3))
    for i in range(DEPTH):
        j = i // 2
        if i % 2 == 0:
            y, qm, hs = _mix_a_front_sample(x2, sc_hist, wts["w_in_a"], conv_a, j=j,
                                            t_len=t_len, bsz=bsz)
            om = _attn_sample(qm.reshape(t_len, bsz, XDIM), mem_kt, mem_vt, i).reshape(rows, XDIM)
            x2 = _out_ln(y, om, wts["w_out_a"], x2, ln1_g, ln1_b, j=j, i=i)
            new_sc.append(hs)
        else:
            q, k, v, gb, zg, qm, hg = _proj_b_front_sample(
                x2, gdn_hist, wts["w_in_b"], conv_b, wts["a_row"], wts["dt_row"], j=j,
                t_len=t_len, t_pad=t_pad, bsz=bsz)
            om = _attn_sample(qm.reshape(t_len, bsz, XDIM), mem_kt, mem_vt, i).reshape(rows, XDIM)
            o, new_gs = _gdn_scan(q, k, v, gb, zg, gdn_norm_w, gdn_s,
                                  new_gs, j=j, layer=j, c=t_pad, bb=SCAN_SEQS_SAMPLE, nc=1)
            o = jnp.transpose(o[:, :t_len], (1, 0, 2)).reshape(rows, GDN_DIM)
            x2 = _out_ln(o, om, wts["w_out_b"], x2, ln1_g, ln1_b, j=j, i=i)
            new_gc.append(hg)
        x2, hf = _ffn_sample(x2, ffn_hist, wts["w_up"], w_conv_ffn, wts["w_down"],
                             ln2_g, ln2_b, i=i, t_len=t_len, bsz=bsz)
        new_ffn.append(hf)
    y = jnp.transpose(x2.reshape(t_len, bsz, D_MODEL), (1, 0, 2))
    gc = jnp.transpose(jnp.stack(new_gc), (0, 2, 1, 3))
    return y, jnp.stack(new_sc), gc, new_gs, jnp.stack(new_ffn)


def kernel(x_prompt, x_sample, mem_prompt, cache_mem_k, cache_mem_v, state_shortconv, state_gdn_conv,
           state_gdn, state_ffn_conv, w_in_a, conv_a, w_out_a, w_in_b, conv_b, a_log, dt_bias,
           gdn_norm_w, w_out_b, w_mem_kv, ln1_g, ln1_b, ln2_g, ln2_b, w_up, w_conv_ffn, w_down):
    wts = _prep_weights(w_in_a, w_out_a, w_in_b, a_log, dt_bias, w_out_b, w_mem_kv, w_up, w_down)
    row3 = lambda a: a.reshape(a.shape[0], 1, a.shape[1])
    shared = (wts, conv_a, conv_b, row3(gdn_norm_w), row3(ln1_g), row3(ln1_b), row3(ln2_g),
              row3(ln2_b), w_conv_ffn)
    bsz = x_prompt.shape[0]
    k_t, v_t = _mem_kv(mem_prompt.reshape(bsz * N_MEM, D_MODEL), wts["w_mem_kv"], bsz=bsz)
    y_prompt, sc_p, gc_p, gs_p, ffn_p = _trunk_prompt(x_prompt, k_t, v_t, *shared)
    y_sample, sc_s, gc_s, gs_s, ffn_s = _trunk_sample(
        x_sample, cache_mem_k, cache_mem_v, state_shortconv, state_gdn_conv, state_gdn,
        state_ffn_conv, *shared)
    to_cache = lambda a: jnp.transpose(a.reshape(DEPTH, bsz, XH, XD, N_MEM), (0, 1, 4, 2, 3))
    return (y_prompt, y_sample, to_cache(k_t), to_cache(v_t),
            sc_p, gc_p, gs_p, ffn_p, sc_s, gc_s, gs_s, ffn_s)
```
